```python
import math, functools
import jax, jax.numpy as jnp
from jax import lax
import numpy as np

D_MODEL = 2048
BATCH = 1
SEQ = 8192
DEPTH = 1
DEC_BATCH = 32
DEC_SEQ = 1
PAST_LEN = 8192
PAGE_SIZE = 128

N_HEADS = D_MODEL // 256
HEAD_DIM = 64
ATTN_WIDTH = N_HEADS * 2 * HEAD_DIM
ATTN_SCALE = HEAD_DIM ** -0.5
Q_BLOCK = 128
SUBLN_EPS = 1e-5
POOL_WINDOWS = (2, 4, 8, 16)
POOL_WIDTH = D_MODEL // 2
POOL_GROUP = POOL_WIDTH // len(POOL_WINDOWS)
POOL_STATE = max(POOL_WINDOWS) - 1
N_GROUPS = 4
EXPERTS_PER_GROUP = 8
N_EXPERTS = N_GROUPS * EXPERTS_PER_GROUP
TOP_K = 2
EXPERT_FF = D_MODEL // 4
PLE_DIM = 256
N_IN = POOL_WIDTH + 3 * ATTN_WIDTH + 2 * D_MODEL
ALPHA = (2.0 * DEPTH) ** 0.25
BETA = (8.0 * DEPTH) ** -0.25
LN_EPS = 1e-5

kernel_name = 'hybrid_pool_diffattn_hmoe_decode_step'


def lambda_init(layer):
    return 0.8 - 0.6 * math.exp(-0.3 * layer)


def layer_norm(x, g, b):
    xf = x.astype(jnp.float32)
    mu = jnp.mean(xf, axis=-1, keepdims=True)
    var = jnp.mean(jnp.square(xf - mu), axis=-1, keepdims=True)
    return ((xf - mu) * lax.rsqrt(var + LN_EPS) * g + b).astype(x.dtype)


def pool_mix(u, prefix, start_pos, w_grp, scale):
    B, T, _ = u.shape
    P = prefix.shape[1]
    ext = jnp.concatenate([prefix.astype(u.dtype), u], axis=1)
    csum = jnp.pad(jnp.cumsum(ext.astype(jnp.float32), axis=1), ((0, 0), (1, 0), (0, 0)))
    pos = start_pos + jnp.arange(T)
    outs = []
    for g, w in enumerate(POOL_WINDOWS):
        c0, c1 = g * POOL_GROUP, (g + 1) * POOL_GROUP
        hi = csum[:, P + 1:P + 1 + T, c0:c1]
        lo = csum[:, P + 1 - w:P + 1 - w + T, c0:c1]
        cnt = jnp.minimum(pos + 1, w).astype(jnp.float32)[None, :, None]
        outs.append((hi - lo) / cnt - u[:, :, c0:c1].astype(jnp.float32))
    d = jnp.stack(outs, axis=2).astype(u.dtype)
    y = jnp.einsum('btgc,gce->btge', d, w_grp).reshape(B, T, POOL_WIDTH) * scale
    return y, ext[:, -P:]


def prompt_attend(q, k, v, lam):
    B, S = q.shape[:2]
    nb = S // Q_BLOCK
    qb = jnp.moveaxis(q.reshape(B, nb, Q_BLOCK, N_HEADS, 2, HEAD_DIM), 1, 0)
    kpos = jnp.arange(S)

    def block(args):
        qi, i = args
        s = jnp.einsum('bqhmd,bkhmd->bmhqk', qi, k, preferred_element_type=jnp.float32) * ATTN_SCALE
        qpos = i * Q_BLOCK + jnp.arange(Q_BLOCK)
        s = jnp.where(kpos[None, :] <= qpos[:, None], s, -jnp.inf)
        p = jax.nn.softmax(s, axis=-1)
        a = (p[:, 0] - lam * p[:, 1]).astype(v.dtype)
        return jnp.einsum('bhqk,bkhe->bqhe', a, v)

    o = lax.map(block, (qb, jnp.arange(nb)))
    return jnp.moveaxis(o, 0, 1).reshape(B, S, N_HEADS, 2 * HEAD_DIM)


def sample_attend(q, k, v, lam, k_past, v_past):
    T = q.shape[1]
    P = k_past.shape[1]
    s_past = jnp.einsum('bqhmd,bkhmd->bmhqk', q, k_past, preferred_element_type=jnp.float32) * ATTN_SCALE
    s_new = jnp.einsum('bqhmd,bkhmd->bmhqk', q, k, preferred_element_type=jnp.float32) * ATTN_SCALE
    s_new = jnp.where(jnp.tril(jnp.ones((T, T), dtype=bool)), s_new, -jnp.inf)
    p = jax.nn.softmax(jnp.concatenate([s_past, s_new], axis=-1), axis=-1)
    a = (p[:, 0] - lam * p[:, 1]).astype(v.dtype)
    return (jnp.einsum('bhqk,bkhe->bqhe', a[..., :P], v_past)
            + jnp.einsum('bhqk,bkhe->bqhe', a[..., P:], v))


def token_mixer(x, attend, pool_prefix, start_pos, lam0, w_in, pool_w, pool_scale,
                lam_q1, lam_k1, lam_q2, lam_k2, subln_g, w_bpool, w_battn, w_out):
    B, T, _ = x.shape
    z = x @ w_in
    o1 = POOL_WIDTH
    o2 = o1 + ATTN_WIDTH
    o3 = o2 + ATTN_WIDTH
    o4 = o3 + ATTN_WIDTH
    u = z[..., :o1]
    q = z[..., o1:o2].reshape(B, T, N_HEADS, 2, HEAD_DIM)
    k = z[..., o2:o3].reshape(B, T, N_HEADS, 2, HEAD_DIM)
    v = z[..., o3:o4].reshape(B, T, N_HEADS, 2 * HEAD_DIM)
    gl = z[..., o4:]
    pool_y, pool_state = pool_mix(u, pool_prefix, start_pos, pool_w, pool_scale)
    lam = (jnp.exp(jnp.sum(lam_q1.astype(jnp.float32) * lam_k1.astype(jnp.float32)))
           - jnp.exp(jnp.sum(lam_q2.astype(jnp.float32) * lam_k2.astype(jnp.float32))) + lam0)
    of = attend(q, k, v, lam).astype(jnp.float32)
    of = of * lax.rsqrt(jnp.mean(jnp.square(of), axis=-1, keepdims=True) + SUBLN_EPS) * subln_g * (1.0 - lam0)
    o = of.reshape(B, T, ATTN_WIDTH).astype(x.dtype)
    gates = jax.nn.sigmoid(gl.astype(jnp.float32)).astype(x.dtype)
    merged = gates[..., :D_MODEL] * (pool_y @ w_bpool) + gates[..., D_MODEL:] * (o @ w_battn)
    return merged @ w_out, k, v, pool_state


def hier_moe(x, w_rg, b_rg, w_re, b_re, w_gate, w_up, w_down):
    B, T, D = x.shape
    xf = x.reshape(-1, D)
    g_prob = jax.nn.softmax((xf @ w_rg + b_rg).astype(jnp.float32), axis=-1)
    g_idx = jnp.argmax(g_prob, axis=-1)
    g_w = jnp.take_along_axis(g_prob, g_idx[:, None], axis=1)[:, 0]
    e_logit = jnp.einsum('nd,gde->nge', xf, w_re) + b_re
    sel = jnp.take_along_axis(e_logit, g_idx[:, None, None], axis=1)[:, 0].astype(jnp.float32)
    top_v, top_i = lax.top_k(sel, TOP_K)
    top_w = jax.nn.softmax(top_v, axis=-1) * g_w[:, None]
    eid = g_idx[:, None] * EXPERTS_PER_GROUP + top_i
    combine = jnp.sum(jax.nn.one_hot(eid, N_EXPERTS, dtype=jnp.float32) * top_w[..., None], axis=1).astype(x.dtype)
    out = jnp.zeros_like(xf)
    for e in range(N_EXPERTS):
        h = jax.nn.silu(xf @ w_gate[e]) * (xf @ w_up[e])
        out = out + combine[:, e:e + 1] * (h @ w_down[e])
    return out.reshape(B, T, D)


def decoder_layer(x, pe, attend, pool_prefix, start_pos, lam0, mix_w, ln1, ffn_w, ple_w, ln2):
    mix, k, v, pstate = token_mixer(x, attend, pool_prefix, start_pos, lam0, *mix_w)
    x1 = layer_norm(ALPHA * x + mix, *ln1)
    ch = hier_moe(x1, *ffn_w)
    ple_proj, ple_gw, ple_gb = ple_w
    ple = jax.nn.sigmoid((x1 @ ple_gw + ple_gb).astype(jnp.float32)).astype(x.dtype) * (pe.astype(x.dtype) @ ple_proj)
    x2 = layer_norm(ALPHA * x1 + ch + ple, *ln2)
    return x2, k, v, pstate


def setup_inputs(seed: int = 0) -> dict:
    key = jax.random.key(seed)
    ks = jax.random.split(key, 40)
    f32 = jnp.float32
    n_pages = PAST_LEN // PAGE_SIZE
    n_pool = (DEC_BATCH * n_pages * 5) // 4

    def nrm(k, shape, s):
        return jax.random.normal(k, shape, f32) * s

    col_scale = jnp.concatenate([jnp.full((POOL_WIDTH,), BETA, f32), jnp.ones((2 * ATTN_WIDTH,), f32),
                                 jnp.full((ATTN_WIDTH,), BETA, f32), jnp.ones((2 * D_MODEL,), f32)])
    return {
        'x_prompt': nrm(ks[0], (BATCH, SEQ, D_MODEL), 1.0),
        'x_sample': nrm(ks[1], (DEC_BATCH, DEC_SEQ, D_MODEL), 1.0),
        'p_prompt': nrm(ks[2], (DEPTH, BATCH, SEQ, PLE_DIM), 1.0),
        'p_sample': nrm(ks[3], (DEPTH, DEC_BATCH, DEC_SEQ, PLE_DIM), 1.0),
        'cache_k': nrm(ks[4], (DEPTH, n_pool, PAGE_SIZE, N_HEADS, 2, HEAD_DIM), 1.0),
        'cache_v': nrm(ks[5], (DEPTH, n_pool, PAGE_SIZE, N_HEADS, 2 * HEAD_DIM), BETA),
        'state_pool': nrm(ks[6], (DEPTH, DEC_BATCH, POOL_STATE, POOL_WIDTH), BETA),
        'page_table': jax.random.permutation(ks[7], n_pool)[:DEC_BATCH * n_pages].reshape(DEC_BATCH, n_pages).astype(jnp.int32),
        'w_in': nrm(ks[8], (DEPTH, D_MODEL, N_IN), D_MODEL ** -0.5) * col_scale,
        'pool_w': nrm(ks[9], (DEPTH, len(POOL_WINDOWS), POOL_GROUP, POOL_GROUP), POOL_GROUP ** -0.5),
        'pool_scale': 1.0 + nrm(ks[10], (DEPTH, POOL_WIDTH), 0.02),
        'lam_q1': nrm(ks[11], (DEPTH, HEAD_DIM), 0.1),
        'lam_k1': nrm(ks[12], (DEPTH, HEAD_DIM), 0.1),
        'lam_q2': nrm(ks[13], (DEPTH, HEAD_DIM), 0.1),
        'lam_k2': nrm(ks[14], (DEPTH, HEAD_DIM), 0.1),
        'subln_g': 1.0 + nrm(ks[15], (DEPTH, 2 * HEAD_DIM), 0.02),
        'w_branch_pool': nrm(ks[16], (DEPTH, POOL_WIDTH, D_MODEL), POOL_WIDTH ** -0.5),
        'w_branch_attn': nrm(ks[17], (DEPTH, ATTN_WIDTH, D_MODEL), ATTN_WIDTH ** -0.5),
        'w_out': nrm(ks[18], (DEPTH, D_MODEL, D_MODEL), BETA * D_MODEL ** -0.5),
        'ln1_g': 1.0 + nrm(ks[19], (DEPTH, D_MODEL), 0.02),
        'ln1_b': nrm(ks[20], (DEPTH, D_MODEL), 0.02),
        'router_group_w': nrm(ks[21], (DEPTH, D_MODEL, N_GROUPS), D_MODEL ** -0.5),
        'router_group_b': nrm(ks[22], (DEPTH, N_GROUPS), 0.01),
        'router_expert_w': nrm(ks[23], (DEPTH, N_GROUPS, D_MODEL, EXPERTS_PER_GROUP), D_MODEL ** -0.5),
        'router_expert_b': nrm(ks[24], (DEPTH, N_GROUPS, EXPERTS_PER_GROUP), 0.01),
        'expert_gate': nrm(ks[25], (DEPTH, N_EXPERTS, D_MODEL, EXPERT_FF), BETA * D_MODEL ** -0.5),
        'expert_up': nrm(ks[26], (DEPTH, N_EXPERTS, D_MODEL, EXPERT_FF), BETA * D_MODEL ** -0.5),
        'expert_down': nrm(ks[27], (DEPTH, N_EXPERTS, EXPERT_FF, D_MODEL), BETA * EXPERT_FF ** -0.5),
        'ple_proj': nrm(ks[28], (DEPTH, PLE_DIM, D_MODEL), BETA * PLE_DIM ** -0.5),
        'ple_gate_w': nrm(ks[29], (DEPTH, D_MODEL, D_MODEL), D_MODEL ** -0.5),
        'ple_gate_b': nrm(ks[30], (DEPTH, D_MODEL), 0.02),
        'ln2_g': 1.0 + nrm(ks[31], (DEPTH, D_MODEL), 0.02),
        'ln2_b': nrm(ks[32], (DEPTH, D_MODEL), 0.02),
    }


def reference(x_prompt, x_sample, p_prompt, p_sample, cache_k, cache_v, state_pool, page_table,
              w_in, pool_w, pool_scale, lam_q1, lam_k1, lam_q2, lam_k2, subln_g,
              w_branch_pool, w_branch_attn, w_out, ln1_g, ln1_b,
              router_group_w, router_group_b, router_expert_w, router_expert_b,
              expert_gate, expert_up, expert_down, ple_proj, ple_gate_w, ple_gate_b, ln2_g, ln2_b):
    hp, hs = x_prompt, x_sample
    n_seq = page_table.shape[0]
    kp_l, vp_l, sp_l, ks_l, vs_l, ss_l = [], [], [], [], [], []
    for l in range(DEPTH):
        lam0 = lambda_init(l)
        mix_w = (w_in[l], pool_w[l], pool_scale[l], lam_q1[l], lam_k1[l], lam_q2[l], lam_k2[l],
                 subln_g[l], w_branch_pool[l], w_branch_attn[l], w_out[l])
        ln1 = (ln1_g[l], ln1_b[l])
        ffn_w = (router_group_w[l], router_group_b[l], router_expert_w[l], router_expert_b[l],
                 expert_gate[l], expert_up[l], expert_down[l])
        ple_w = (ple_proj[l], ple_gate_w[l], ple_gate_b[l])
        ln2 = (ln2_g[l], ln2_b[l])
        prefix = jnp.zeros((hp.shape[0], POOL_STATE, POOL_WIDTH), hp.dtype)
        hp, kp, vp, sp = decoder_layer(hp, p_prompt[l], prompt_attend, prefix, 0, lam0,
                                       mix_w, ln1, ffn_w, ple_w, ln2)
        k_past = cache_k[l][page_table].reshape(n_seq, -1, N_HEADS, 2, HEAD_DIM)
        v_past = cache_v[l][page_table].reshape(n_seq, -1, N_HEADS, 2 * HEAD_DIM)
        attend_s = functools.partial(sample_attend, k_past=k_past, v_past=v_past)
        hs, ks_, vs_, ss_ = decoder_layer(hs, p_sample[l], attend_s, state_pool[l], PAST_LEN, lam0,
                                          mix_w, ln1, ffn_w, ple_w, ln2)
        kp_l.append(kp); vp_l.append(vp); sp_l.append(sp)
        ks_l.append(ks_); vs_l.append(vs_); ss_l.append(ss_)
    return (hp, hs, jnp.stack(kp_l), jnp.stack(vp_l), jnp.stack(sp_l),
            jnp.stack(ks_l), jnp.stack(vs_l), jnp.stack(ss_l))
```

```python
import functools
import math

import jax
import jax.numpy as jnp
from jax import lax
from jax.experimental import pallas as pl
from jax.experimental.pallas import tpu as pltpu

F32 = jnp.float32
BF16 = jnp.bfloat16
I32 = jnp.int32

N_HEADS = 8
HEAD_DIM = 64
HEAD_W = 2 * HEAD_DIM
ATTN_SCALE = HEAD_DIM ** -0.5
SUBLN_EPS = 1e-5
LN_EPS = 1e-5
POOL_WINDOWS = (2, 4, 8, 16)
POOL_STATE = max(POOL_WINDOWS) - 1
N_GROUPS = 4
EXPERTS_PER_GROUP = 8
N_EXPERTS = N_GROUPS * EXPERTS_PER_GROUP
TOP_K = 2

LANE = 128
SUBLANE = 8
VMEM_LIMIT = 56 * 1024 * 1024

ROW_PAD = 256
TM_INPROJ = 384
TM_ROW = 256
TQ = 256
TR = 256
HALO = 16
CHUNK = 16


def _lambda_init(layer):
    return 0.8 - 0.6 * math.exp(-0.3 * layer)


def _cparams(sem, vmem=VMEM_LIMIT):
    return pltpu.CompilerParams(dimension_semantics=sem, vmem_limit_bytes=vmem)


def _const_spec(shape):
    nd = len(shape)
    return pl.BlockSpec(shape, lambda *_: (0,) * nd, pipeline_mode=pl.Buffered(1))


def _inproj_body(x_ref, w_ref, u_ref, q_ref, k_ref, kb_ref, v_ref, vb_ref, g_ref):
    j = pl.program_id(0)
    z = jnp.dot(x_ref[...], w_ref[...], preferred_element_type=F32)

    @pl.when(j == 0)
    def _():
        u_ref[...] = z

    @pl.when(j == 1)
    def _():
        q_ref[...] = (z * ATTN_SCALE).astype(BF16)

    @pl.when(j == 2)
    def _():
        k_ref[...] = z
        kb_ref[...] = z.astype(BF16)

    @pl.when(j == 3)
    def _():
        v_ref[...] = z
        vb_ref[...] = z.astype(BF16)

    @pl.when(j >= 4)
    def _():
        g_ref[...] = jax.nn.sigmoid(z).astype(BF16)


def _inproj(xb, w_in_b):
    ta, d = xb.shape
    n_in = w_in_b.shape[1]
    wn = N_HEADS * HEAD_W
    assert n_in == 4 * wn + 2 * d and (2 * d) % wn == 0
    tm = TM_INPROJ
    assert ta % tm == 0
    ni = ta // tm
    nj = n_in // wn

    def once(jo):
        return lambda j, i: (jnp.where(j < jo, 0, jnp.where(j == jo, i, ni - 1)), 0)

    blk = lambda jo: pl.BlockSpec((tm, wn), once(jo))
    g_spec = pl.BlockSpec((tm, wn), lambda j, i: (jnp.where(j < 4, 0, i), jnp.maximum(j - 4, 0)))
    return pl.pallas_call(
        _inproj_body,
        grid=(nj, ni),
        in_specs=[pl.BlockSpec((tm, d), lambda j, i: (i, 0)),
                  pl.BlockSpec((d, wn), lambda j, i: (0, j))],
        out_specs=[blk(0), blk(1), blk(2), blk(2), blk(3), blk(3), g_spec],
        out_shape=[jax.ShapeDtypeStruct((ta, wn), F32),
                   jax.ShapeDtypeStruct((ta, wn), BF16),
                   jax.ShapeDtypeStruct((ta, wn), F32),
                   jax.ShapeDtypeStruct((ta, wn), BF16),
                   jax.ShapeDtypeStruct((ta, wn), F32),
                   jax.ShapeDtypeStruct((ta, wn), BF16),
                   jax.ShapeDtypeStruct((ta, 2 * d), BF16)],
        compiler_params=_cparams(("arbitrary", "arbitrary")),
        name="inproj",
    )(xb, w_in_b)


def _lam_value(lamv_ref, lam0):
    lv = lamv_ref[...]
    a = jnp.sum(lv[0:1] * lv[1:2], axis=1, keepdims=True)
    b = jnp.sum(lv[2:3] * lv[3:4], axis=1, keepdims=True)
    return jnp.exp(a) - jnp.exp(b) + lam0


def _subln(o, g, lam0):
    ms = jnp.mean(o * o, axis=-1, keepdims=True)
    return o * lax.rsqrt(ms + SUBLN_EPS) * g * (1.0 - lam0)


def _prompt_attn_body(lamv_ref, g_ref, q_ref, k_ref, v_ref, o_ref, *, tq, lam0):
    i = pl.program_id(1)
    q = q_ref[...].astype(F32)
    lane = lax.broadcasted_iota(I32, (tq, HEAD_W), 1)
    q2 = jnp.concatenate([jnp.where(lane < HEAD_DIM, q, 0.0),
                          jnp.where(lane >= HEAD_DIM, q, 0.0)], axis=0).astype(BF16)
    row = lax.broadcasted_iota(I32, (2 * tq, tq), 0)
    row = jnp.where(row >= tq, row - tq, row)
    col = lax.broadcasted_iota(I32, (2 * tq, tq), 1)
    causal = col <= row

    def step(j, carry, masked):
        m, l, acc = carry
        start = pl.multiple_of(j * tq, tq)
        k = k_ref[pl.ds(start, tq), :]
        v = v_ref[pl.ds(start, tq), :]
        s = lax.dot_general(q2, k, (((1,), (1,)), ((), ())), preferred_element_type=F32)
        if masked:
            s = jnp.where(causal, s, -jnp.inf)
        m_new = jnp.maximum(m, jnp.max(s, axis=1, keepdims=True))
        alpha = jnp.exp(m - m_new)
        p = jnp.exp(s - m_new)
        l = alpha * l + jnp.sum(p, axis=1, keepdims=True)
        acc = alpha * acc + jnp.dot(p.astype(BF16), v, preferred_element_type=F32)
        return m_new, l, acc

    init = (jnp.full((2 * tq, 1), -jnp.inf, F32), jnp.zeros((2 * tq, 1), F32),
            jnp.zeros((2 * tq, HEAD_W), F32))
    carry = lax.fori_loop(0, i, lambda j, c: step(j, c, False), init)
    m, l, acc = step(i, carry, True)
    on = acc / l
    lam = _lam_value(lamv_ref, lam0)
    o = on[:tq] - lam * on[tq:]
    o_ref[...] = _subln(o, g_ref[...], lam0).astype(o_ref.dtype)


def _prompt_attn(qb, kb, vb, lamv, subln_g, t, lam0):
    tq = TQ
    assert t % tq == 0
    kern = functools.partial(_prompt_attn_body, tq=tq, lam0=lam0)
    return pl.pallas_call(
        kern,
        grid=(N_HEADS, t // tq),
        in_specs=[_const_spec(lamv.shape), _const_spec(subln_g.shape),
                  pl.BlockSpec((tq, HEAD_W), lambda h, i: (i, h)),
                  pl.BlockSpec((t, HEAD_W), lambda h, i: (0, h)),
                  pl.BlockSpec((t, HEAD_W), lambda h, i: (0, h))],
        out_specs=pl.BlockSpec((tq, HEAD_W), lambda h, i: (i, h)),
        out_shape=jax.ShapeDtypeStruct((t, N_HEADS * HEAD_W), BF16),
        compiler_params=_cparams(("arbitrary", "arbitrary")),
        name="prompt_attn",
    )(lamv, subln_g, qb, kb, vb)


def _decode_attn_body(pt_ref, lamv_ref, g_ref, q_ref, kn_ref, vn_ref, kt_hbm, v_hbm, o_ref,
                      kbuf, vbuf, sem, *, n_pages, page, lam0):
    b = pl.program_id(0)
    nb = pl.num_programs(0)
    nr = 2 * N_HEADS
    wn = N_HEADS * HEAD_W
    vrows = page * N_HEADS

    def copies(bb, jj, slot):
        pg = pt_ref[bb, jj]
        ck = pltpu.make_async_copy(kt_hbm.at[pg], kbuf.at[slot], sem.at[0, slot])
        cv = pltpu.make_async_copy(v_hbm.at[pl.ds(pl.multiple_of(pg * vrows, vrows), vrows)],
                                   vbuf.at[slot], sem.at[1, slot])
        return ck, cv

    def start(bb, jj, slot):
        for c in copies(bb, jj, slot):
            c.start()

    @pl.when(b == 0)
    def _():
        start(0, 0, 0)

    rr = lax.broadcasted_iota(I32, (nr, wn), 0)
    cc = lax.broadcasted_iota(I32, (nr, wn), 1)
    qrow = q_ref[0]
    qbd = jnp.where(cc // HEAD_DIM == rr, jnp.broadcast_to(qrow.astype(F32), (nr, wn)), 0.0).astype(BF16)
    rhead = lax.broadcasted_iota(I32, (nr, HEAD_W), 0) // 2

    def body(j, carry):
        m, l, acc = carry
        slot = j % 2

        @pl.when(j + 1 < n_pages)
        def _():
            start(b, j + 1, 1 - slot)

        @pl.when(jnp.logical_and(j + 1 == n_pages, b + 1 < nb))
        def _():
            start(b + 1, 0, 1 - slot)

        for c in copies(b, j, slot):
            c.wait()
        kt = kbuf[slot].astype(BF16)
        s = jnp.dot(qbd, kt, preferred_element_type=F32)
        m_new = jnp.maximum(m, jnp.max(s, axis=1, keepdims=True))
        alpha = jnp.exp(m - m_new)
        p = jnp.exp(s - m_new)
        l = alpha * l + jnp.sum(p, axis=1, keepdims=True)
        pb = p.astype(BF16)
        pv = jnp.zeros((nr, HEAD_W), F32)
        for h in range(N_HEADS):
            vh = vbuf[slot, pl.ds(h, page, stride=N_HEADS), :].astype(BF16)
            pv = pv + jnp.where(rhead == h, jnp.dot(pb, vh, preferred_element_type=F32), 0.0)
        return m_new, l, alpha * acc + pv

    assert n_pages % 2 == 0
    init = (jnp.full((nr, 1), -jnp.inf, F32), jnp.zeros((nr, 1), F32), jnp.zeros((nr, HEAD_W), F32))
    m, l, acc = lax.fori_loop(0, n_pages, body, init)

    kn = jnp.broadcast_to(kn_ref[0], (nr, wn))
    s_new = jnp.sum(qbd.astype(F32) * kn, axis=1, keepdims=True)
    vn = vn_ref[0]
    vnew = jnp.zeros((nr, HEAD_W), F32)
    for h in range(N_HEADS):
        vnew = vnew + jnp.where(rhead == h, jnp.broadcast_to(vn[:, h * HEAD_W:(h + 1) * HEAD_W], (nr, HEAD_W)), 0.0)
    m_new = jnp.maximum(m, s_new)
    alpha = jnp.exp(m - m_new)
    p_new = jnp.exp(s_new - m_new)
    l = alpha * l + p_new
    acc = alpha * acc + p_new * vnew
    on = acc / l
    lam = _lam_value(lamv_ref, lam0)
    o = on - lam * pltpu.roll(on, nr - 1, 0)
    o_ref[0] = _subln(o, g_ref[...], lam0)


def _decode_attn(page_table, lamv, subln_g, q_s, k_s, v_s, kt_pool, v_pool, lam0):
    nb, n_pages = page_table.shape
    wn = N_HEADS * HEAD_W
    page = kt_pool.shape[2]
    nr = 2 * N_HEADS
    kern = functools.partial(_decode_attn_body, n_pages=n_pages, page=page, lam0=lam0)
    grid_spec = pltpu.PrefetchScalarGridSpec(
        num_scalar_prefetch=1,
        grid=(nb,),
        in_specs=[pl.BlockSpec(lamv.shape, lambda b, pt: (0, 0)),
                  pl.BlockSpec(subln_g.shape, lambda b, pt: (0, 0)),
                  pl.BlockSpec((1, 1, wn), lambda b, pt: (b, 0, 0)),
                  pl.BlockSpec((1, 1, wn), lambda b, pt: (b, 0, 0)),
                  pl.BlockSpec((1, 1, wn), lambda b, pt: (b, 0, 0)),
                  pl.BlockSpec(memory_space=pl.ANY),
                  pl.BlockSpec(memory_space=pl.ANY)],
        out_specs=pl.BlockSpec((1, nr, HEAD_W), lambda b, pt: (b, 0, 0)),
        scratch_shapes=[pltpu.VMEM((2, wn, page), F32),
                        pltpu.VMEM((2, page * N_HEADS, HEAD_W), F32),
                        pltpu.SemaphoreType.DMA((2, 2))],
    )
    out = pl.pallas_call(
        kern,
        grid_spec=grid_spec,
        out_shape=jax.ShapeDtypeStruct((nb, nr, HEAD_W), F32),
        compiler_params=_cparams(("arbitrary",)),
        name="decode_attn",
    )(page_table, lamv, subln_g, q_s.reshape(nb, 1, wn), k_s.reshape(nb, 1, wn),
      v_s.reshape(nb, 1, wn), kt_pool, v_pool)
    return out[:, 0::2, :].reshape(nb, wn)


def _pool_prompt_body(u_ref, halo_ref, d_ref, ext_ref, *, tm, gw):
    i = pl.program_id(0)
    ext_ref[pl.ds(HALO, tm), :] = u_ref[...]

    @pl.when(i == 0)
    def _():
        ext_ref[pl.ds(0, HALO), :] = jnp.zeros((HALO, u_ref.shape[1]), F32)

    @pl.when(i > 0)
    def _():
        ext_ref[pl.ds(0, HALO), :] = halo_ref[...]

    pos = i * tm + lax.broadcasted_iota(I32, (tm, 1), 0)
    for g, w in enumerate(POOL_WINDOWS):
        cols = pl.ds(g * gw, gw)
        tok = ext_ref[pl.ds(HALO, tm), cols]
        s = tok
        for back in range(1, w):
            s = s + ext_ref[pl.ds(HALO - back, tm), cols]
        inv = 1.0 / jnp.minimum(pos + 1, w).astype(F32)
        d_ref[:, cols] = (s * inv - tok).astype(d_ref.dtype)


def _pool_prompt(u_all, t):
    wp = u_all.shape[1]
    gw = wp // len(POOL_WINDOWS)
    tm = TM_ROW
    assert t % tm == 0 and tm % HALO == 0 and gw % LANE == 0
    kern = functools.partial(_pool_prompt_body, tm=tm, gw=gw)
    return pl.pallas_call(
        kern,
        grid=(t // tm,),
        in_specs=[pl.BlockSpec((tm, wp), lambda i: (i, 0)),
                  pl.BlockSpec((HALO, wp), lambda i: (jnp.maximum(i * (tm // HALO) - 1, 0), 0))],
        out_specs=pl.BlockSpec((tm, wp), lambda i: (i, 0)),
        out_shape=jax.ShapeDtypeStruct((t, wp), BF16),
        scratch_shapes=[pltpu.VMEM((tm + HALO, wp), F32)],
        compiler_params=_cparams(("arbitrary",)),
        name="pool_prompt",
    )(u_all, u_all)


def _pool_sample_body(u_ref, st_ref, d_ref, *, gw, start_pos):
    st = st_ref[...]
    u = u_ref[...]
    ridx = lax.broadcasted_iota(I32, st.shape, 1)
    for g, w in enumerate(POOL_WINDOWS):
        hist = jnp.sum(jnp.where(ridx >= POOL_STATE - (w - 1), st, 0.0), axis=1)
        cnt = float(min(start_pos + 1, w))
        d = (hist + u) * (1.0 / cnt) - u
        d_ref[:, g * gw:(g + 1) * gw] = d[:, g * gw:(g + 1) * gw].astype(d_ref.dtype)


def _pool_sample(u_s, state, start_pos):
    nb, wp = u_s.shape
    gw = wp // len(POOL_WINDOWS)
    kern = functools.partial(_pool_sample_body, gw=gw, start_pos=start_pos)
    return pl.pallas_call(
        kern,
        grid=(1,),
        in_specs=[_const_spec(u_s.shape), _const_spec(state.shape)],
        out_specs=_const_spec((nb, wp)),
        out_shape=jax.ShapeDtypeStruct((nb, wp), BF16),
        compiler_params=_cparams(("arbitrary",)),
        name="pool_sample",
    )(u_s, state)


def _layer_norm(x, g, b):
    mu = jnp.mean(x, axis=-1, keepdims=True)
    xc = x - mu
    var = jnp.mean(xc * xc, axis=-1, keepdims=True)
    return xc * lax.rsqrt(var + LN_EPS) * g + b


def _mix_body(d_ref, o_ref, g_ref, xp_ref, xs_ref, pw_ref, ps_ref, wbp_ref, wba_ref, wo_ref,
              lg_ref, lb_ref, wr_ref, br_ref, x1_ref, x1b_ref, lo_ref, *, n_prompt_tiles, alpha):
    i = pl.program_id(0)
    tm, dm = x1_ref.shape
    ng = len(POOL_WINDOWS)
    gw = d_ref.shape[1] // ng
    ys = []
    for g in range(ng):
        ys.append(jnp.dot(d_ref[:, g * gw:(g + 1) * gw], pw_ref[g], preferred_element_type=F32))
    pool_y = (jnp.concatenate(ys, axis=1) * ps_ref[...]).astype(BF16)
    a = jnp.dot(pool_y, wbp_ref[...], preferred_element_type=F32)
    bb = jnp.dot(o_ref[...], wba_ref[...], preferred_element_type=F32)
    gates = g_ref[...].astype(F32)
    merged = (gates[:, :dm] * a + gates[:, dm:] * bb).astype(BF16)
    mix = jnp.dot(merged, wo_ref[...], preferred_element_type=F32)

    def finish(x):
        x1 = _layer_norm(alpha * x + mix, lg_ref[...], lb_ref[...])
        x1_ref[...] = x1
        x1b_ref[...] = x1.astype(BF16)
        lo_ref[...] = jnp.dot(x1, wr_ref[...], preferred_element_type=F32,
                              precision=lax.Precision.HIGHEST) + br_ref[...]

    @pl.when(i < n_prompt_tiles)
    def _():
        finish(xp_ref[...])

    @pl.when(i >= n_prompt_tiles)
    def _():
        ns = xs_ref.shape[0]
        finish(jnp.concatenate([xs_ref[...], jnp.zeros((tm - ns, dm), F32)], axis=0))


def _mix(d_all, o_all, gates, x_p, x_s, pool_w_b, pool_scale, wbp, wba, wo, ln_g, ln_b, wr, br, alpha):
    ta = d_all.shape[0]
    t, dm = x_p.shape
    ns = x_s.shape[0]
    tm = TM_ROW
    assert ta % tm == 0 and t % tm == 0 and ns % SUBLANE == 0 and ns <= tm and ta == t + tm
    npt = t // tm
    wp = d_all.shape[1]
    wa = o_all.shape[1]
    row = lambda w: pl.BlockSpec((tm, w), lambda i: (i, 0))
    kern = functools.partial(_mix_body, n_prompt_tiles=npt, alpha=alpha)
    return pl.pallas_call(
        kern,
        grid=(ta // tm,),
        in_specs=[row(wp), row(wa), row(2 * dm),
                  pl.BlockSpec((tm, dm), lambda i: (jnp.minimum(i, npt - 1), 0)),
                  _const_spec(x_s.shape),
                  _const_spec(pool_w_b.shape), _const_spec(pool_scale.shape),
                  _const_spec(wbp.shape), _const_spec(wba.shape), _const_spec(wo.shape),
                  _const_spec(ln_g.shape), _const_spec(ln_b.shape),
                  _const_spec(wr.shape), _const_spec(br.shape)],
        out_specs=[row(dm), row(dm), row(LANE)],
        out_shape=[jax.ShapeDtypeStruct((ta, dm), F32),
                   jax.ShapeDtypeStruct((ta, dm), BF16),
                   jax.ShapeDtypeStruct((ta, LANE), F32)],
        compiler_params=_cparams(("arbitrary",)),
        name="branch_mix",
    )(d_all, o_all, gates, x_p, x_s, pool_w_b, pool_scale, wbp, wba, wo, ln_g, ln_b, wr, br)


def _ple_body(x1_ref, x1b_ref, pe_ref, gw_ref, gb_ref, pp_ref, r_ref, *, alpha):
    gate = jax.nn.sigmoid(jnp.dot(x1b_ref[...], gw_ref[...], preferred_element_type=F32) + gb_ref[...])
    emb = jnp.dot(pe_ref[...], pp_ref[...], preferred_element_type=F32)
    r_ref[...] = alpha * x1_ref[...] + gate * emb


def _ple(x1, x1b, pe_b, gw_b, gb, pp_b, alpha):
    ta, dm = x1.shape
    tm = TM_ROW
    row = lambda w: pl.BlockSpec((tm, w), lambda i: (i, 0))
    return pl.pallas_call(
        functools.partial(_ple_body, alpha=alpha),
        grid=(ta // tm,),
        in_specs=[row(dm), row(dm), row(pe_b.shape[1]), _const_spec(gw_b.shape),
                  _const_spec(gb.shape), _const_spec(pp_b.shape)],
        out_specs=row(dm),
        out_shape=jax.ShapeDtypeStruct((ta, dm), F32),
        compiler_params=_cparams(("arbitrary",)),
        name="ple",
    )(x1, x1b, pe_b, gw_b, gb, pp_b)


def _route_body(lo_ref, eid_ref, wt_ref):
    lo = lo_ref[...]
    lane_i = lax.broadcasted_iota(I32, lo.shape, 1)
    lane = lane_i.astype(F32)
    ninf = -jnp.inf
    big = float(LANE)
    is_g = lane_i < N_GROUPS
    gl = jnp.where(is_g, lo, ninf)
    gmax = jnp.max(gl, axis=1, keepdims=True)
    gidx = jnp.min(jnp.where(gl == gmax, lane, big), axis=1, keepdims=True)
    gsum = jnp.sum(jnp.where(is_g, jnp.exp(gl - gmax), 0.0), axis=1, keepdims=True)
    gw = 1.0 / gsum
    lo_e = N_GROUPS + EXPERTS_PER_GROUP * gidx
    el = jnp.where(jnp.logical_and(lane >= lo_e, lane < lo_e + EXPERTS_PER_GROUP), lo, ninf)
    v1 = jnp.max(el, axis=1, keepdims=True)
    i1 = jnp.min(jnp.where(el == v1, lane, big), axis=1, keepdims=True)
    el2 = jnp.where(lane == i1, ninf, el)
    v2 = jnp.max(el2, axis=1, keepdims=True)
    i2 = jnp.min(jnp.where(el2 == v2, lane, big), axis=1, keepdims=True)
    e21 = jnp.exp(v2 - v1)
    w1 = gw / (1.0 + e21)
    w2 = gw * e21 / (1.0 + e21)
    eid = jnp.where(lane_i == 0, i1 - N_GROUPS, jnp.where(lane_i == 1, i2 - N_GROUPS, 0.0))
    eid_ref[...] = eid.astype(I32)
    wt_ref[...] = jnp.where(lane_i == 0, w1, jnp.where(lane_i == 1, w2, 0.0))


def _route(logits):
    ta = logits.shape[0]
    tm = TM_ROW
    row = pl.BlockSpec((tm, LANE), lambda i: (i, 0))
    return pl.pallas_call(
        _route_body,
        grid=(ta // tm,),
        in_specs=[row],
        out_specs=[row, row],
        out_shape=[jax.ShapeDtypeStruct((ta, LANE), I32), jax.ShapeDtypeStruct((ta, LANE), F32)],
        compiler_params=_cparams(("arbitrary",)),
        name="route",
    )(logits)


def _slots_body(eid_ref, pos_ref, meta_ref, cnt_ref, run_ref, off_ref, *, tm, tr):
    ph = pl.program_id(0)
    i = pl.program_id(1)
    eid = eid_ref[...]
    lane = lax.broadcasted_iota(I32, (tm, LANE), 1)
    e0 = eid[:, 0:1]
    e1 = eid[:, 1:2]
    oh0 = (lane == e0).astype(F32)
    oh1 = (lane == e1).astype(F32)
    both = oh0 + oh1

    @pl.when(jnp.logical_and(ph == 0, i == 0))
    def _():
        cnt_ref[...] = jnp.zeros_like(cnt_ref)

    @pl.when(ph == 0)
    def _():
        cnt_ref[...] = cnt_ref[...] + jnp.sum(both, axis=0, keepdims=True)

    @pl.when(jnp.logical_and(ph == 1, i == 0))
    def _():
        cnt = cnt_ref[...]
        tiles = jnp.floor((cnt + (tr - 1)) * (1.0 / tr))
        a = lax.broadcasted_iota(I32, (LANE, LANE), 0)
        b = lax.broadcasted_iota(I32, (LANE, LANE), 1)
        upper = (a < b).astype(BF16)
        t8 = jnp.broadcast_to(tiles, (SUBLANE, LANE)).astype(BF16)
        off_tiles = jnp.dot(t8, upper, preferred_element_type=F32)[0:1]
        off_ref[...] = off_tiles * tr
        run_ref[...] = jnp.zeros_like(run_ref)
        end_tiles = off_tiles + tiles
        lane1 = lax.broadcasted_iota(I32, (LANE, LANE), 1)
        jrow = lax.broadcasted_iota(I32, (LANE, LANE), 0).astype(F32)
        done = jnp.logical_and(lane1 < N_EXPERTS, jnp.broadcast_to(end_tiles, (LANE, LANE)) <= jrow)
        te = jnp.sum(done.astype(F32), axis=1, keepdims=True)
        meta_ref[...] = jnp.broadcast_to(te, (LANE, LANE)).astype(I32)

    @pl.when(ph == 1)
    def _():
        r = lax.broadcasted_iota(I32, (tm, tm), 0)
        c = lax.broadcasted_iota(I32, (tm, tm), 1)
        lower = (c < r).astype(BF16)
        before = jnp.dot(lower, both.astype(BF16), preferred_element_type=F32) + run_ref[...] + off_ref[...]
        p0 = jnp.sum(oh0 * before, axis=1, keepdims=True)
        p1 = jnp.sum(oh1 * before, axis=1, keepdims=True)
        pos_ref[...] = jnp.where(lane == 0, p0, jnp.where(lane == 1, p1, 0.0)).astype(I32)
        run_ref[...] = run_ref[...] + jnp.sum(both, axis=0, keepdims=True)


def _slots(eid):
    ta = eid.shape[0]
    tm = TM_ROW
    kern = functools.partial(_slots_body, tm=tm, tr=TR)
    return pl.pallas_call(
        kern,
        grid=(2, ta // tm),
        in_specs=[pl.BlockSpec((tm, LANE), lambda ph, i: (i, 0))],
        out_specs=[pl.BlockSpec((tm, LANE), lambda ph, i: (ph * i, 0)),
                   pl.BlockSpec((LANE, LANE), lambda ph, i: (0, 0))],
        out_shape=[jax.ShapeDtypeStruct((ta, LANE), I32), jax.ShapeDtypeStruct((LANE, LANE), I32)],
        scratch_shapes=[pltpu.VMEM((1, LANE), F32), pltpu.VMEM((1, LANE), F32), pltpu.VMEM((1, LANE), F32)],
        compiler_params=_cparams(("arbitrary", "arbitrary")),
        name="slots",
    )(eid)


def _rowcopy_body(si_ref, di_ref, src_hbm, *rest, chunk):
    dst_hbm, sem = rest[-2:]

    def mk(n):
        s = pl.multiple_of(si_ref[n] * CHUNK, CHUNK)
        d = pl.multiple_of(di_ref[n] * CHUNK, CHUNK)
        return pltpu.make_async_copy(src_hbm.at[pl.ds(s, CHUNK)], dst_hbm.at[pl.ds(d, CHUNK)], sem)

    def go(n, _):
        mk(n).start()
        return 0

    def done(n, _):
        mk(n).wait()
        return 0

    lax.fori_loop(0, chunk, go, 0)
    lax.fori_loop(0, chunk, done, 0)


def _rowcopy(src_idx, dst_idx, src_flat, dst_rows, dst_init=None):
    n = src_idx.shape[0]
    chunk = 512
    assert n % chunk == 0
    idx_spec = pl.BlockSpec((chunk,), lambda i: (i,), memory_space=pltpu.SMEM)
    any_spec = pl.BlockSpec(memory_space=pl.ANY)
    extra = () if dst_init is None else (dst_init,)
    return pl.pallas_call(
        functools.partial(_rowcopy_body, chunk=chunk),
        grid=(n // chunk,),
        in_specs=[idx_spec, idx_spec, any_spec] + [any_spec] * len(extra),
        out_specs=any_spec,
        out_shape=jax.ShapeDtypeStruct((dst_rows * CHUNK, LANE), src_flat.dtype),
        scratch_shapes=[pltpu.SemaphoreType.DMA(())],
        input_output_aliases={} if dst_init is None else {3: 0},
        compiler_params=_cparams(("arbitrary",)),
        name="rowcopy",
    )(src_idx, dst_idx, src_flat, *extra)


def _expert_body(te_ref, nt_ref, x_ref, wg_ref, wu_ref, wd_ref, y_ref):
    j = pl.program_id(0)

    @pl.when(j < nt_ref[0])
    def _():
        x = x_ref[...].astype(BF16)
        hg = jnp.dot(x, wg_ref[0].astype(BF16), preferred_element_type=F32)
        hu = jnp.dot(x, wu_ref[0].astype(BF16), preferred_element_type=F32)
        h = (hg * jax.nn.sigmoid(hg) * hu).astype(BF16)
        y_ref[...] = jnp.dot(h, wd_ref[0].astype(BF16), preferred_element_type=F32)

    @pl.when(j >= nt_ref[0])
    def _():
        y_ref[...] = jnp.zeros_like(y_ref)


def _experts(tile_expert, n_tiles, xs, w_gate, w_up, w_down):
    r, dm = xs.shape
    ff = w_gate.shape[2]
    tr = TR
    grid_spec = pltpu.PrefetchScalarGridSpec(
        num_scalar_prefetch=2,
        grid=(r // tr,),
        in_specs=[pl.BlockSpec((tr, dm), lambda j, te, nt: (j, 0)),
                  pl.BlockSpec((1, dm, ff), lambda j, te, nt: (te[j], 0, 0)),
                  pl.BlockSpec((1, dm, ff), lambda j, te, nt: (te[j], 0, 0)),
                  pl.BlockSpec((1, ff, dm), lambda j, te, nt: (te[j], 0, 0))],
        out_specs=pl.BlockSpec((tr, dm), lambda j, te, nt: (j, 0)),
    )
    return pl.pallas_call(
        _expert_body,
        grid_spec=grid_spec,
        out_shape=jax.ShapeDtypeStruct((r, dm), F32),
        compiler_params=_cparams(("arbitrary",)),
        name="experts",
    )(tile_expert, n_tiles, xs, w_gate, w_up, w_down)


def _final_body(r_ref, y0_ref, y1_ref, wt_ref, g_ref, b_ref, yp_ref, ys_ref, *, n_prompt_tiles):
    i = pl.program_id(0)
    wt = wt_ref[...]
    x = r_ref[...] + wt[:, 0:1] * y0_ref[...] + wt[:, 1:2] * y1_ref[...]
    y = _layer_norm(x, g_ref[...], b_ref[...])

    @pl.when(i < n_prompt_tiles)
    def _():
        yp_ref[...] = y

    @pl.when(i >= n_prompt_tiles)
    def _():
        ys_ref[...] = y


def _final(r, y0, y1, wts, ln_g, ln_b, t):
    ta, dm = r.shape
    tm = TM_ROW
    npt = t // tm
    row = lambda w: pl.BlockSpec((tm, w), lambda i: (i, 0))
    return pl.pallas_call(
        functools.partial(_final_body, n_prompt_tiles=npt),
        grid=(ta // tm,),
        in_specs=[row(dm), row(dm), row(dm), row(LANE), _const_spec(ln_g.shape), _const_spec(ln_b.shape)],
        out_specs=[pl.BlockSpec((tm, dm), lambda i: (jnp.minimum(i, npt - 1), 0)),
                   pl.BlockSpec((tm, dm), lambda i: (0, 0))],
        out_shape=[jax.ShapeDtypeStruct((t, dm), F32), jax.ShapeDtypeStruct((tm, dm), F32)],
        compiler_params=_cparams(("arbitrary",)),
        name="final",
    )(r, y0, y1, wts, ln_g, ln_b)


def _layer(layer, x_p, x_s, pe_p, pe_s, cache_k, cache_v, state, page_table, w):
    t, dm = x_p.shape
    nb = x_s.shape[0]
    wn = N_HEADS * HEAD_W
    ta = t + ROW_PAD
    assert nb <= ROW_PAD and t % ROW_PAD == 0
    lam0 = _lambda_init(layer)
    depth_total = w["depth"]
    alpha = (2.0 * depth_total) ** 0.25
    page = cache_k.shape[1]
    start_pos = page_table.shape[1] * page

    pad = ta - t - nb
    xb = jnp.concatenate([x_p.astype(BF16), x_s.astype(BF16), jnp.zeros((pad, dm), BF16)], axis=0)
    u, qb, k, kb, v, vb, gates = _inproj(xb, w["w_in"].astype(BF16))

    lamv = jnp.concatenate([w["lam_q1"], w["lam_k1"], w["lam_q2"], w["lam_k2"]], axis=0)
    subln_g = w["subln_g"]

    o_p = _prompt_attn(qb, kb, vb, lamv, subln_g, t, lam0)
    kt_pool = jnp.transpose(cache_k, (0, 2, 3, 4, 1)).reshape(cache_k.shape[0], wn, page)
    v_pool = cache_v.reshape(cache_v.shape[0] * page * N_HEADS, HEAD_W)
    o_s = _decode_attn(page_table, lamv, subln_g, qb[t:t + nb], k[t:t + nb], v[t:t + nb],
                       kt_pool, v_pool, lam0)
    o_all = jnp.concatenate([o_p, o_s.astype(BF16), jnp.zeros((pad, wn), BF16)], axis=0)

    d_p = _pool_prompt(u, t)
    d_s = _pool_sample(u[t:t + nb], state, start_pos)
    d_all = jnp.concatenate([d_p, d_s, jnp.zeros((pad, d_p.shape[1]), BF16)], axis=0)

    x1, x1b, logits = _mix(d_all, o_all, gates, x_p, x_s, w["pool_w"].astype(BF16), w["pool_scale"],
                           w["w_branch_pool"].astype(BF16), w["w_branch_attn"].astype(BF16),
                           w["w_out"].astype(BF16), w["ln1_g"], w["ln1_b"], w["router_w"], w["router_b"], alpha)

    pe_b = jnp.concatenate([pe_p.astype(BF16), pe_s.astype(BF16), jnp.zeros((pad, pe_p.shape[1]), BF16)], axis=0)
    r = _ple(x1, x1b, pe_b, w["ple_gate_w"].astype(BF16), w["ple_gate_b"], w["ple_proj"].astype(BF16), alpha)

    eid, wts = _route(logits)
    pos, meta = _slots(eid)
    n_pairs = TOP_K * ta
    n_rows = n_pairs + N_EXPERTS * TR
    n_row_tiles = n_rows // TR
    assert n_row_tiles <= LANE
    te_raw = meta[:n_row_tiles, 0]
    tile_expert = jnp.minimum(te_raw, N_EXPERTS - 1)
    n_tiles = jnp.sum((te_raw < N_EXPERTS).astype(I32)).reshape(1)

    tok = jnp.arange(ta, dtype=I32)
    src_tok = jnp.concatenate([tok, tok])
    slot = jnp.concatenate([pos[:, 0], pos[:, 1]])
    xs_flat = _rowcopy(src_tok, slot, x1.reshape(ta * CHUNK, LANE), n_rows,
                       dst_init=jnp.zeros((n_rows * CHUNK, LANE), F32))
    ys = _experts(tile_expert, n_tiles, xs_flat.reshape(n_rows, dm), w["expert_gate"], w["expert_up"],
                  w["expert_down"])
    dst = jnp.concatenate([tok, tok + ta])
    y01 = _rowcopy(slot, dst, ys.reshape(n_rows * CHUNK, LANE), 2 * ta)
    y01 = y01.reshape(2, ta, dm)
    y_p, y_s = _final(r, y01[0], y01[1], wts, w["ln2_g"], w["ln2_b"], t)
    return y_p, y_s[:nb], k, v, u


def kernel(x_prompt, x_sample, p_prompt, p_sample, cache_k, cache_v, state_pool, page_table, w_in, pool_w, pool_scale, lam_q1, lam_k1, lam_q2, lam_k2, subln_g, w_branch_pool, w_branch_attn, w_out, ln1_g, ln1_b, router_group_w, router_group_b, router_expert_w, router_expert_b, expert_gate, expert_up, expert_down, ple_proj, ple_gate_w, ple_gate_b, ln2_g, ln2_b):
    depth = w_in.shape[0]
    bp, t, dm = x_prompt.shape
    nb, ts, _ = x_sample.shape
    assert bp == 1 and ts == 1 and dm // 256 == N_HEADS
    assert dm % CHUNK == 0 and dm // CHUNK == LANE
    hp = x_prompt.reshape(t, dm)
    hs = x_sample.reshape(nb, dm)
    outs = {n: [] for n in ("kp", "vp", "sp", "ks", "vs", "ss")}
    for l in range(depth):
        wr = jnp.concatenate([router_group_w[l],
                              jnp.transpose(router_expert_w[l], (1, 0, 2)).reshape(dm, N_EXPERTS)], axis=1)
        wr = jnp.pad(wr, ((0, 0), (0, LANE - wr.shape[1])))
        br = jnp.pad(jnp.concatenate([router_group_b[l], router_expert_b[l].reshape(-1)]),
                     (0, LANE - N_GROUPS - N_EXPERTS)).reshape(1, LANE)
        row = lambda a: a[l].reshape(1, -1)
        w = dict(depth=depth, w_in=w_in[l], pool_w=pool_w[l], pool_scale=row(pool_scale),
                 lam_q1=row(lam_q1), lam_k1=row(lam_k1), lam_q2=row(lam_q2), lam_k2=row(lam_k2),
                 subln_g=row(subln_g), w_branch_pool=w_branch_pool[l], w_branch_attn=w_branch_attn[l],
                 w_out=w_out[l], ln1_g=row(ln1_g), ln1_b=row(ln1_b), router_w=wr, router_b=br,
                 expert_gate=expert_gate[l], expert_up=expert_up[l], expert_down=expert_down[l],
                 ple_proj=ple_proj[l], ple_gate_w=ple_gate_w[l], ple_gate_b=row(ple_gate_b),
                 ln2_g=row(ln2_g), ln2_b=row(ln2_b))
        state = state_pool[l]
        hp, hs, k, v, u = _layer(l, hp, hs, p_prompt[l, 0], p_sample[l, :, 0], cache_k[l], cache_v[l],
                                 state, page_table, w)
        outs["kp"].append(k[:t].reshape(1, t, N_HEADS, 2, HEAD_DIM))
        outs["vp"].append(v[:t].reshape(1, t, N_HEADS, HEAD_W))
        outs["sp"].append(u[t - POOL_STATE:t].reshape(1, POOL_STATE, -1))
        outs["ks"].append(k[t:t + nb].reshape(nb, 1, N_HEADS, 2, HEAD_DIM))
        outs["vs"].append(v[t:t + nb].reshape(nb, 1, N_HEADS, HEAD_W))
        outs["ss"].append(jnp.concatenate([state[:, 1:], u[t:t + nb][:, None, :]], axis=1))
    st = lambda n: jnp.stack(outs[n])
    return (hp.reshape(1, t, dm), hs.reshape(nb, 1, dm), st("kp"), st("vp"), st("sp"),
            st("ks"), st("vs"), st("ss"))
```

```python
import functools
import math

import jax
import jax.numpy as jnp
from jax import lax
from jax.experimental import pallas as pl
from jax.experimental.pallas import tpu as pltpu

F32 = jnp.float32
BF16 = jnp.bfloat16
I32 = jnp.int32

N_HEADS = 8
HEAD_DIM = 64
HEAD_W = 2 * HEAD_DIM
ATTN_SCALE = HEAD_DIM ** -0.5
LOG2E = math.log2(math.e)
SUBLN_EPS = 1e-5
LN_EPS = 1e-5
POOL_WINDOWS = (2, 4, 8, 16)
POOL_STATE = max(POOL_WINDOWS) - 1
N_GROUPS = 4
EXPERTS_PER_GROUP = 8
N_EXPERTS = N_GROUPS * EXPERTS_PER_GROUP
TOP_K = 2

LANE = 128
SUBLANE = 8
VMEM_LIMIT = 56 * 1024 * 1024

ROW_PAD = 256
TM_INPROJ = 384
TM_ROW = 256
TQ = 256
TK = 512
TR = 256
HALO = 16
CHUNK = 16
DECODE_GROUP = 2
DECODE_SLOTS = 6


def _lambda_init(layer):
    return 0.8 - 0.6 * math.exp(-0.3 * layer)


def _cparams(sem, vmem=VMEM_LIMIT):
    return pltpu.CompilerParams(dimension_semantics=sem, vmem_limit_bytes=vmem)


def _const_spec(shape):
    nd = len(shape)
    return pl.BlockSpec(shape, lambda *_: (0,) * nd, pipeline_mode=pl.Buffered(1))


def _inproj_body(x_ref, w_ref, u_ref, q_ref, k_ref, kb_ref, v_ref, vb_ref, g_ref):
    j = pl.program_id(0)
    z = jnp.dot(x_ref[...], w_ref[...], preferred_element_type=F32)

    @pl.when(j == 0)
    def _():
        u_ref[...] = z

    @pl.when(j == 1)
    def _():
        q_ref[...] = (z * (ATTN_SCALE * LOG2E)).astype(BF16)

    @pl.when(j == 2)
    def _():
        k_ref[...] = z
        kb_ref[...] = z.astype(BF16)

    @pl.when(j == 3)
    def _():
        v_ref[...] = z
        vb_ref[...] = z.astype(BF16)

    @pl.when(j >= 4)
    def _():
        g_ref[...] = jax.nn.sigmoid(z).astype(BF16)


def _inproj(xb, w_in_b):
    ta, d = xb.shape
    n_in = w_in_b.shape[1]
    wn = N_HEADS * HEAD_W
    assert n_in == 4 * wn + 2 * d and (2 * d) % wn == 0
    tm = TM_INPROJ
    assert ta % tm == 0
    ni = ta // tm
    nj = n_in // wn

    def once(jo):
        return lambda j, i: (jnp.where(j < jo, 0, jnp.where(j == jo, i, ni - 1)), 0)

    blk = lambda jo: pl.BlockSpec((tm, wn), once(jo))
    g_spec = pl.BlockSpec((tm, wn), lambda j, i: (jnp.where(j < 4, 0, i), jnp.maximum(j - 4, 0)))
    return pl.pallas_call(
        _inproj_body,
        grid=(nj, ni),
        in_specs=[pl.BlockSpec((tm, d), lambda j, i: (i, 0)),
                  pl.BlockSpec((d, wn), lambda j, i: (0, j))],
        out_specs=[blk(0), blk(1), blk(2), blk(2), blk(3), blk(3), g_spec],
        out_shape=[jax.ShapeDtypeStruct((ta, wn), F32),
                   jax.ShapeDtypeStruct((ta, wn), BF16),
                   jax.ShapeDtypeStruct((ta, wn), F32),
                   jax.ShapeDtypeStruct((ta, wn), BF16),
                   jax.ShapeDtypeStruct((ta, wn), F32),
                   jax.ShapeDtypeStruct((ta, wn), BF16),
                   jax.ShapeDtypeStruct((ta, 2 * d), BF16)],
        compiler_params=_cparams(("arbitrary", "arbitrary")),
        name="inproj",
    )(xb, w_in_b)


def _lam_value(lamv_ref, lam0):
    lv = lamv_ref[...]
    a = jnp.sum(lv[0:1] * lv[1:2], axis=1, keepdims=True)
    b = jnp.sum(lv[2:3] * lv[3:4], axis=1, keepdims=True)
    return jnp.exp(a) - jnp.exp(b) + lam0


def _subln(o, g, lam0):
    ms = jnp.mean(o * o, axis=-1, keepdims=True)
    return o * lax.rsqrt(ms + SUBLN_EPS) * g * (1.0 - lam0)


def _prompt_attn_body(lamv_ref, gcol_ref, q_ref, k_ref, v_ref, o_ref, vt_ref, *, tq, tk, lam0):
    i = pl.program_id(1)
    nkv = v_ref.shape[0] // tk

    @pl.when(i == 0)
    def _():
        def tr(jb, _):
            blk = v_ref[pl.ds(pl.multiple_of(jb * tk, tk), tk), :].astype(F32)
            vt_ref[jb] = blk.T.astype(BF16)
            return 0
        lax.fori_loop(0, nkv, tr, 0)

    qt = q_ref[...].astype(F32).T
    drow = lax.broadcasted_iota(I32, (HEAD_W, tq), 0)
    q2t = jnp.concatenate([jnp.where(drow < HEAD_DIM, qt, 0.0),
                           jnp.where(drow >= HEAD_DIM, qt, 0.0)], axis=1).astype(BF16)
    n_full = (i * tq) // tk
    kpos = n_full * tk + lax.broadcasted_iota(I32, (tk, 2 * tq), 0)
    qcol = lax.broadcasted_iota(I32, (tk, 2 * tq), 1)
    qpos = i * tq + jnp.where(qcol >= tq, qcol - tq, qcol)
    causal = kpos <= qpos

    def step(j, carry, masked):
        m, l, acc = carry
        k = k_ref[pl.ds(pl.multiple_of(j * tk, tk), tk), :]
        s = jnp.dot(k, q2t, preferred_element_type=F32)
        if masked:
            s = jnp.where(causal, s, -jnp.inf)
        m_new = jnp.maximum(m, jnp.max(s, axis=0, keepdims=True))
        alpha = jnp.exp2(m - m_new)
        p = jnp.exp2(s - m_new)
        l = alpha * l + jnp.sum(p, axis=0, keepdims=True)
        acc = alpha * acc + jnp.dot(vt_ref[j], p.astype(BF16), preferred_element_type=F32)
        return m_new, l, acc

    init = (jnp.full((1, 2 * tq), -jnp.inf, F32), jnp.zeros((1, 2 * tq), F32),
            jnp.zeros((HEAD_W, 2 * tq), F32))
    carry = lax.fori_loop(0, n_full, lambda j, c: step(j, c, False), init)
    m, l, acc = step(n_full, carry, True)
    on = acc / l
    lam = _lam_value(lamv_ref, lam0)
    ot = on[:, :tq] - lam * on[:, tq:]
    ms = jnp.mean(ot * ot, axis=0, keepdims=True)
    ot = ot * lax.rsqrt(ms + SUBLN_EPS) * gcol_ref[...] * (1.0 - lam0)
    o_ref[...] = ot.T.astype(o_ref.dtype)


def _prompt_attn(qb, kb, vb, lamv, subln_g, t, lam0):
    tq, tk = TQ, TK
    assert t % tk == 0 and tk % tq == 0
    kern = functools.partial(_prompt_attn_body, tq=tq, tk=tk, lam0=lam0)
    gcol = subln_g.reshape(HEAD_W, 1)
    return pl.pallas_call(
        kern,
        grid=(N_HEADS, t // tq),
        in_specs=[_const_spec(lamv.shape), _const_spec(gcol.shape),
                  pl.BlockSpec((tq, HEAD_W), lambda h, i: (i, h)),
                  pl.BlockSpec((t, HEAD_W), lambda h, i: (0, h)),
                  pl.BlockSpec((t, HEAD_W), lambda h, i: (0, h))],
        out_specs=pl.BlockSpec((tq, HEAD_W), lambda h, i: (i, h)),
        out_shape=jax.ShapeDtypeStruct((t, N_HEADS * HEAD_W), BF16),
        scratch_shapes=[pltpu.VMEM((t // tk, HEAD_W, tk), BF16)],
        compiler_params=_cparams(("arbitrary", "arbitrary")),
        name="prompt_attn",
    )(lamv, gcol, qb, kb, vb)


def _decode_attn_body(pt_ref, lamv_ref, g_ref, q_ref, kn_ref, vn_ref, kt_hbm, v_hbm, o_ref,
                      kbuf, vbuf, sem, *, n_pages, page, n_slots, group, lam0):
    b = pl.program_id(0)
    nb = pl.num_programs(0)
    nr = 2 * N_HEADS
    wn = N_HEADS * HEAD_W
    vrows = page * N_HEADS
    ahead = n_slots - group
    total = nb * n_pages
    assert n_pages % group == 0 and n_slots % group == 0 and ahead >= group

    def copies(g):
        slot = g % n_slots
        pg = pt_ref[g // n_pages, g % n_pages]
        ck = pltpu.make_async_copy(kt_hbm.at[pg], kbuf.at[slot], sem.at[0, slot])
        cv = pltpu.make_async_copy(v_hbm.at[pl.ds(pl.multiple_of(pg * vrows, vrows), vrows)],
                                   vbuf.at[slot], sem.at[1, slot])
        return ck, cv

    def start(g):
        for c in copies(g):
            c.start()

    @pl.when(b == 0)
    def _():
        for g in range(ahead):
            start(g)

    rr = lax.broadcasted_iota(I32, (nr, wn), 0)
    cc = lax.broadcasted_iota(I32, (nr, wn), 1)
    qrow = q_ref[0]
    qbd = jnp.where(cc // HEAD_DIM == rr, jnp.broadcast_to(qrow.astype(F32), (nr, wn)), 0.0).astype(BF16)
    rhead = lax.broadcasted_iota(I32, (nr, HEAD_W), 0) // 2
    tok = lax.broadcasted_iota(I32, (page, vrows), 0)
    vrow = lax.broadcasted_iota(I32, (page, vrows), 1)
    spread = jnp.where(vrow // N_HEADS == tok, 1.0, 0.0).astype(BF16)
    own_head = lax.broadcasted_iota(I32, (nr, vrows), 1) % N_HEADS == lax.broadcasted_iota(I32, (nr, vrows), 0) // 2

    def body(jg, carry):
        m, l, acc = carry
        g0 = b * n_pages + jg * group
        slots = []
        ss = []
        for a in range(group):
            g = g0 + a

            @pl.when(g + ahead < total)
            def _():
                start(g + ahead)

            for c in copies(g):
                c.wait()
            slot = g % n_slots
            slots.append(slot)
            kt = kbuf[slot].astype(BF16)
            ss.append(jnp.dot(qbd, kt, preferred_element_type=F32))
        m_new = m
        for s in ss:
            m_new = jnp.maximum(m_new, jnp.max(s, axis=1, keepdims=True))
        alpha = jnp.exp2(m - m_new)
        l = alpha * l
        acc = alpha * acc
        for s, slot in zip(ss, slots):
            p = jnp.exp2(s - m_new)
            l = l + jnp.sum(p, axis=1, keepdims=True)
            pe = jnp.dot(p.astype(BF16), spread, preferred_element_type=F32)
            pe = jnp.where(own_head, pe, 0.0).astype(BF16)
            acc = acc + jnp.dot(pe, vbuf[slot].astype(BF16), preferred_element_type=F32)
        return m_new, l, acc

    init = (jnp.full((nr, 1), -jnp.inf, F32), jnp.zeros((nr, 1), F32), jnp.zeros((nr, HEAD_W), F32))
    m, l, acc = lax.fori_loop(0, n_pages // group, body, init)

    kn = jnp.broadcast_to(kn_ref[0], (nr, wn))
    s_new = jnp.sum(qbd.astype(F32) * kn, axis=1, keepdims=True)
    vn = vn_ref[0]
    vnew = jnp.zeros((nr, HEAD_W), F32)
    for h in range(N_HEADS):
        vnew = vnew + jnp.where(rhead == h, jnp.broadcast_to(vn[:, h * HEAD_W:(h + 1) * HEAD_W], (nr, HEAD_W)), 0.0)
    m_new = jnp.maximum(m, s_new)
    alpha = jnp.exp2(m - m_new)
    p_new = jnp.exp2(s_new - m_new)
    l = alpha * l + p_new
    acc = alpha * acc + p_new * vnew
    on = acc / l
    lam = _lam_value(lamv_ref, lam0)
    o = on - lam * pltpu.roll(on, nr - 1, 0)
    o_ref[0] = _subln(o, g_ref[...], lam0)


def _decode_attn(page_table, lamv, subln_g, q_s, k_s, v_s, kt_pool, v_pool, lam0):
    nb, n_pages = page_table.shape
    wn = N_HEADS * HEAD_W
    page = kt_pool.shape[2]
    nr = 2 * N_HEADS
    n_slots = DECODE_SLOTS
    kern = functools.partial(_decode_attn_body, n_pages=n_pages, page=page, n_slots=n_slots,
                             group=DECODE_GROUP, lam0=lam0)
    grid_spec = pltpu.PrefetchScalarGridSpec(
        num_scalar_prefetch=1,
        grid=(nb,),
        in_specs=[pl.BlockSpec(lamv.shape, lambda b, pt: (0, 0)),
                  pl.BlockSpec(subln_g.shape, lambda b, pt: (0, 0)),
                  pl.BlockSpec((1, 1, wn), lambda b, pt: (b, 0, 0)),
                  pl.BlockSpec((1, 1, wn), lambda b, pt: (b, 0, 0)),
                  pl.BlockSpec((1, 1, wn), lambda b, pt: (b, 0, 0)),
                  pl.BlockSpec(memory_space=pl.ANY),
                  pl.BlockSpec(memory_space=pl.ANY)],
        out_specs=pl.BlockSpec((1, nr, HEAD_W), lambda b, pt: (b, 0, 0)),
        scratch_shapes=[pltpu.VMEM((n_slots, wn, page), F32),
                        pltpu.VMEM((n_slots, page * N_HEADS, HEAD_W), F32),
                        pltpu.SemaphoreType.DMA((2, n_slots))],
    )
    out = pl.pallas_call(
        kern,
        grid_spec=grid_spec,
        out_shape=jax.ShapeDtypeStruct((nb, nr, HEAD_W), F32),
        compiler_params=_cparams(("arbitrary",)),
        name="decode_attn",
    )(page_table, lamv, subln_g, q_s.reshape(nb, 1, wn), k_s.reshape(nb, 1, wn),
      v_s.reshape(nb, 1, wn), kt_pool, v_pool)
    return out[:, 0::2, :].reshape(nb, wn)


def _pool_prompt_body(u_ref, halo_ref, d_ref, ext_ref, *, tm, gw):
    i = pl.program_id(0)
    ext_ref[pl.ds(HALO, tm), :] = u_ref[...]

    @pl.when(i == 0)
    def _():
        ext_ref[pl.ds(0, HALO), :] = jnp.zeros((HALO, u_ref.shape[1]), F32)

    @pl.when(i > 0)
    def _():
        ext_ref[pl.ds(0, HALO), :] = halo_ref[...]

    pos = i * tm + lax.broadcasted_iota(I32, (tm, 1), 0)
    for g, w in enumerate(POOL_WINDOWS):
        cols = pl.ds(g * gw, gw)
        tok = ext_ref[pl.ds(HALO, tm), cols]
        s = tok
        for back in range(1, w):
            s = s + ext_ref[pl.ds(HALO - back, tm), cols]
        inv = 1.0 / jnp.minimum(pos + 1, w).astype(F32)
        d_ref[:, cols] = (s * inv - tok).astype(d_ref.dtype)


def _pool_prompt(u_all, t):
    wp = u_all.shape[1]
    gw = wp // len(POOL_WINDOWS)
    tm = TM_ROW
    assert t % tm == 0 and tm % HALO == 0 and gw % LANE == 0
    kern = functools.partial(_pool_prompt_body, tm=tm, gw=gw)
    return pl.pallas_call(
        kern,
        grid=(t // tm,),
        in_specs=[pl.BlockSpec((tm, wp), lambda i: (i, 0)),
                  pl.BlockSpec((HALO, wp), lambda i: (jnp.maximum(i * (tm // HALO) - 1, 0), 0))],
        out_specs=pl.BlockSpec((tm, wp), lambda i: (i, 0)),
        out_shape=jax.ShapeDtypeStruct((t, wp), BF16),
        scratch_shapes=[pltpu.VMEM((tm + HALO, wp), F32)],
        compiler_params=_cparams(("arbitrary",)),
        name="pool_prompt",
    )(u_all, u_all)


def _pool_sample_body(u_ref, st_ref, d_ref, *, gw, start_pos):
    st = st_ref[...]
    u = u_ref[...]
    ridx = lax.broadcasted_iota(I32, st.shape, 1)
    for g, w in enumerate(POOL_WINDOWS):
        hist = jnp.sum(jnp.where(ridx >= POOL_STATE - (w - 1), st, 0.0), axis=1)
        cnt = float(min(start_pos + 1, w))
        d = (hist + u) * (1.0 / cnt) - u
        d_ref[:, g * gw:(g + 1) * gw] = d[:, g * gw:(g + 1) * gw].astype(d_ref.dtype)


def _pool_sample(u_s, state, start_pos):
    nb, wp = u_s.shape
    gw = wp // len(POOL_WINDOWS)
    kern = functools.partial(_pool_sample_body, gw=gw, start_pos=start_pos)
    return pl.pallas_call(
        kern,
        grid=(1,),
        in_specs=[_const_spec(u_s.shape), _const_spec(state.shape)],
        out_specs=_const_spec((nb, wp)),
        out_shape=jax.ShapeDtypeStruct((nb, wp), BF16),
        compiler_params=_cparams(("arbitrary",)),
        name="pool_sample",
    )(u_s, state)


def _store_flat(flat_ref, x, lead=()):
    rows = x.shape[0]
    for c in range(CHUNK):
        flat_ref[(*lead, pl.ds(c, rows, stride=CHUNK), slice(None))] = x[:, c * LANE:(c + 1) * LANE]


def _load_flat(flat_ref, rows, lead=()):
    return jnp.concatenate(
        [flat_ref[(*lead, pl.ds(c, rows, stride=CHUNK), slice(None))] for c in range(CHUNK)], axis=1)


def _layer_norm(x, g, b):
    mu = jnp.mean(x, axis=-1, keepdims=True)
    xc = x - mu
    var = jnp.mean(xc * xc, axis=-1, keepdims=True)
    return xc * lax.rsqrt(var + LN_EPS) * g + b


def _mix_body(d_ref, o_ref, g_ref, xp_ref, xs_ref, pw_ref, ps_ref, wbp_ref, wba_ref, wo_ref,
              lg_ref, lb_ref, wr_ref, br_ref, x1_ref, x1f_ref, lo_ref, *, n_prompt_tiles, alpha):
    i = pl.program_id(0)
    tm, dm = x1_ref.shape
    ng = len(POOL_WINDOWS)
    gw = d_ref.shape[1] // ng
    ys = []
    for g in range(ng):
        ys.append(jnp.dot(d_ref[:, g * gw:(g + 1) * gw], pw_ref[g], preferred_element_type=F32))
    pool_y = (jnp.concatenate(ys, axis=1) * ps_ref[...]).astype(BF16)
    a = jnp.dot(pool_y, wbp_ref[...], preferred_element_type=F32)
    bb = jnp.dot(o_ref[...], wba_ref[...], preferred_element_type=F32)
    gates = g_ref[...].astype(F32)
    merged = (gates[:, :dm] * a + gates[:, dm:] * bb).astype(BF16)
    mix = jnp.dot(merged, wo_ref[...], preferred_element_type=F32)

    def finish(x):
        x1 = _layer_norm(alpha * x + mix, lg_ref[...], lb_ref[...])
        x1_ref[...] = x1
        _store_flat(x1f_ref, x1)
        lo_ref[...] = jnp.dot(x1, wr_ref[...], preferred_element_type=F32,
                              precision=lax.Precision.HIGHEST) + br_ref[...]

    @pl.when(i < n_prompt_tiles)
    def _():
        finish(xp_ref[...])

    @pl.when(i >= n_prompt_tiles)
    def _():
        ns = xs_ref.shape[0]
        finish(jnp.concatenate([xs_ref[...], jnp.zeros((tm - ns, dm), F32)], axis=0))


def _mix(d_all, o_all, gates, x_p, x_s, pool_w_b, pool_scale, wbp, wba, wo, ln_g, ln_b, wr, br, alpha):
    ta = d_all.shape[0]
    t, dm = x_p.shape
    ns = x_s.shape[0]
    tm = TM_ROW
    assert ta % tm == 0 and t % tm == 0 and ns % SUBLANE == 0 and ns <= tm and ta == t + tm
    npt = t // tm
    wp = d_all.shape[1]
    wa = o_all.shape[1]
    row = lambda w: pl.BlockSpec((tm, w), lambda i: (i, 0))
    kern = functools.partial(_mix_body, n_prompt_tiles=npt, alpha=alpha)
    return pl.pallas_call(
        kern,
        grid=(ta // tm,),
        in_specs=[row(wp), row(wa), row(2 * dm),
                  pl.BlockSpec((tm, dm), lambda i: (jnp.minimum(i, npt - 1), 0)),
                  _const_spec(x_s.shape),
                  _const_spec(pool_w_b.shape), _const_spec(pool_scale.shape),
                  _const_spec(wbp.shape), _const_spec(wba.shape), _const_spec(wo.shape),
                  _const_spec(ln_g.shape), _const_spec(ln_b.shape),
                  _const_spec(wr.shape), _const_spec(br.shape)],
        out_specs=[row(dm), pl.BlockSpec((tm * CHUNK, LANE), lambda i: (i, 0)), row(LANE)],
        out_shape=[jax.ShapeDtypeStruct((ta, dm), F32),
                   jax.ShapeDtypeStruct((ta * CHUNK, LANE), F32),
                   jax.ShapeDtypeStruct((ta, LANE), F32)],
        compiler_params=_cparams(("arbitrary",)),
        name="branch_mix",
    )(d_all, o_all, gates, x_p, x_s, pool_w_b, pool_scale, wbp, wba, wo, ln_g, ln_b, wr, br)


def _ple_body(x1_ref, pe_ref, gw_ref, gb_ref, pp_ref, r_ref, *, alpha):
    x1 = x1_ref[...]
    gate = jax.nn.sigmoid(jnp.dot(x1.astype(BF16), gw_ref[...], preferred_element_type=F32) + gb_ref[...])
    emb = jnp.dot(pe_ref[...], pp_ref[...], preferred_element_type=F32)
    r_ref[...] = alpha * x1 + gate * emb


def _ple(x1, pe_b, gw_b, gb, pp_b, alpha):
    ta, dm = x1.shape
    tm = TM_ROW
    row = lambda w: pl.BlockSpec((tm, w), lambda i: (i, 0))
    return pl.pallas_call(
        functools.partial(_ple_body, alpha=alpha),
        grid=(ta // tm,),
        in_specs=[row(dm), row(pe_b.shape[1]), _const_spec(gw_b.shape),
                  _const_spec(gb.shape), _const_spec(pp_b.shape)],
        out_specs=row(dm),
        out_shape=jax.ShapeDtypeStruct((ta, dm), F32),
        compiler_params=_cparams(("arbitrary",)),
        name="ple",
    )(x1, pe_b, gw_b, gb, pp_b)


def _route_body(lo_ref, eid_ref, wt_ref):
    lo = lo_ref[...]
    lane_i = lax.broadcasted_iota(I32, lo.shape, 1)
    lane = lane_i.astype(F32)
    ninf = -jnp.inf
    big = float(LANE)
    is_g = lane_i < N_GROUPS
    gl = jnp.where(is_g, lo, ninf)
    gmax = jnp.max(gl, axis=1, keepdims=True)
    gidx = jnp.min(jnp.where(gl == gmax, lane, big), axis=1, keepdims=True)
    gsum = jnp.sum(jnp.where(is_g, jnp.exp(gl - gmax), 0.0), axis=1, keepdims=True)
    gw = 1.0 / gsum
    lo_e = N_GROUPS + EXPERTS_PER_GROUP * gidx
    el = jnp.where(jnp.logical_and(lane >= lo_e, lane < lo_e + EXPERTS_PER_GROUP), lo, ninf)
    v1 = jnp.max(el, axis=1, keepdims=True)
    i1 = jnp.min(jnp.where(el == v1, lane, big), axis=1, keepdims=True)
    el2 = jnp.where(lane == i1, ninf, el)
    v2 = jnp.max(el2, axis=1, keepdims=True)
    i2 = jnp.min(jnp.where(el2 == v2, lane, big), axis=1, keepdims=True)
    e21 = jnp.exp(v2 - v1)
    w1 = gw / (1.0 + e21)
    w2 = gw * e21 / (1.0 + e21)
    eid = jnp.where(lane_i == 0, i1 - N_GROUPS, jnp.where(lane_i == 1, i2 - N_GROUPS, 0.0))
    eid_ref[...] = eid.astype(I32)
    wt_ref[...] = jnp.where(lane_i == 0, w1, jnp.where(lane_i == 1, w2, 0.0))


def _route(logits):
    ta = logits.shape[0]
    tm = TM_ROW
    row = pl.BlockSpec((tm, LANE), lambda i: (i, 0))
    return pl.pallas_call(
        _route_body,
        grid=(ta // tm,),
        in_specs=[row],
        out_specs=[row, row],
        out_shape=[jax.ShapeDtypeStruct((ta, LANE), I32), jax.ShapeDtypeStruct((ta, LANE), F32)],
        compiler_params=_cparams(("arbitrary",)),
        name="route",
    )(logits)


def _slots_body(eid_ref, pos_ref, meta_ref, cnt_ref, run_ref, off_ref, *, tm, tr):
    ph = pl.program_id(0)
    i = pl.program_id(1)
    eid = eid_ref[...]
    lane = lax.broadcasted_iota(I32, (tm, LANE), 1)
    e0 = eid[:, 0:1]
    e1 = eid[:, 1:2]
    oh0 = (lane == e0).astype(F32)
    oh1 = (lane == e1).astype(F32)
    both = oh0 + oh1

    @pl.when(jnp.logical_and(ph == 0, i == 0))
    def _():
        cnt_ref[...] = jnp.zeros_like(cnt_ref)

    @pl.when(ph == 0)
    def _():
        cnt_ref[...] = cnt_ref[...] + jnp.sum(both, axis=0, keepdims=True)

    @pl.when(jnp.logical_and(ph == 1, i == 0))
    def _():
        cnt = cnt_ref[...]
        tiles = jnp.floor((cnt + (tr - 1)) * (1.0 / tr))
        a = lax.broadcasted_iota(I32, (LANE, LANE), 0)
        b = lax.broadcasted_iota(I32, (LANE, LANE), 1)
        upper = (a < b).astype(BF16)
        t8 = jnp.broadcast_to(tiles, (SUBLANE, LANE)).astype(BF16)
        off_tiles = jnp.dot(t8, upper, preferred_element_type=F32)[0:1]
        off_ref[...] = off_tiles * tr
        run_ref[...] = jnp.zeros_like(run_ref)
        end_tiles = off_tiles + tiles
        lane1 = lax.broadcasted_iota(I32, (LANE, LANE), 1)
        jrow = lax.broadcasted_iota(I32, (LANE, LANE), 0).astype(F32)
        done = jnp.logical_and(lane1 < N_EXPERTS, jnp.broadcast_to(end_tiles, (LANE, LANE)) <= jrow)
        te = jnp.sum(done.astype(F32), axis=1, keepdims=True)
        owns = jnp.logical_and(jnp.broadcast_to(off_tiles, (LANE, LANE)) <= jrow,
                               jrow < jnp.broadcast_to(end_tiles, (LANE, LANE)))
        left = jnp.broadcast_to(cnt, (LANE, LANE)) - (jrow - jnp.broadcast_to(off_tiles, (LANE, LANE))) * tr
        nv = jnp.sum(jnp.where(owns, jnp.minimum(left, float(tr)), 0.0), axis=1, keepdims=True)
        meta_ref[...] = jnp.where(lane1 == 0, jnp.broadcast_to(te, (LANE, LANE)),
                                  jnp.broadcast_to(nv, (LANE, LANE))).astype(I32)

    @pl.when(ph == 1)
    def _():
        r = lax.broadcasted_iota(I32, (tm, tm), 0)
        c = lax.broadcasted_iota(I32, (tm, tm), 1)
        lower = (c < r).astype(BF16)
        before = jnp.dot(lower, both.astype(BF16), preferred_element_type=F32) + run_ref[...] + off_ref[...]
        p0 = jnp.sum(oh0 * before, axis=1, keepdims=True)
        p1 = jnp.sum(oh1 * before, axis=1, keepdims=True)
        pos_ref[...] = jnp.where(lane == 0, p0, jnp.where(lane == 1, p1, 0.0)).astype(I32)
        run_ref[...] = run_ref[...] + jnp.sum(both, axis=0, keepdims=True)


def _slots(eid):
    ta = eid.shape[0]
    tm = TM_ROW
    kern = functools.partial(_slots_body, tm=tm, tr=TR)
    return pl.pallas_call(
        kern,
        grid=(2, ta // tm),
        in_specs=[pl.BlockSpec((tm, LANE), lambda ph, i: (i, 0))],
        out_specs=[pl.BlockSpec((tm, LANE), lambda ph, i: (ph * i, 0)),
                   pl.BlockSpec((LANE, LANE), lambda ph, i: (0, 0))],
        out_shape=[jax.ShapeDtypeStruct((ta, LANE), I32), jax.ShapeDtypeStruct((LANE, LANE), I32)],
        scratch_shapes=[pltpu.VMEM((1, LANE), F32), pltpu.VMEM((1, LANE), F32), pltpu.VMEM((1, LANE), F32)],
        compiler_params=_cparams(("arbitrary", "arbitrary")),
        name="slots",
    )(eid)


def _invert_body(p0_ref, p1_ref, src_ref, dst_ref, *, ta, n_rows):
    def init(r, _):
        src_ref[r] = 0
        dst_ref[r] = 0
        return 0

    def fill(t, _):
        a = p0_ref[t]
        b = p1_ref[t]
        src_ref[a] = t
        dst_ref[a] = t
        src_ref[b] = t
        dst_ref[b] = ta + t
        return 0

    lax.fori_loop(0, n_rows, init, 0, unroll=8)
    lax.fori_loop(0, ta, fill, 0, unroll=4)


def _invert(pos0, pos1, n_rows):
    ta = pos0.shape[0]
    smem = pl.BlockSpec(memory_space=pltpu.SMEM)
    return pl.pallas_call(
        functools.partial(_invert_body, ta=ta, n_rows=n_rows),
        in_specs=[smem, smem],
        out_specs=[smem, smem],
        out_shape=[jax.ShapeDtypeStruct((n_rows,), I32), jax.ShapeDtypeStruct((n_rows,), I32)],
        name="invert",
    )(pos0, pos1)


def _expert_body(te_ref, nv_ref, nt_ref, src_ref, dst_ref, x_hbm, wg_ref, wu_ref, wd_ref, y_hbm,
                 xbuf, ybuf, wgb, wub, wdb, gsem, ssem, *, tr):
    j = pl.program_id(0)
    nt = nt_ref[0]
    slot = j % 2

    GATHER, SCATTER = 0, 1
    UNROLL = 8

    def row_copy(jj, sl, r, which):
        n = jj * tr + r
        stage = pl.ds(pl.multiple_of(r * CHUNK, CHUNK), CHUNK)
        if which == GATHER:
            src = pl.ds(pl.multiple_of(src_ref[n] * CHUNK, CHUNK), CHUNK)
            return pltpu.make_async_copy(x_hbm.at[src], xbuf.at[sl, stage], gsem.at[sl])
        dst = pl.ds(pl.multiple_of(dst_ref[n] * CHUNK, CHUNK), CHUNK)
        return pltpu.make_async_copy(ybuf.at[sl, stage], y_hbm.at[dst], ssem.at[sl])

    def for_rows(jj, sl, which, wait):
        def act(r):
            c = row_copy(jj, sl, r, which)
            c.wait() if wait else c.start()

        def block(rb, _):
            for q in range(UNROLL):
                act(rb * UNROLL + q)
            return 0

        if which == GATHER:
            lax.fori_loop(0, tr // UNROLL, block, 0)
            return
        nv = nv_ref[jj]
        full = nv // UNROLL
        lax.fori_loop(0, full, block, 0)
        for q in range(UNROLL - 1):
            @pl.when(full * UNROLL + q < nv)
            def _():
                act(full * UNROLL + q)

    @pl.when(j < nt)
    def _():
        @pl.when(j == 0)
        def _():
            for_rows(0, 0, GATHER, False)

        @pl.when(j + 1 < nt)
        def _():
            for_rows(j + 1, 1 - slot, GATHER, False)

        for_rows(j, slot, GATHER, True)

        @pl.when(jnp.logical_or(j == 0, te_ref[j] != te_ref[jnp.maximum(j - 1, 0)]))
        def _():
            wgb[...] = wg_ref[0].astype(BF16)
            wub[...] = wu_ref[0].astype(BF16)
            wdb[...] = wd_ref[0].astype(BF16)

        x = _load_flat(xbuf, tr, (slot,)).astype(BF16)
        hg = jnp.dot(x, wgb[...], preferred_element_type=F32)
        hu = jnp.dot(x, wub[...], preferred_element_type=F32)
        h = (hg * jax.nn.sigmoid(hg) * hu).astype(BF16)
        y = jnp.dot(h, wdb[...], preferred_element_type=F32)

        @pl.when(j >= 2)
        def _():
            for_rows(j - 2, slot, SCATTER, True)

        _store_flat(ybuf, y, (slot,))
        for_rows(j, slot, SCATTER, False)

        @pl.when(j == nt - 1)
        def _():
            for_rows(j, slot, SCATTER, True)

            @pl.when(j >= 1)
            def _():
                for_rows(j - 1, 1 - slot, SCATTER, True)


def _experts(tile_expert, tile_valid, n_tiles, src, dst, x_flat, w_gate, w_up, w_down, out_rows):
    n_rows = src.shape[0]
    _, dm, ff = w_gate.shape
    tr = TR
    assert n_rows % tr == 0 and dm == CHUNK * LANE
    sp = lambda j, te, nv, nt, s, d: (te[j], 0, 0)
    any_spec = pl.BlockSpec(memory_space=pl.ANY)
    grid_spec = pltpu.PrefetchScalarGridSpec(
        num_scalar_prefetch=5,
        grid=(n_rows // tr,),
        in_specs=[any_spec,
                  pl.BlockSpec((1, dm, ff), sp), pl.BlockSpec((1, dm, ff), sp), pl.BlockSpec((1, ff, dm), sp)],
        out_specs=any_spec,
        scratch_shapes=[pltpu.VMEM((2, tr * CHUNK, LANE), F32), pltpu.VMEM((2, tr * CHUNK, LANE), F32),
                        pltpu.VMEM((dm, ff), BF16), pltpu.VMEM((dm, ff), BF16), pltpu.VMEM((ff, dm), BF16),
                        pltpu.SemaphoreType.DMA((2,)), pltpu.SemaphoreType.DMA((2,))],
    )
    return pl.pallas_call(
        functools.partial(_expert_body, tr=tr),
        grid_spec=grid_spec,
        out_shape=jax.ShapeDtypeStruct((out_rows * CHUNK, LANE), F32),
        compiler_params=_cparams(("arbitrary",)),
        name="experts",
    )(tile_expert, tile_valid, n_tiles, src, dst, x_flat, w_gate, w_up, w_down)


def _final_body(r_ref, y0_ref, y1_ref, wt_ref, g_ref, b_ref, yp_ref, ys_ref, *, n_prompt_tiles):
    i = pl.program_id(0)
    wt = wt_ref[...]
    tm = r_ref.shape[0]
    x = r_ref[...] + wt[:, 0:1] * _load_flat(y0_ref, tm) + wt[:, 1:2] * _load_flat(y1_ref, tm)
    y = _layer_norm(x, g_ref[...], b_ref[...])

    @pl.when(i < n_prompt_tiles)
    def _():
        yp_ref[...] = y

    @pl.when(i >= n_prompt_tiles)
    def _():
        ys_ref[...] = y


def _final(r, y01_flat, wts, ln_g, ln_b, t):
    ta, dm = r.shape
    tm = TM_ROW
    npt = t // tm
    nta = ta // tm
    row = lambda w: pl.BlockSpec((tm, w), lambda i: (i, 0))
    return pl.pallas_call(
        functools.partial(_final_body, n_prompt_tiles=npt),
        grid=(nta,),
        in_specs=[row(dm),
                  pl.BlockSpec((tm * CHUNK, LANE), lambda i: (i, 0)),
                  pl.BlockSpec((tm * CHUNK, LANE), lambda i: (nta + i, 0)),
                  row(LANE), _const_spec(ln_g.shape), _const_spec(ln_b.shape)],
        out_specs=[pl.BlockSpec((tm, dm), lambda i: (jnp.minimum(i, npt - 1), 0)),
                   pl.BlockSpec((tm, dm), lambda i: (0, 0))],
        out_shape=[jax.ShapeDtypeStruct((t, dm), F32), jax.ShapeDtypeStruct((tm, dm), F32)],
        compiler_params=_cparams(("arbitrary",)),
        name="final",
    )(r, y01_flat, y01_flat, wts, ln_g, ln_b)


def _layer(layer, x_p, x_s, pe_p, pe_s, cache_k, cache_v, state, page_table, w):
    t, dm = x_p.shape
    nb = x_s.shape[0]
    wn = N_HEADS * HEAD_W
    ta = t + ROW_PAD
    assert nb <= ROW_PAD and t % ROW_PAD == 0
    lam0 = _lambda_init(layer)
    depth_total = w["depth"]
    alpha = (2.0 * depth_total) ** 0.25
    page = cache_k.shape[1]
    start_pos = page_table.shape[1] * page

    pad = ta - t - nb
    xb = jnp.concatenate([x_p.astype(BF16), x_s.astype(BF16), jnp.zeros((pad, dm), BF16)], axis=0)
    u, qb, k, kb, v, vb, gates = _inproj(xb, w["w_in"].astype(BF16))

    lamv = jnp.concatenate([w["lam_q1"], w["lam_k1"], w["lam_q2"], w["lam_k2"]], axis=0)
    subln_g = w["subln_g"]

    o_p = _prompt_attn(qb, kb, vb, lamv, subln_g, t, lam0)
    kt_pool = jnp.transpose(cache_k, (0, 2, 3, 4, 1)).reshape(cache_k.shape[0], wn, page)
    v_pool = cache_v.reshape(cache_v.shape[0] * page * N_HEADS, HEAD_W)
    o_s = _decode_attn(page_table, lamv, subln_g, qb[t:t + nb], k[t:t + nb], v[t:t + nb],
                       kt_pool, v_pool, lam0)
    o_all = jnp.concatenate([o_p, o_s.astype(BF16), jnp.zeros((pad, wn), BF16)], axis=0)

    d_p = _pool_prompt(u, t)
    d_s = _pool_sample(u[t:t + nb], state, start_pos)
    d_all = jnp.concatenate([d_p, d_s, jnp.zeros((pad, d_p.shape[1]), BF16)], axis=0)

    x1, x1_flat, logits = _mix(d_all, o_all, gates, x_p, x_s, w["pool_w"].astype(BF16), w["pool_scale"],
                               w["w_branch_pool"].astype(BF16), w["w_branch_attn"].astype(BF16),
                               w["w_out"].astype(BF16), w["ln1_g"], w["ln1_b"], w["router_w"], w["router_b"], alpha)

    pe_b = jnp.concatenate([pe_p.astype(BF16), pe_s.astype(BF16), jnp.zeros((pad, pe_p.shape[1]), BF16)], axis=0)
    r = _ple(x1, pe_b, w["ple_gate_w"].astype(BF16), w["ple_gate_b"], w["ple_proj"].astype(BF16), alpha)

    eid, wts = _route(logits)
    pos, meta = _slots(eid)
    n_pairs = TOP_K * ta
    n_rows = n_pairs + N_EXPERTS * TR
    n_row_tiles = n_rows // TR
    assert n_row_tiles <= LANE
    te_raw = meta[:n_row_tiles, 0]
    tile_expert = jnp.minimum(te_raw, N_EXPERTS - 1)
    n_tiles = jnp.sum((te_raw < N_EXPERTS).astype(I32)).reshape(1)

    tile_valid = meta[:n_row_tiles, 1]
    src, dst = _invert(pos[:, 0], pos[:, 1], n_rows)
    y01_flat = _experts(tile_expert, tile_valid, n_tiles, src, dst, x1_flat, w["expert_gate"], w["expert_up"],
                        w["expert_down"], TOP_K * ta)
    y_p, y_s = _final(r, y01_flat, wts, w["ln2_g"], w["ln2_b"], t)
    return y_p, y_s[:nb], k, v, u


def kernel(x_prompt, x_sample, p_prompt, p_sample, cache_k, cache_v, state_pool, page_table, w_in, pool_w, pool_scale, lam_q1, lam_k1, lam_q2, lam_k2, subln_g, w_branch_pool, w_branch_attn, w_out, ln1_g, ln1_b, router_group_w, router_group_b, router_expert_w, router_expert_b, expert_gate, expert_up, expert_down, ple_proj, ple_gate_w, ple_gate_b, ln2_g, ln2_b):
    depth = w_in.shape[0]
    bp, t, dm = x_prompt.shape
    nb, ts, _ = x_sample.shape
    assert bp == 1 and ts == 1 and dm // 256 == N_HEADS
    assert dm % CHUNK == 0 and dm // CHUNK == LANE
    hp = x_prompt.reshape(t, dm)
    hs = x_sample.reshape(nb, dm)
    outs = {n: [] for n in ("kp", "vp", "sp", "ks", "vs", "ss")}
    for l in range(depth):
        wr = jnp.concatenate([router_group_w[l],
                              jnp.transpose(router_expert_w[l], (1, 0, 2)).reshape(dm, N_EXPERTS)], axis=1)
        wr = jnp.pad(wr, ((0, 0), (0, LANE - wr.shape[1])))
        br = jnp.pad(jnp.concatenate([router_group_b[l], router_expert_b[l].reshape(-1)]),
                     (0, LANE - N_GROUPS - N_EXPERTS)).reshape(1, LANE)
        row = lambda a: a[l].reshape(1, -1)
        w = dict(depth=depth, w_in=w_in[l], pool_w=pool_w[l], pool_scale=row(pool_scale),
                 lam_q1=row(lam_q1), lam_k1=row(lam_k1), lam_q2=row(lam_q2), lam_k2=row(lam_k2),
                 subln_g=row(subln_g), w_branch_pool=w_branch_pool[l], w_branch_attn=w_branch_attn[l],
                 w_out=w_out[l], ln1_g=row(ln1_g), ln1_b=row(ln1_b), router_w=wr, router_b=br,
                 expert_gate=expert_gate[l], expert_up=expert_up[l], expert_down=expert_down[l],
                 ple_proj=ple_proj[l], ple_gate_w=ple_gate_w[l], ple_gate_b=row(ple_gate_b),
                 ln2_g=row(ln2_g), ln2_b=row(ln2_b))
        state = state_pool[l]
        hp, hs, k, v, u = _layer(l, hp, hs, p_prompt[l, 0], p_sample[l, :, 0], cache_k[l], cache_v[l],
                                 state, page_table, w)
        outs["kp"].append(k[:t].reshape(1, t, N_HEADS, 2, HEAD_DIM))
        outs["vp"].append(v[:t].reshape(1, t, N_HEADS, HEAD_W))
        outs["sp"].append(u[t - POOL_STATE:t].reshape(1, POOL_STATE, -1))
        outs["ks"].append(k[t:t + nb].reshape(nb, 1, N_HEADS, 2, HEAD_DIM))
        outs["vs"].append(v[t:t + nb].reshape(nb, 1, N_HEADS, HEAD_W))
        outs["ss"].append(jnp.concatenate([state[:, 1:], u[t:t + nb][:, None, :]], axis=1))
    st = lambda n: jnp.stack(outs[n])
    return (hp.reshape(1, t, dm), hs.reshape(nb, 1, dm), st("kp"), st("vp"), st("sp"),
            st("ks"), st("vs"), st("ss"))
```

```python
import functools
import math

import jax
import jax.numpy as jnp
from jax import lax
from jax.experimental import pallas as pl
from jax.experimental.pallas import tpu as pltpu

F32 = jnp.float32
BF16 = jnp.bfloat16
I32 = jnp.int32

N_HEADS = 8
HEAD_DIM = 64
HEAD_W = 2 * HEAD_DIM
ATTN_SCALE = HEAD_DIM ** -0.5
LOG2E = math.log2(math.e)
SUBLN_EPS = 1e-5
LN_EPS = 1e-5
POOL_WINDOWS = (2, 4, 8, 16)
POOL_STATE = max(POOL_WINDOWS) - 1
N_GROUPS = 4
EXPERTS_PER_GROUP = 8
N_EXPERTS = N_GROUPS * EXPERTS_PER_GROUP
TOP_K = 2

LANE = 128
SUBLANE = 8
VMEM_LIMIT = 56 * 1024 * 1024

ROW_PAD = 256
TM_INPROJ = 768
TM_ROW = 256
TQ = 512
TK = 512
HEADS_PER_STEP = 1
TR = 256
HALO = 16
CHUNK = 16
DECODE_GROUP = 4
DECODE_SLOTS = 12


def _lambda_init(layer):
    return 0.8 - 0.6 * math.exp(-0.3 * layer)


def _cparams(sem, vmem=VMEM_LIMIT):
    return pltpu.CompilerParams(dimension_semantics=sem, vmem_limit_bytes=vmem)


def _const_spec(shape):
    nd = len(shape)
    return pl.BlockSpec(shape, lambda *_: (0,) * nd, pipeline_mode=pl.Buffered(1))


def _inproj_body(x_ref, w_ref, u_ref, q_ref, k_ref, kb_ref, v_ref, vb_ref, g_ref):
    j = pl.program_id(0)

    def z():
        return jnp.dot(x_ref[...], w_ref[...], preferred_element_type=F32)

    @pl.when(j == 0)
    def _():
        u_ref[...] = z()

    @pl.when(j == 1)
    def _():
        q_ref[...] = (z() * (ATTN_SCALE * LOG2E)).astype(BF16)

    @pl.when(j == 2)
    def _():
        kz = z()
        k_ref[...] = kz
        kb_ref[...] = kz.astype(BF16)

    @pl.when(j == 3)
    def _():
        vz = z()
        v_ref[...] = vz
        vb_ref[...] = vz.astype(BF16)

    @pl.when(j >= 4)
    def _():
        g_ref[...] = jax.nn.sigmoid(z()).astype(BF16)


def _inproj(xb, w_in_b):
    ta, d = xb.shape
    n_in = w_in_b.shape[1]
    wn = N_HEADS * HEAD_W
    assert n_in == 4 * wn + 2 * d and (2 * d) % wn == 0
    tm = TM_INPROJ
    assert ta % tm == 0
    ni = ta // tm
    nj = n_in // wn

    def once(jo):
        return lambda j, i: (jnp.where(j < jo, 0, jnp.where(j == jo, i, ni - 1)), 0)

    blk = lambda jo: pl.BlockSpec((tm, wn), once(jo))
    g_spec = pl.BlockSpec((tm, wn), lambda j, i: (jnp.where(j < 4, 0, i), jnp.maximum(j - 4, 0)))
    return pl.pallas_call(
        _inproj_body,
        grid=(nj, ni),
        in_specs=[pl.BlockSpec((tm, d), lambda j, i: (i, 0)),
                  pl.BlockSpec((d, wn), lambda j, i: (0, j))],
        out_specs=[blk(0), blk(1), blk(2), blk(2), blk(3), blk(3), g_spec],
        out_shape=[jax.ShapeDtypeStruct((ta, wn), F32),
                   jax.ShapeDtypeStruct((ta, wn), BF16),
                   jax.ShapeDtypeStruct((ta, wn), F32),
                   jax.ShapeDtypeStruct((ta, wn), BF16),
                   jax.ShapeDtypeStruct((ta, wn), F32),
                   jax.ShapeDtypeStruct((ta, wn), BF16),
                   jax.ShapeDtypeStruct((ta, 2 * d), BF16)],
        compiler_params=_cparams(("arbitrary", "arbitrary")),
        name="inproj",
    )(xb, w_in_b)


def _lam_value(lamv_ref, lam0):
    lv = lamv_ref[...]
    a = jnp.sum(lv[0:1] * lv[1:2], axis=1, keepdims=True)
    b = jnp.sum(lv[2:3] * lv[3:4], axis=1, keepdims=True)
    return jnp.exp(a) - jnp.exp(b) + lam0


def _subln(o, g, lam0):
    ms = jnp.mean(o * o, axis=-1, keepdims=True)
    return o * lax.rsqrt(ms + SUBLN_EPS) * g * (1.0 - lam0)


def _prompt_attn_body(lamv_ref, gcol_ref, q_ref, k_ref, v_ref, o_ref, vt_ref, q2t_ref, s_even, s_odd,
                      m_ref, l_ref, acc_ref, *, tq, tk, hp, lam0):
    i = pl.program_id(1)
    nkv = v_ref.shape[0] // tk
    heads = [pl.ds(hh * HEAD_W, HEAD_W) for hh in range(hp)]

    @pl.when(i == 0)
    def _():
        def tr(jb, _):
            for hh in range(hp):
                blk = v_ref[pl.ds(pl.multiple_of(jb * tk, tk), tk), heads[hh]].astype(F32)
                vt_ref[hh, jb] = blk.T.astype(BF16)
            return 0
        lax.fori_loop(0, nkv, tr, 0)

    drow = lax.broadcasted_iota(I32, (HEAD_W, tq), 0)
    for hh in range(hp):
        qt = q_ref[:, heads[hh]].astype(F32).T
        q2t_ref[hh] = jnp.concatenate([jnp.where(drow < HEAD_DIM, qt, 0.0),
                                       jnp.where(drow >= HEAD_DIM, qt, 0.0)], axis=1).astype(BF16)
    m_ref[...] = jnp.full(m_ref.shape, -jnp.inf, F32)
    l_ref[...] = jnp.zeros(l_ref.shape, F32)
    acc_ref[...] = jnp.zeros(acc_ref.shape, F32)
    n_full = (i * tq) // tk

    def scores(j, s_ref):
        rows = pl.ds(pl.multiple_of(j * tk, tk), tk)
        for hh in range(hp):
            s_ref[hh] = jnp.dot(k_ref[rows, heads[hh]], q2t_ref[hh], preferred_element_type=F32)

    def absorb(j, s_ref, masked):
        for hh in range(hp):
            s = s_ref[hh]
            if masked:
                kpos = n_full * tk + lax.broadcasted_iota(I32, (tk, 2 * tq), 0)
                qcol = lax.broadcasted_iota(I32, (tk, 2 * tq), 1)
                qpos = i * tq + jnp.where(qcol >= tq, qcol - tq, qcol)
                s = jnp.where(kpos <= qpos, s, -jnp.inf)
            m = m_ref[hh]
            m_new = jnp.maximum(m, jnp.max(s, axis=0, keepdims=True))
            alpha = jnp.exp2(m - m_new)
            p = jnp.exp2(s - m_new)
            l_ref[hh] = alpha * l_ref[hh] + jnp.sum(p, axis=0, keepdims=True)
            acc_ref[hh] = alpha * acc_ref[hh] + jnp.dot(vt_ref[hh, j], p.astype(BF16), preferred_element_type=F32)
            m_ref[hh] = m_new

    def by_parity(j, fn):
        @pl.when(j % 2 == 0)
        def _():
            fn(s_even, s_odd)

        @pl.when(j % 2 == 1)
        def _():
            fn(s_odd, s_even)

    scores(0, s_even)

    def body(j, _):
        def stage(cur, nxt):
            scores(j + 1, nxt)
            absorb(j, cur, False)
        by_parity(j, stage)
        return 0

    lax.fori_loop(0, n_full, body, 0)
    by_parity(n_full, lambda cur, nxt: absorb(n_full, cur, True))

    lam = _lam_value(lamv_ref, lam0)
    for hh in range(hp):
        on = acc_ref[hh] / l_ref[hh]
        ot = on[:, :tq] - lam * on[:, tq:]
        ms = jnp.mean(ot * ot, axis=0, keepdims=True)
        ot = ot * lax.rsqrt(ms + SUBLN_EPS) * gcol_ref[...] * (1.0 - lam0)
        o_ref[:, heads[hh]] = ot.T.astype(o_ref.dtype)


def _prompt_attn(qb, kb, vb, lamv, subln_g, t, lam0):
    tq, tk, hp = TQ, TK, HEADS_PER_STEP
    assert t % tk == 0 and tk % tq == 0 and N_HEADS % hp == 0
    kern = functools.partial(_prompt_attn_body, tq=tq, tk=tk, hp=hp, lam0=lam0)
    gcol = subln_g.reshape(HEAD_W, 1)
    return pl.pallas_call(
        kern,
        grid=(N_HEADS // hp, t // tq),
        in_specs=[_const_spec(lamv.shape), _const_spec(gcol.shape),
                  pl.BlockSpec((tq, hp * HEAD_W), lambda h, i: (i, h)),
                  pl.BlockSpec((t, hp * HEAD_W), lambda h, i: (0, h)),
                  pl.BlockSpec((t, hp * HEAD_W), lambda h, i: (0, h))],
        out_specs=pl.BlockSpec((tq, hp * HEAD_W), lambda h, i: (i, h)),
        out_shape=jax.ShapeDtypeStruct((t, N_HEADS * HEAD_W), BF16),
        scratch_shapes=[pltpu.VMEM((hp, t // tk, HEAD_W, tk), BF16),
                        pltpu.VMEM((hp, HEAD_W, 2 * tq), BF16),
                        pltpu.VMEM((hp, tk, 2 * tq), F32), pltpu.VMEM((hp, tk, 2 * tq), F32),
                        pltpu.VMEM((hp, 1, 2 * tq), F32), pltpu.VMEM((hp, 1, 2 * tq), F32),
                        pltpu.VMEM((hp, HEAD_W, 2 * tq), F32)],
        compiler_params=_cparams(("arbitrary", "arbitrary")),
        name="prompt_attn",
    )(lamv, gcol, qb, kb, vb)


def _decode_attn_body(pt_ref, lamv_ref, g_ref, q_ref, qc_ref, kn_ref, vn_ref, kt_hbm, v_hbm, o_ref,
                      kbuf, vbuf, sem, *, n_pages, page, n_slots, group, lam0):
    b = pl.program_id(0)
    nb = pl.num_programs(0)
    nr = 2 * N_HEADS
    wn = N_HEADS * HEAD_W
    vrows = page * N_HEADS
    ahead = n_slots - group
    total = nb * n_pages
    assert n_pages % group == 0 and n_slots % group == 0 and ahead >= group

    def copies(g):
        slot = g % n_slots
        pg = pt_ref[g // n_pages, g % n_pages]
        ck = pltpu.make_async_copy(kt_hbm.at[pg], kbuf.at[slot], sem.at[0, slot])
        cv = pltpu.make_async_copy(v_hbm.at[pl.ds(pl.multiple_of(pg * vrows, vrows), vrows)],
                                   vbuf.at[slot], sem.at[1, slot])
        return ck, cv

    def start(g):
        for c in copies(g):
            c.start()

    @pl.when(b == 0)
    def _():
        for g in range(ahead):
            start(g)

    rr = lax.broadcasted_iota(I32, (nr, wn), 0)
    cc = lax.broadcasted_iota(I32, (nr, wn), 1)
    qrow = q_ref[0]
    qbd = jnp.where(cc // HEAD_DIM == rr, jnp.broadcast_to(qrow.astype(F32), (nr, wn)), 0.0).astype(BF16)
    rhead = lax.broadcasted_iota(I32, (nr, HEAD_W), 0) // 2
    qcol = jnp.broadcast_to(qc_ref[0], (wn, page))
    tok = lax.broadcasted_iota(I32, (page, vrows), 0)
    vrow = lax.broadcasted_iota(I32, (page, vrows), 1)
    spread = jnp.where(vrow // N_HEADS == tok, 1.0, 0.0).astype(BF16)
    own_head = lax.broadcasted_iota(I32, (nr, vrows), 1) % N_HEADS == lax.broadcasted_iota(I32, (nr, vrows), 0) // 2

    def body(jg, carry):
        m, l, acc = carry
        g0 = b * n_pages + jg * group
        slots = []
        ss = []
        for a in range(group):
            g = g0 + a

            @pl.when(g + ahead < total)
            def _():
                start(g + ahead)

            for c in copies(g):
                c.wait()
            slot = g % n_slots
            slots.append(slot)
            prod = kbuf[slot] * qcol
            ss.append(jnp.sum(prod.reshape(nr, HEAD_DIM, page), axis=1))
        m_new = m
        for s in ss:
            m_new = jnp.maximum(m_new, jnp.max(s, axis=1, keepdims=True))
        alpha = jnp.exp2(m - m_new)
        l = alpha * l
        acc = alpha * acc
        for s, slot in zip(ss, slots):
            p = jnp.exp2(s - m_new)
            l = l + jnp.sum(p, axis=1, keepdims=True)
            pe = jnp.dot(p.astype(BF16), spread, preferred_element_type=F32)
            pe = jnp.where(own_head, pe, 0.0).astype(BF16)
            acc = acc + jnp.dot(pe, vbuf[slot].astype(BF16), preferred_element_type=F32)
        return m_new, l, acc

    init = (jnp.full((nr, 1), -jnp.inf, F32), jnp.zeros((nr, 1), F32), jnp.zeros((nr, HEAD_W), F32))
    m, l, acc = lax.fori_loop(0, n_pages // group, body, init)

    kn = jnp.broadcast_to(kn_ref[0], (nr, wn))
    s_new = jnp.sum(qbd.astype(F32) * kn, axis=1, keepdims=True)
    vn = vn_ref[0]
    vnew = jnp.zeros((nr, HEAD_W), F32)
    for h in range(N_HEADS):
        vnew = vnew + jnp.where(rhead == h, jnp.broadcast_to(vn[:, h * HEAD_W:(h + 1) * HEAD_W], (nr, HEAD_W)), 0.0)
    m_new = jnp.maximum(m, s_new)
    alpha = jnp.exp2(m - m_new)
    p_new = jnp.exp2(s_new - m_new)
    l = alpha * l + p_new
    acc = alpha * acc + p_new * vnew
    on = acc / l
    lam = _lam_value(lamv_ref, lam0)
    o = on - lam * pltpu.roll(on, nr - 1, 0)
    o_ref[0] = _subln(o, g_ref[...], lam0)


def _decode_attn(page_table, lamv, subln_g, q_s, k_s, v_s, kt_pool, v_pool, lam0):
    nb, n_pages = page_table.shape
    wn = N_HEADS * HEAD_W
    page = kt_pool.shape[2]
    nr = 2 * N_HEADS
    n_slots = DECODE_SLOTS
    kern = functools.partial(_decode_attn_body, n_pages=n_pages, page=page, n_slots=n_slots,
                             group=DECODE_GROUP, lam0=lam0)
    grid_spec = pltpu.PrefetchScalarGridSpec(
        num_scalar_prefetch=1,
        grid=(nb,),
        in_specs=[pl.BlockSpec(lamv.shape, lambda b, pt: (0, 0)),
                  pl.BlockSpec(subln_g.shape, lambda b, pt: (0, 0)),
                  pl.BlockSpec((1, 1, wn), lambda b, pt: (b, 0, 0)),
                  pl.BlockSpec((1, wn, 1), lambda b, pt: (b, 0, 0)),
                  pl.BlockSpec((1, 1, wn), lambda b, pt: (b, 0, 0)),
                  pl.BlockSpec((1, 1, wn), lambda b, pt: (b, 0, 0)),
                  pl.BlockSpec(memory_space=pl.ANY),
                  pl.BlockSpec(memory_space=pl.ANY)],
        out_specs=pl.BlockSpec((1, nr, HEAD_W), lambda b, pt: (b, 0, 0)),
        scratch_shapes=[pltpu.VMEM((n_slots, wn, page), F32),
                        pltpu.VMEM((n_slots, page * N_HEADS, HEAD_W), F32),
                        pltpu.SemaphoreType.DMA((2, n_slots))],
    )
    out = pl.pallas_call(
        kern,
        grid_spec=grid_spec,
        out_shape=jax.ShapeDtypeStruct((nb, nr, HEAD_W), F32),
        compiler_params=_cparams(("arbitrary",)),
        name="decode_attn",
    )(page_table, lamv, subln_g, q_s.reshape(nb, 1, wn), q_s.astype(F32).reshape(nb, wn, 1),
      k_s.reshape(nb, 1, wn), v_s.reshape(nb, 1, wn), kt_pool, v_pool)
    return out[:, 0::2, :].reshape(nb, wn)


def _pool_prompt_body(u_ref, halo_ref, d_ref, ext_ref, *, tm, gw):
    i = pl.program_id(0)
    ext_ref[pl.ds(HALO, tm), :] = u_ref[...]

    @pl.when(i == 0)
    def _():
        ext_ref[pl.ds(0, HALO), :] = jnp.zeros((HALO, u_ref.shape[1]), F32)

    @pl.when(i > 0)
    def _():
        ext_ref[pl.ds(0, HALO), :] = halo_ref[...]

    pos = i * tm + lax.broadcasted_iota(I32, (tm, 1), 0)
    for g, w in enumerate(POOL_WINDOWS):
        cols = pl.ds(g * gw, gw)
        tok = ext_ref[pl.ds(HALO, tm), cols]
        s = tok
        for back in range(1, w):
            s = s + ext_ref[pl.ds(HALO - back, tm), cols]
        inv = 1.0 / jnp.minimum(pos + 1, w).astype(F32)
        d_ref[:, cols] = (s * inv - tok).astype(d_ref.dtype)


def _pool_prompt(u_all, t):
    wp = u_all.shape[1]
    gw = wp // len(POOL_WINDOWS)
    tm = TM_ROW
    assert t % tm == 0 and tm % HALO == 0 and gw % LANE == 0
    kern = functools.partial(_pool_prompt_body, tm=tm, gw=gw)
    return pl.pallas_call(
        kern,
        grid=(t // tm,),
        in_specs=[pl.BlockSpec((tm, wp), lambda i: (i, 0)),
                  pl.BlockSpec((HALO, wp), lambda i: (jnp.maximum(i * (tm // HALO) - 1, 0), 0))],
        out_specs=pl.BlockSpec((tm, wp), lambda i: (i, 0)),
        out_shape=jax.ShapeDtypeStruct((t, wp), BF16),
        scratch_shapes=[pltpu.VMEM((tm + HALO, wp), F32)],
        compiler_params=_cparams(("arbitrary",)),
        name="pool_prompt",
    )(u_all, u_all)


def _pool_sample_body(u_ref, st_ref, d_ref, *, gw, start_pos):
    st = st_ref[...]
    u = u_ref[...]
    ridx = lax.broadcasted_iota(I32, st.shape, 1)
    for g, w in enumerate(POOL_WINDOWS):
        hist = jnp.sum(jnp.where(ridx >= POOL_STATE - (w - 1), st, 0.0), axis=1)
        cnt = float(min(start_pos + 1, w))
        d = (hist + u) * (1.0 / cnt) - u
        d_ref[:, g * gw:(g + 1) * gw] = d[:, g * gw:(g + 1) * gw].astype(d_ref.dtype)


def _pool_sample(u_s, state, start_pos):
    nb, wp = u_s.shape
    gw = wp // len(POOL_WINDOWS)
    kern = functools.partial(_pool_sample_body, gw=gw, start_pos=start_pos)
    return pl.pallas_call(
        kern,
        grid=(1,),
        in_specs=[_const_spec(u_s.shape), _const_spec(state.shape)],
        out_specs=_const_spec((nb, wp)),
        out_shape=jax.ShapeDtypeStruct((nb, wp), BF16),
        compiler_params=_cparams(("arbitrary",)),
        name="pool_sample",
    )(u_s, state)


def _store_flat(flat_ref, x, lead=()):
    rows = x.shape[0]
    for c in range(CHUNK):
        flat_ref[(*lead, pl.ds(c, rows, stride=CHUNK), slice(None))] = x[:, c * LANE:(c + 1) * LANE]


def _load_flat(flat_ref, rows, lead=()):
    return jnp.concatenate(
        [flat_ref[(*lead, pl.ds(c, rows, stride=CHUNK), slice(None))] for c in range(CHUNK)], axis=1)


def _layer_norm(x, g, b):
    mu = jnp.mean(x, axis=-1, keepdims=True)
    xc = x - mu
    var = jnp.mean(xc * xc, axis=-1, keepdims=True)
    return xc * lax.rsqrt(var + LN_EPS) * g + b


def _mix_body(d_ref, o_ref, g_ref, xp_ref, xs_ref, pw_ref, ps_ref, wbp_ref, wba_ref, wo_ref,
              lg_ref, lb_ref, wr_ref, br_ref, x1_ref, x1f_ref, lo_ref, *, n_prompt_tiles, alpha):
    i = pl.program_id(0)
    tm, dm = x1_ref.shape
    ng = len(POOL_WINDOWS)
    gw = d_ref.shape[1] // ng
    ys = []
    for g in range(ng):
        ys.append(jnp.dot(d_ref[:, g * gw:(g + 1) * gw], pw_ref[g], preferred_element_type=F32))
    pool_y = (jnp.concatenate(ys, axis=1) * ps_ref[...]).astype(BF16)
    a = jnp.dot(pool_y, wbp_ref[...], preferred_element_type=F32)
    bb = jnp.dot(o_ref[...], wba_ref[...], preferred_element_type=F32)
    gates = g_ref[...].astype(F32)
    merged = (gates[:, :dm] * a + gates[:, dm:] * bb).astype(BF16)
    mix = jnp.dot(merged, wo_ref[...], preferred_element_type=F32)

    def finish(x):
        x1 = _layer_norm(alpha * x + mix, lg_ref[...], lb_ref[...])
        x1_ref[...] = x1
        _store_flat(x1f_ref, x1)
        hi = x1.astype(BF16)
        lo = (x1 - hi.astype(F32)).astype(BF16)
        dot = functools.partial(jnp.dot, preferred_element_type=F32)
        lo_ref[...] = dot(hi, wr_ref[0]) + (dot(lo, wr_ref[0]) + dot(hi, wr_ref[1])) + br_ref[...]

    @pl.when(i < n_prompt_tiles)
    def _():
        finish(xp_ref[...])

    @pl.when(i >= n_prompt_tiles)
    def _():
        ns = xs_ref.shape[0]
        finish(jnp.concatenate([xs_ref[...], jnp.zeros((tm - ns, dm), F32)], axis=0))


def _mix(d_all, o_all, gates, x_p, x_s, pool_w_b, pool_scale, wbp, wba, wo, ln_g, ln_b, wr, br, alpha):
    ta = d_all.shape[0]
    t, dm = x_p.shape
    ns = x_s.shape[0]
    tm = TM_ROW
    assert ta % tm == 0 and t % tm == 0 and ns % SUBLANE == 0 and ns <= tm and ta == t + tm
    npt = t // tm
    wp = d_all.shape[1]
    wa = o_all.shape[1]
    row = lambda w: pl.BlockSpec((tm, w), lambda i: (i, 0))
    kern = functools.partial(_mix_body, n_prompt_tiles=npt, alpha=alpha)
    return pl.pallas_call(
        kern,
        grid=(ta // tm,),
        in_specs=[row(wp), row(wa), row(2 * dm),
                  pl.BlockSpec((tm, dm), lambda i: (jnp.minimum(i, npt - 1), 0)),
                  _const_spec(x_s.shape),
                  _const_spec(pool_w_b.shape), _const_spec(pool_scale.shape),
                  _const_spec(wbp.shape), _const_spec(wba.shape), _const_spec(wo.shape),
                  _const_spec(ln_g.shape), _const_spec(ln_b.shape),
                  _const_spec(wr.shape), _const_spec(br.shape)],
        out_specs=[row(dm), pl.BlockSpec((tm * CHUNK, LANE), lambda i: (i, 0)), row(LANE)],
        out_shape=[jax.ShapeDtypeStruct((ta, dm), F32),
                   jax.ShapeDtypeStruct((ta * CHUNK, LANE), F32),
                   jax.ShapeDtypeStruct((ta, LANE), F32)],
        compiler_params=_cparams(("arbitrary",)),
        name="branch_mix",
    )(d_all, o_all, gates, x_p, x_s, pool_w_b, pool_scale, wbp, wba, wo, ln_g, ln_b, wr, br)


def _ple_body(x1_ref, pe_ref, gw_ref, gb_ref, pp_ref, r_ref, *, alpha):
    x1 = x1_ref[...]
    gate = jax.nn.sigmoid(jnp.dot(x1.astype(BF16), gw_ref[...], preferred_element_type=F32) + gb_ref[...])
    emb = jnp.dot(pe_ref[...], pp_ref[...], preferred_element_type=F32)
    r_ref[...] = alpha * x1 + gate * emb


def _ple(x1, pe_b, gw_b, gb, pp_b, alpha):
    ta, dm = x1.shape
    tm = TM_ROW
    row = lambda w: pl.BlockSpec((tm, w), lambda i: (i, 0))
    return pl.pallas_call(
        functools.partial(_ple_body, alpha=alpha),
        grid=(ta // tm,),
        in_specs=[row(dm), row(pe_b.shape[1]), _const_spec(gw_b.shape),
                  _const_spec(gb.shape), _const_spec(pp_b.shape)],
        out_specs=row(dm),
        out_shape=jax.ShapeDtypeStruct((ta, dm), F32),
        compiler_params=_cparams(("arbitrary",)),
        name="ple",
    )(x1, pe_b, gw_b, gb, pp_b)


def _route_body(lo_ref, eid_ref, wt_ref):
    lo = lo_ref[...]
    lane_i = lax.broadcasted_iota(I32, lo.shape, 1)
    lane = lane_i.astype(F32)
    ninf = -jnp.inf
    big = float(LANE)
    is_g = lane_i < N_GROUPS
    gl = jnp.where(is_g, lo, ninf)
    gmax = jnp.max(gl, axis=1, keepdims=True)
    gidx = jnp.min(jnp.where(gl == gmax, lane, big), axis=1, keepdims=True)
    gsum = jnp.sum(jnp.where(is_g, jnp.exp(gl - gmax), 0.0), axis=1, keepdims=True)
    gw = 1.0 / gsum
    lo_e = N_GROUPS + EXPERTS_PER_GROUP * gidx
    el = jnp.where(jnp.logical_and(lane >= lo_e, lane < lo_e + EXPERTS_PER_GROUP), lo, ninf)
    v1 = jnp.max(el, axis=1, keepdims=True)
    i1 = jnp.min(jnp.where(el == v1, lane, big), axis=1, keepdims=True)
    el2 = jnp.where(lane == i1, ninf, el)
    v2 = jnp.max(el2, axis=1, keepdims=True)
    i2 = jnp.min(jnp.where(el2 == v2, lane, big), axis=1, keepdims=True)
    e21 = jnp.exp(v2 - v1)
    w1 = gw / (1.0 + e21)
    w2 = gw * e21 / (1.0 + e21)
    eid = jnp.where(lane_i == 0, i1 - N_GROUPS, jnp.where(lane_i == 1, i2 - N_GROUPS, 0.0))
    eid_ref[...] = eid.astype(I32)
    wt_ref[...] = jnp.where(lane_i == 0, w1, jnp.where(lane_i == 1, w2, 0.0))


def _route(logits):
    ta = logits.shape[0]
    tm = TM_ROW
    row = pl.BlockSpec((tm, LANE), lambda i: (i, 0))
    return pl.pallas_call(
        _route_body,
        grid=(ta // tm,),
        in_specs=[row],
        out_specs=[row, row],
        out_shape=[jax.ShapeDtypeStruct((ta, LANE), I32), jax.ShapeDtypeStruct((ta, LANE), F32)],
        compiler_params=_cparams(("arbitrary",)),
        name="route",
    )(logits)


def _slots_body(eid_ref, pos_ref, meta_ref, cnt_ref, run_ref, off_ref, *, tm, tr):
    ph = pl.program_id(0)
    i = pl.program_id(1)
    eid = eid_ref[...]
    lane = lax.broadcasted_iota(I32, (tm, LANE), 1)
    e0 = eid[:, 0:1]
    e1 = eid[:, 1:2]
    oh0 = (lane == e0).astype(F32)
    oh1 = (lane == e1).astype(F32)
    both = oh0 + oh1

    @pl.when(jnp.logical_and(ph == 0, i == 0))
    def _():
        cnt_ref[...] = jnp.zeros_like(cnt_ref)

    @pl.when(ph == 0)
    def _():
        cnt_ref[...] = cnt_ref[...] + jnp.sum(both, axis=0, keepdims=True)

    @pl.when(jnp.logical_and(ph == 1, i == 0))
    def _():
        cnt = cnt_ref[...]
        tiles = jnp.floor((cnt + (tr - 1)) * (1.0 / tr))
        a = lax.broadcasted_iota(I32, (LANE, LANE), 0)
        b = lax.broadcasted_iota(I32, (LANE, LANE), 1)
        upper = (a < b).astype(BF16)
        t8 = jnp.broadcast_to(tiles, (SUBLANE, LANE)).astype(BF16)
        off_tiles = jnp.dot(t8, upper, preferred_element_type=F32)[0:1]
        off_ref[...] = off_tiles * tr
        run_ref[...] = jnp.zeros_like(run_ref)
        end_tiles = off_tiles + tiles
        lane1 = lax.broadcasted_iota(I32, (LANE, LANE), 1)
        jrow = lax.broadcasted_iota(I32, (LANE, LANE), 0).astype(F32)
        done = jnp.logical_and(lane1 < N_EXPERTS, jnp.broadcast_to(end_tiles, (LANE, LANE)) <= jrow)
        te = jnp.sum(done.astype(F32), axis=1, keepdims=True)
        owns = jnp.logical_and(jnp.broadcast_to(off_tiles, (LANE, LANE)) <= jrow,
                               jrow < jnp.broadcast_to(end_tiles, (LANE, LANE)))
        left = jnp.broadcast_to(cnt, (LANE, LANE)) - (jrow - jnp.broadcast_to(off_tiles, (LANE, LANE))) * tr
        nv = jnp.sum(jnp.where(owns, jnp.minimum(left, float(tr)), 0.0), axis=1, keepdims=True)
        meta_ref[...] = jnp.where(lane1 == 0, jnp.broadcast_to(te, (LANE, LANE)),
                                  jnp.broadcast_to(nv, (LANE, LANE))).astype(I32)

    @pl.when(ph == 1)
    def _():
        r = lax.broadcasted_iota(I32, (tm, tm), 0)
        c = lax.broadcasted_iota(I32, (tm, tm), 1)
        lower = (c < r).astype(BF16)
        before = jnp.dot(lower, both.astype(BF16), preferred_element_type=F32) + run_ref[...] + off_ref[...]
        p0 = jnp.sum(oh0 * before, axis=1, keepdims=True)
        p1 = jnp.sum(oh1 * before, axis=1, keepdims=True)
        pos_ref[...] = jnp.where(lane == 0, p0, jnp.where(lane == 1, p1, 0.0)).astype(I32)
        run_ref[...] = run_ref[...] + jnp.sum(both, axis=0, keepdims=True)


def _slots(eid):
    ta = eid.shape[0]
    tm = TM_ROW
    kern = functools.partial(_slots_body, tm=tm, tr=TR)
    return pl.pallas_call(
        kern,
        grid=(2, ta // tm),
        in_specs=[pl.BlockSpec((tm, LANE), lambda ph, i: (i, 0))],
        out_specs=[pl.BlockSpec((tm, LANE), lambda ph, i: (ph * i, 0)),
                   pl.BlockSpec((LANE, LANE), lambda ph, i: (0, 0))],
        out_shape=[jax.ShapeDtypeStruct((ta, LANE), I32), jax.ShapeDtypeStruct((LANE, LANE), I32)],
        scratch_shapes=[pltpu.VMEM((1, LANE), F32), pltpu.VMEM((1, LANE), F32), pltpu.VMEM((1, LANE), F32)],
        compiler_params=_cparams(("arbitrary", "arbitrary")),
        name="slots",
    )(eid)


def _invert_body(p0_ref, p1_ref, src_ref, dst_ref, *, ta, n_rows):
    def init(r, _):
        src_ref[r] = 0
        dst_ref[r] = 0
        return 0

    def fill(t, _):
        a = p0_ref[t]
        b = p1_ref[t]
        src_ref[a] = t
        dst_ref[a] = t
        src_ref[b] = t
        dst_ref[b] = ta + t
        return 0

    lax.fori_loop(0, n_rows, init, 0, unroll=8)
    lax.fori_loop(0, ta, fill, 0, unroll=4)


def _invert(pos0, pos1, n_rows):
    ta = pos0.shape[0]
    smem = pl.BlockSpec(memory_space=pltpu.SMEM)
    return pl.pallas_call(
        functools.partial(_invert_body, ta=ta, n_rows=n_rows),
        in_specs=[smem, smem],
        out_specs=[smem, smem],
        out_shape=[jax.ShapeDtypeStruct((n_rows,), I32), jax.ShapeDtypeStruct((n_rows,), I32)],
        name="invert",
    )(pos0, pos1)


def _expert_body(te_ref, nv_ref, nt_ref, src_ref, dst_ref, x_hbm, wg_ref, wu_ref, wd_ref, y_hbm,
                 xbuf, ybuf, wgb, wub, wdb, gsem, ssem, *, tr):
    j = pl.program_id(0)
    nt = nt_ref[0]
    slot = j % 2

    GATHER, SCATTER = 0, 1
    UNROLL = 8

    def row_copy(jj, sl, r, which):
        n = jj * tr + r
        stage = pl.ds(pl.multiple_of(r * CHUNK, CHUNK), CHUNK)
        if which == GATHER:
            src = pl.ds(pl.multiple_of(src_ref[n] * CHUNK, CHUNK), CHUNK)
            return pltpu.make_async_copy(x_hbm.at[src], xbuf.at[sl, stage], gsem.at[sl])
        dst = pl.ds(pl.multiple_of(dst_ref[n] * CHUNK, CHUNK), CHUNK)
        return pltpu.make_async_copy(ybuf.at[sl, stage], y_hbm.at[dst], ssem.at[sl])

    def for_rows(jj, sl, which, wait):
        def act(r):
            c = row_copy(jj, sl, r, which)
            c.wait() if wait else c.start()

        def block(rb, _):
            for q in range(UNROLL):
                act(rb * UNROLL + q)
            return 0

        if which == GATHER:
            lax.fori_loop(0, tr // UNROLL, block, 0)
            return
        nv = nv_ref[jj]
        full = nv // UNROLL
        lax.fori_loop(0, full, block, 0)
        for q in range(UNROLL - 1):
            @pl.when(full * UNROLL + q < nv)
            def _():
                act(full * UNROLL + q)

    @pl.when(j < nt)
    def _():
        @pl.when(j == 0)
        def _():
            for_rows(0, 0, GATHER, False)

        @pl.when(j + 1 < nt)
        def _():
            for_rows(j + 1, 1 - slot, GATHER, False)

        for_rows(j, slot, GATHER, True)

        @pl.when(jnp.logical_or(j == 0, te_ref[j] != te_ref[jnp.maximum(j - 1, 0)]))
        def _():
            wgb[...] = wg_ref[0].astype(BF16)
            wub[...] = wu_ref[0].astype(BF16)
            wdb[...] = wd_ref[0].astype(BF16)

        x = _load_flat(xbuf, tr, (slot,)).astype(BF16)
        hg = jnp.dot(x, wgb[...], preferred_element_type=F32)
        hu = jnp.dot(x, wub[...], preferred_element_type=F32)
        h = (hg * jax.nn.sigmoid(hg) * hu).astype(BF16)
        y = jnp.dot(h, wdb[...], preferred_element_type=F32)

        @pl.when(j >= 2)
        def _():
            for_rows(j - 2, slot, SCATTER, True)

        _store_flat(ybuf, y, (slot,))
        for_rows(j, slot, SCATTER, False)

        @pl.when(j == nt - 1)
        def _():
            for_rows(j, slot, SCATTER, True)

            @pl.when(j >= 1)
            def _():
                for_rows(j - 1, 1 - slot, SCATTER, True)


def _experts(tile_expert, tile_valid, n_tiles, src, dst, x_flat, w_gate, w_up, w_down, out_rows):
    n_rows = src.shape[0]
    _, dm, ff = w_gate.shape
    tr = TR
    assert n_rows % tr == 0 and dm == CHUNK * LANE
    sp = lambda j, te, nv, nt, s, d: (te[j], 0, 0)
    any_spec = pl.BlockSpec(memory_space=pl.ANY)
    grid_spec = pltpu.PrefetchScalarGridSpec(
        num_scalar_prefetch=5,
        grid=(n_rows // tr,),
        in_specs=[any_spec,
                  pl.BlockSpec((1, dm, ff), sp), pl.BlockSpec((1, dm, ff), sp), pl.BlockSpec((1, ff, dm), sp)],
        out_specs=any_spec,
        scratch_shapes=[pltpu.VMEM((2, tr * CHUNK, LANE), F32), pltpu.VMEM((2, tr * CHUNK, LANE), F32),
                        pltpu.VMEM((dm, ff), BF16), pltpu.VMEM((dm, ff), BF16), pltpu.VMEM((ff, dm), BF16),
                        pltpu.SemaphoreType.DMA((2,)), pltpu.SemaphoreType.DMA((2,))],
    )
    return pl.pallas_call(
        functools.partial(_expert_body, tr=tr),
        grid_spec=grid_spec,
        out_shape=jax.ShapeDtypeStruct((out_rows * CHUNK, LANE), F32),
        compiler_params=_cparams(("arbitrary",)),
        name="experts",
    )(tile_expert, tile_valid, n_tiles, src, dst, x_flat, w_gate, w_up, w_down)


def _final_body(r_ref, y0_ref, y1_ref, wt_ref, g_ref, b_ref, yp_ref, ys_ref, *, n_prompt_tiles):
    i = pl.program_id(0)
    wt = wt_ref[...]
    tm = r_ref.shape[0]
    x = r_ref[...] + wt[:, 0:1] * _load_flat(y0_ref, tm) + wt[:, 1:2] * _load_flat(y1_ref, tm)
    y = _layer_norm(x, g_ref[...], b_ref[...])

    @pl.when(i < n_prompt_tiles)
    def _():
        yp_ref[...] = y

    @pl.when(i >= n_prompt_tiles)
    def _():
        ys_ref[...] = y


def _final(r, y01_flat, wts, ln_g, ln_b, t):
    ta, dm = r.shape
    tm = TM_ROW
    npt = t // tm
    nta = ta // tm
    row = lambda w: pl.BlockSpec((tm, w), lambda i: (i, 0))
    return pl.pallas_call(
        functools.partial(_final_body, n_prompt_tiles=npt),
        grid=(nta,),
        in_specs=[row(dm),
                  pl.BlockSpec((tm * CHUNK, LANE), lambda i: (i, 0)),
                  pl.BlockSpec((tm * CHUNK, LANE), lambda i: (nta + i, 0)),
                  row(LANE), _const_spec(ln_g.shape), _const_spec(ln_b.shape)],
        out_specs=[pl.BlockSpec((tm, dm), lambda i: (jnp.minimum(i, npt - 1), 0)),
                   pl.BlockSpec((tm, dm), lambda i: (0, 0))],
        out_shape=[jax.ShapeDtypeStruct((t, dm), F32), jax.ShapeDtypeStruct((tm, dm), F32)],
        compiler_params=_cparams(("arbitrary",)),
        name="final",
    )(r, y01_flat, y01_flat, wts, ln_g, ln_b)


def _layer(layer, x_p, x_s, pe_p, pe_s, cache_k, cache_v, state, page_table, w):
    t, dm = x_p.shape
    nb = x_s.shape[0]
    wn = N_HEADS * HEAD_W
    ta = t + ROW_PAD
    assert nb <= ROW_PAD and t % ROW_PAD == 0
    lam0 = _lambda_init(layer)
    depth_total = w["depth"]
    alpha = (2.0 * depth_total) ** 0.25
    page = cache_k.shape[1]
    start_pos = page_table.shape[1] * page

    pad = ta - t - nb
    xb = jnp.concatenate([x_p.astype(BF16), x_s.astype(BF16), jnp.zeros((pad, dm), BF16)], axis=0)
    u, qb, k, kb, v, vb, gates = _inproj(xb, w["w_in"].astype(BF16))

    lamv = jnp.concatenate([w["lam_q1"], w["lam_k1"], w["lam_q2"], w["lam_k2"]], axis=0)
    subln_g = w["subln_g"]

    o_p = _prompt_attn(qb, kb, vb, lamv, subln_g, t, lam0)
    kt_pool = jnp.transpose(cache_k, (0, 2, 3, 4, 1)).reshape(cache_k.shape[0], wn, page)
    v_pool = cache_v.reshape(cache_v.shape[0] * page * N_HEADS, HEAD_W)
    o_s = _decode_attn(page_table, lamv, subln_g, qb[t:t + nb], k[t:t + nb], v[t:t + nb],
                       kt_pool, v_pool, lam0)
    o_all = jnp.concatenate([o_p, o_s.astype(BF16), jnp.zeros((pad, wn), BF16)], axis=0)

    d_p = _pool_prompt(u, t)
    d_s = _pool_sample(u[t:t + nb], state, start_pos)
    d_all = jnp.concatenate([d_p, d_s, jnp.zeros((pad, d_p.shape[1]), BF16)], axis=0)

    x1, x1_flat, logits = _mix(d_all, o_all, gates, x_p, x_s, w["pool_w"].astype(BF16), w["pool_scale"],
                               w["w_branch_pool"].astype(BF16), w["w_branch_attn"].astype(BF16),
                               w["w_out"].astype(BF16), w["ln1_g"], w["ln1_b"], w["router_w"], w["router_b"], alpha)

    pe_b = jnp.concatenate([pe_p.astype(BF16), pe_s.astype(BF16), jnp.zeros((pad, pe_p.shape[1]), BF16)], axis=0)
    r = _ple(x1, pe_b, w["ple_gate_w"].astype(BF16), w["ple_gate_b"], w["ple_proj"].astype(BF16), alpha)

    eid, wts = _route(logits)
    pos, meta = _slots(eid)
    n_pairs = TOP_K * ta
    n_rows = n_pairs + N_EXPERTS * TR
    n_row_tiles = n_rows // TR
    assert n_row_tiles <= LANE
    te_raw = meta[:n_row_tiles, 0]
    tile_expert = jnp.minimum(te_raw, N_EXPERTS - 1)
    n_tiles = jnp.sum((te_raw < N_EXPERTS).astype(I32)).reshape(1)

    tile_valid = meta[:n_row_tiles, 1]
    src, dst = _invert(pos[:, 0], pos[:, 1], n_rows)
    y01_flat = _experts(tile_expert, tile_valid, n_tiles, src, dst, x1_flat, w["expert_gate"], w["expert_up"],
                        w["expert_down"], TOP_K * ta)
    y_p, y_s = _final(r, y01_flat, wts, w["ln2_g"], w["ln2_b"], t)
    return y_p, y_s[:nb], k, v, u


def kernel(x_prompt, x_sample, p_prompt, p_sample, cache_k, cache_v, state_pool, page_table, w_in, pool_w, pool_scale, lam_q1, lam_k1, lam_q2, lam_k2, subln_g, w_branch_pool, w_branch_attn, w_out, ln1_g, ln1_b, router_group_w, router_group_b, router_expert_w, router_expert_b, expert_gate, expert_up, expert_down, ple_proj, ple_gate_w, ple_gate_b, ln2_g, ln2_b):
    depth = w_in.shape[0]
    bp, t, dm = x_prompt.shape
    nb, ts, _ = x_sample.shape
    assert bp == 1 and ts == 1 and dm // 256 == N_HEADS
    assert dm % CHUNK == 0 and dm // CHUNK == LANE
    hp = x_prompt.reshape(t, dm)
    hs = x_sample.reshape(nb, dm)
    outs = {n: [] for n in ("kp", "vp", "sp", "ks", "vs", "ss")}
    for l in range(depth):
        wr = jnp.concatenate([router_group_w[l],
                              jnp.transpose(router_expert_w[l], (1, 0, 2)).reshape(dm, N_EXPERTS)], axis=1)
        wr = jnp.pad(wr, ((0, 0), (0, LANE - wr.shape[1])))
        wr_hi = wr.astype(BF16)
        wr = jnp.stack([wr_hi, (wr - wr_hi.astype(F32)).astype(BF16)])
        br = jnp.pad(jnp.concatenate([router_group_b[l], router_expert_b[l].reshape(-1)]),
                     (0, LANE - N_GROUPS - N_EXPERTS)).reshape(1, LANE)
        row = lambda a: a[l].reshape(1, -1)
        w = dict(depth=depth, w_in=w_in[l], pool_w=pool_w[l], pool_scale=row(pool_scale),
                 lam_q1=row(lam_q1), lam_k1=row(lam_k1), lam_q2=row(lam_q2), lam_k2=row(lam_k2),
                 subln_g=row(subln_g), w_branch_pool=w_branch_pool[l], w_branch_attn=w_branch_attn[l],
                 w_out=w_out[l], ln1_g=row(ln1_g), ln1_b=row(ln1_b), router_w=wr, router_b=br,
                 expert_gate=expert_gate[l], expert_up=expert_up[l], expert_down=expert_down[l],
                 ple_proj=ple_proj[l], ple_gate_w=ple_gate_w[l], ple_gate_b=row(ple_gate_b),
                 ln2_g=row(ln2_g), ln2_b=row(ln2_b))
        state = state_pool[l]
        hp, hs, k, v, u = _layer(l, hp, hs, p_prompt[l, 0], p_sample[l, :, 0], cache_k[l], cache_v[l],
                                 state, page_table, w)
        outs["kp"].append(k[:t].reshape(1, t, N_HEADS, 2, HEAD_DIM))
        outs["vp"].append(v[:t].reshape(1, t, N_HEADS, HEAD_W))
        outs["sp"].append(u[t - POOL_STATE:t].reshape(1, POOL_STATE, -1))
        outs["ks"].append(k[t:t + nb].reshape(nb, 1, N_HEADS, 2, HEAD_DIM))
        outs["vs"].append(v[t:t + nb].reshape(nb, 1, N_HEADS, HEAD_W))
        outs["ss"].append(jnp.concatenate([state[:, 1:], u[t:t + nb][:, None, :]], axis=1))
    st = lambda n: jnp.stack(outs[n])
    return (hp.reshape(1, t, dm), hs.reshape(nb, 1, dm), st("kp"), st("vp"), st("sp"),
            st("ks"), st("vs"), st("ss"))
```

```python
import functools
import math

import jax
import jax.numpy as jnp
from jax import lax
from jax.experimental import pallas as pl
from jax.experimental.pallas import tpu as pltpu

F32 = jnp.float32
BF16 = jnp.bfloat16
I32 = jnp.int32

N_HEADS = 8
HEAD_DIM = 64
HEAD_W = 2 * HEAD_DIM
ATTN_SCALE = HEAD_DIM ** -0.5
LOG2E = math.log2(math.e)
SUBLN_EPS = 1e-5
LN_EPS = 1e-5
POOL_WINDOWS = (2, 4, 8, 16)
POOL_STATE = max(POOL_WINDOWS) - 1
N_GROUPS = 4
EXPERTS_PER_GROUP = 8
N_EXPERTS = N_GROUPS * EXPERTS_PER_GROUP
TOP_K = 2

LANE = 128
SUBLANE = 8
VMEM_LIMIT = 56 * 1024 * 1024

ROW_PAD = 256
TM_INPROJ = 768
TM_ROW = 256
TQ = 512
TK = 512
HEADS_PER_STEP = 1
TR = 256
HALO = 16
CHUNK = 16
DECODE_GROUP = 4
DECODE_SLOTS = 12


def _lambda_init(layer):
    return 0.8 - 0.6 * math.exp(-0.3 * layer)


def _cparams(sem, vmem=VMEM_LIMIT):
    return pltpu.CompilerParams(dimension_semantics=sem, vmem_limit_bytes=vmem)


def _const_spec(shape):
    nd = len(shape)
    return pl.BlockSpec(shape, lambda *_: (0,) * nd, pipeline_mode=pl.Buffered(1))


def _inproj_body(x_ref, w_ref, u_ref, q_ref, k_ref, kb_ref, v_ref, vb_ref, g_ref):
    j = pl.program_id(0)

    def z():
        return jnp.dot(x_ref[...], w_ref[...], preferred_element_type=F32)

    @pl.when(j == 0)
    def _():
        u_ref[...] = z()

    @pl.when(j == 1)
    def _():
        q_ref[...] = (z() * (ATTN_SCALE * LOG2E)).astype(BF16)

    @pl.when(j == 2)
    def _():
        kz = z()
        k_ref[...] = kz
        kb_ref[...] = kz.astype(BF16)

    @pl.when(j == 3)
    def _():
        vz = z()
        v_ref[...] = vz
        vb_ref[...] = vz.astype(BF16)

    @pl.when(j >= 4)
    def _():
        g_ref[...] = jax.nn.sigmoid(z()).astype(BF16)


def _inproj(xb, w_in_b):
    ta, d = xb.shape
    n_in = w_in_b.shape[1]
    wn = N_HEADS * HEAD_W
    assert n_in == 4 * wn + 2 * d and (2 * d) % wn == 0
    tm = TM_INPROJ
    assert ta % tm == 0
    ni = ta // tm
    nj = n_in // wn

    def once(jo):
        return lambda j, i: (jnp.where(j < jo, 0, jnp.where(j == jo, i, ni - 1)), 0)

    blk = lambda jo: pl.BlockSpec((tm, wn), once(jo))
    g_spec = pl.BlockSpec((tm, wn), lambda j, i: (jnp.where(j < 4, 0, i), jnp.maximum(j - 4, 0)))
    return pl.pallas_call(
        _inproj_body,
        grid=(nj, ni),
        in_specs=[pl.BlockSpec((tm, d), lambda j, i: (i, 0)),
                  pl.BlockSpec((d, wn), lambda j, i: (0, j))],
        out_specs=[blk(0), blk(1), blk(2), blk(2), blk(3), blk(3), g_spec],
        out_shape=[jax.ShapeDtypeStruct((ta, wn), F32),
                   jax.ShapeDtypeStruct((ta, wn), BF16),
                   jax.ShapeDtypeStruct((ta, wn), F32),
                   jax.ShapeDtypeStruct((ta, wn), BF16),
                   jax.ShapeDtypeStruct((ta, wn), F32),
                   jax.ShapeDtypeStruct((ta, wn), BF16),
                   jax.ShapeDtypeStruct((ta, 2 * d), BF16)],
        compiler_params=_cparams(("arbitrary", "arbitrary")),
        name="inproj",
    )(xb, w_in_b)


def _lam_value(lamv_ref, lam0):
    lv = lamv_ref[...]
    a = jnp.sum(lv[0:1] * lv[1:2], axis=1, keepdims=True)
    b = jnp.sum(lv[2:3] * lv[3:4], axis=1, keepdims=True)
    return jnp.exp(a) - jnp.exp(b) + lam0


def _subln(o, g, lam0):
    ms = jnp.mean(o * o, axis=-1, keepdims=True)
    return o * lax.rsqrt(ms + SUBLN_EPS) * g * (1.0 - lam0)


def _prompt_attn_body(lamv_ref, gcol_ref, q_ref, k_ref, v_ref, o_ref, vt_ref, q2t_ref, s_even, s_odd,
                      m_ref, l_ref, acc_ref, *, tq, tk, hp, lam0):
    i = pl.program_id(1)
    nkv = v_ref.shape[0] // tk
    heads = [pl.ds(hh * HEAD_W, HEAD_W) for hh in range(hp)]

    @pl.when(i == 0)
    def _():
        def tr(jb, _):
            for hh in range(hp):
                blk = v_ref[pl.ds(pl.multiple_of(jb * tk, tk), tk), heads[hh]].astype(F32)
                vt_ref[hh, jb] = blk.T.astype(BF16)
            return 0
        lax.fori_loop(0, nkv, tr, 0)

    drow = lax.broadcasted_iota(I32, (HEAD_W, tq), 0)
    for hh in range(hp):
        qt = q_ref[:, heads[hh]].astype(F32).T
        q2t_ref[hh] = jnp.concatenate([jnp.where(drow < HEAD_DIM, qt, 0.0),
                                       jnp.where(drow >= HEAD_DIM, qt, 0.0)], axis=1).astype(BF16)
    m_ref[...] = jnp.full(m_ref.shape, -jnp.inf, F32)
    l_ref[...] = jnp.zeros(l_ref.shape, F32)
    acc_ref[...] = jnp.zeros(acc_ref.shape, F32)
    n_full = (i * tq) // tk

    def scores(j, s_ref):
        rows = pl.ds(pl.multiple_of(j * tk, tk), tk)
        for hh in range(hp):
            s_ref[hh] = jnp.dot(k_ref[rows, heads[hh]], q2t_ref[hh], preferred_element_type=F32)

    def absorb(j, s_ref, masked):
        for hh in range(hp):
            s = s_ref[hh]
            if masked:
                kpos = n_full * tk + lax.broadcasted_iota(I32, (tk, 2 * tq), 0)
                qcol = lax.broadcasted_iota(I32, (tk, 2 * tq), 1)
                qpos = i * tq + jnp.where(qcol >= tq, qcol - tq, qcol)
                s = jnp.where(kpos <= qpos, s, -jnp.inf)
            m = m_ref[hh]
            m_new = jnp.maximum(m, jnp.max(s, axis=0, keepdims=True))
            alpha = jnp.exp2(m - m_new)
            p = jnp.exp2(s - m_new)
            l_ref[hh] = alpha * l_ref[hh] + jnp.sum(p, axis=0, keepdims=True)
            acc_ref[hh] = alpha * acc_ref[hh] + jnp.dot(vt_ref[hh, j], p.astype(BF16), preferred_element_type=F32)
            m_ref[hh] = m_new

    def by_parity(j, fn):
        @pl.when(j % 2 == 0)
        def _():
            fn(s_even, s_odd)

        @pl.when(j % 2 == 1)
        def _():
            fn(s_odd, s_even)

    scores(0, s_even)

    def body(j, _):
        def stage(cur, nxt):
            scores(j + 1, nxt)
            absorb(j, cur, False)
        by_parity(j, stage)
        return 0

    lax.fori_loop(0, n_full, body, 0)
    by_parity(n_full, lambda cur, nxt: absorb(n_full, cur, True))

    lam = _lam_value(lamv_ref, lam0)
    for hh in range(hp):
        on = acc_ref[hh] / l_ref[hh]
        ot = on[:, :tq] - lam * on[:, tq:]
        ms = jnp.mean(ot * ot, axis=0, keepdims=True)
        ot = ot * lax.rsqrt(ms + SUBLN_EPS) * gcol_ref[...] * (1.0 - lam0)
        o_ref[:, heads[hh]] = ot.T.astype(o_ref.dtype)


def _prompt_attn(qb, kb, vb, lamv, subln_g, t, lam0):
    tq, tk, hp = TQ, TK, HEADS_PER_STEP
    assert t % tk == 0 and tk % tq == 0 and N_HEADS % hp == 0
    kern = functools.partial(_prompt_attn_body, tq=tq, tk=tk, hp=hp, lam0=lam0)
    gcol = subln_g.reshape(HEAD_W, 1)
    return pl.pallas_call(
        kern,
        grid=(N_HEADS // hp, t // tq),
        in_specs=[_const_spec(lamv.shape), _const_spec(gcol.shape),
                  pl.BlockSpec((tq, hp * HEAD_W), lambda h, i: (i, h)),
                  pl.BlockSpec((t, hp * HEAD_W), lambda h, i: (0, h)),
                  pl.BlockSpec((t, hp * HEAD_W), lambda h, i: (0, h))],
        out_specs=pl.BlockSpec((tq, hp * HEAD_W), lambda h, i: (i, h)),
        out_shape=jax.ShapeDtypeStruct((t, N_HEADS * HEAD_W), BF16),
        scratch_shapes=[pltpu.VMEM((hp, t // tk, HEAD_W, tk), BF16),
                        pltpu.VMEM((hp, HEAD_W, 2 * tq), BF16),
                        pltpu.VMEM((hp, tk, 2 * tq), F32), pltpu.VMEM((hp, tk, 2 * tq), F32),
                        pltpu.VMEM((hp, 1, 2 * tq), F32), pltpu.VMEM((hp, 1, 2 * tq), F32),
                        pltpu.VMEM((hp, HEAD_W, 2 * tq), F32)],
        compiler_params=_cparams(("arbitrary", "arbitrary")),
        name="prompt_attn",
    )(lamv, gcol, qb, kb, vb)


def _decode_attn_body(pt_ref, lamv_ref, g_ref, q_ref, qc_ref, kn_ref, vn_ref, kt_hbm, v_hbm, o_ref,
                      kbuf, vbuf, sem, *, n_pages, page, n_slots, group, lam0):
    b = pl.program_id(0)
    nb = pl.num_programs(0)
    nr = 2 * N_HEADS
    wn = N_HEADS * HEAD_W
    vrows = page * N_HEADS
    ahead = n_slots - group
    total = nb * n_pages
    assert n_pages % group == 0 and n_slots % group == 0 and ahead >= group

    def copies(g):
        slot = g % n_slots
        pg = pt_ref[g // n_pages, g % n_pages]
        ck = pltpu.make_async_copy(kt_hbm.at[pg], kbuf.at[slot], sem.at[0, slot])
        cv = pltpu.make_async_copy(v_hbm.at[pl.ds(pl.multiple_of(pg * vrows, vrows), vrows)],
                                   vbuf.at[slot], sem.at[1, slot])
        return ck, cv

    def start(g):
        for c in copies(g):
            c.start()

    @pl.when(b == 0)
    def _():
        for g in range(ahead):
            start(g)

    rr = lax.broadcasted_iota(I32, (nr, wn), 0)
    cc = lax.broadcasted_iota(I32, (nr, wn), 1)
    qrow = q_ref[0]
    qbd = jnp.where(cc // HEAD_DIM == rr, jnp.broadcast_to(qrow.astype(F32), (nr, wn)), 0.0).astype(BF16)
    rhead = lax.broadcasted_iota(I32, (nr, HEAD_W), 0) // 2
    qcol = jnp.broadcast_to(qc_ref[0], (wn, page))
    tok = lax.broadcasted_iota(I32, (page, vrows), 0)
    vrow = lax.broadcasted_iota(I32, (page, vrows), 1)
    spread = jnp.where(vrow // N_HEADS == tok, 1.0, 0.0).astype(BF16)
    own_head = lax.broadcasted_iota(I32, (nr, vrows), 1) % N_HEADS == lax.broadcasted_iota(I32, (nr, vrows), 0) // 2

    def body(jg, carry):
        m, l, acc = carry
        g0 = b * n_pages + jg * group
        slots = []
        ss = []
        for a in range(group):
            g = g0 + a

            @pl.when(g + ahead < total)
            def _():
                start(g + ahead)

            for c in copies(g):
                c.wait()
            slot = g % n_slots
            slots.append(slot)
            prod = kbuf[slot] * qcol
            ss.append(jnp.sum(prod.reshape(nr, HEAD_DIM, page), axis=1))
        m_new = m
        for s in ss:
            m_new = jnp.maximum(m_new, jnp.max(s, axis=1, keepdims=True))
        alpha = jnp.exp2(m - m_new)
        l = alpha * l
        acc = alpha * acc
        for s, slot in zip(ss, slots):
            p = jnp.exp2(s - m_new)
            l = l + jnp.sum(p, axis=1, keepdims=True)
            pe = jnp.dot(p.astype(BF16), spread, preferred_element_type=F32)
            pe = jnp.where(own_head, pe, 0.0).astype(BF16)
            acc = acc + jnp.dot(pe, vbuf[slot].astype(BF16), preferred_element_type=F32)
        return m_new, l, acc

    init = (jnp.full((nr, 1), -jnp.inf, F32), jnp.zeros((nr, 1), F32), jnp.zeros((nr, HEAD_W), F32))
    m, l, acc = lax.fori_loop(0, n_pages // group, body, init)

    kn = jnp.broadcast_to(kn_ref[0], (nr, wn))
    s_new = jnp.sum(qbd.astype(F32) * kn, axis=1, keepdims=True)
    vn = vn_ref[0]
    vnew = jnp.zeros((nr, HEAD_W), F32)
    for h in range(N_HEADS):
        vnew = vnew + jnp.where(rhead == h, jnp.broadcast_to(vn[:, h * HEAD_W:(h + 1) * HEAD_W], (nr, HEAD_W)), 0.0)
    m_new = jnp.maximum(m, s_new)
    alpha = jnp.exp2(m - m_new)
    p_new = jnp.exp2(s_new - m_new)
    l = alpha * l + p_new
    acc = alpha * acc + p_new * vnew
    on = acc / l
    lam = _lam_value(lamv_ref, lam0)
    o = on - lam * pltpu.roll(on, nr - 1, 0)
    o_ref[0] = _subln(o, g_ref[...], lam0)


def _decode_attn(page_table, lamv, subln_g, q_s, k_s, v_s, kt_pool, v_pool, lam0):
    nb, n_pages = page_table.shape
    wn = N_HEADS * HEAD_W
    page = kt_pool.shape[2]
    nr = 2 * N_HEADS
    n_slots = DECODE_SLOTS
    kern = functools.partial(_decode_attn_body, n_pages=n_pages, page=page, n_slots=n_slots,
                             group=DECODE_GROUP, lam0=lam0)
    grid_spec = pltpu.PrefetchScalarGridSpec(
        num_scalar_prefetch=1,
        grid=(nb,),
        in_specs=[pl.BlockSpec(lamv.shape, lambda b, pt: (0, 0)),
                  pl.BlockSpec(subln_g.shape, lambda b, pt: (0, 0)),
                  pl.BlockSpec((1, 1, wn), lambda b, pt: (b, 0, 0)),
                  pl.BlockSpec((1, wn, 1), lambda b, pt: (b, 0, 0)),
                  pl.BlockSpec((1, 1, wn), lambda b, pt: (b, 0, 0)),
                  pl.BlockSpec((1, 1, wn), lambda b, pt: (b, 0, 0)),
                  pl.BlockSpec(memory_space=pl.ANY),
                  pl.BlockSpec(memory_space=pl.ANY)],
        out_specs=pl.BlockSpec((1, nr, HEAD_W), lambda b, pt: (b, 0, 0)),
        scratch_shapes=[pltpu.VMEM((n_slots, wn, page), F32),
                        pltpu.VMEM((n_slots, page * N_HEADS, HEAD_W), F32),
                        pltpu.SemaphoreType.DMA((2, n_slots))],
    )
    out = pl.pallas_call(
        kern,
        grid_spec=grid_spec,
        out_shape=jax.ShapeDtypeStruct((nb, nr, HEAD_W), F32),
        compiler_params=_cparams(("arbitrary",)),
        name="decode_attn",
    )(page_table, lamv, subln_g, q_s.reshape(nb, 1, wn), q_s.astype(F32).reshape(nb, wn, 1),
      k_s.reshape(nb, 1, wn), v_s.reshape(nb, 1, wn), kt_pool, v_pool)
    return out[:, 0::2, :].reshape(nb, wn)


def _pool_prompt_body(u_ref, halo_ref, d_ref, ext_ref, *, tm, gw):
    i = pl.program_id(0)
    ext_ref[pl.ds(HALO, tm), :] = u_ref[...]

    @pl.when(i == 0)
    def _():
        ext_ref[pl.ds(0, HALO), :] = jnp.zeros((HALO, u_ref.shape[1]), F32)

    @pl.when(i > 0)
    def _():
        ext_ref[pl.ds(0, HALO), :] = halo_ref[...]

    pos = i * tm + lax.broadcasted_iota(I32, (tm, 1), 0)
    for g, w in enumerate(POOL_WINDOWS):
        cols = pl.ds(g * gw, gw)
        tok = ext_ref[pl.ds(HALO, tm), cols]
        s = tok
        for back in range(1, w):
            s = s + ext_ref[pl.ds(HALO - back, tm), cols]
        inv = 1.0 / jnp.minimum(pos + 1, w).astype(F32)
        d_ref[:, cols] = (s * inv - tok).astype(d_ref.dtype)


def _pool_prompt(u_all, t):
    wp = u_all.shape[1]
    gw = wp // len(POOL_WINDOWS)
    tm = TM_ROW
    assert t % tm == 0 and tm % HALO == 0 and gw % LANE == 0
    kern = functools.partial(_pool_prompt_body, tm=tm, gw=gw)
    return pl.pallas_call(
        kern,
        grid=(t // tm,),
        in_specs=[pl.BlockSpec((tm, wp), lambda i: (i, 0)),
                  pl.BlockSpec((HALO, wp), lambda i: (jnp.maximum(i * (tm // HALO) - 1, 0), 0))],
        out_specs=pl.BlockSpec((tm, wp), lambda i: (i, 0)),
        out_shape=jax.ShapeDtypeStruct((t, wp), BF16),
        scratch_shapes=[pltpu.VMEM((tm + HALO, wp), F32)],
        compiler_params=_cparams(("arbitrary",)),
        name="pool_prompt",
    )(u_all, u_all)


def _pool_sample_body(u_ref, st_ref, d_ref, *, gw, start_pos):
    st = st_ref[...]
    u = u_ref[...]
    ridx = lax.broadcasted_iota(I32, st.shape, 1)
    for g, w in enumerate(POOL_WINDOWS):
        hist = jnp.sum(jnp.where(ridx >= POOL_STATE - (w - 1), st, 0.0), axis=1)
        cnt = float(min(start_pos + 1, w))
        d = (hist + u) * (1.0 / cnt) - u
        d_ref[:, g * gw:(g + 1) * gw] = d[:, g * gw:(g + 1) * gw].astype(d_ref.dtype)


def _pool_sample(u_s, state, start_pos):
    nb, wp = u_s.shape
    gw = wp // len(POOL_WINDOWS)
    kern = functools.partial(_pool_sample_body, gw=gw, start_pos=start_pos)
    return pl.pallas_call(
        kern,
        grid=(1,),
        in_specs=[_const_spec(u_s.shape), _const_spec(state.shape)],
        out_specs=_const_spec((nb, wp)),
        out_shape=jax.ShapeDtypeStruct((nb, wp), BF16),
        compiler_params=_cparams(("arbitrary",)),
        name="pool_sample",
    )(u_s, state)


def _store_flat(flat_ref, x, lead=()):
    rows = x.shape[0]
    for c in range(CHUNK):
        flat_ref[(*lead, pl.ds(c, rows, stride=CHUNK), slice(None))] = x[:, c * LANE:(c + 1) * LANE]


def _load_flat(flat_ref, rows, lead=()):
    return jnp.concatenate(
        [flat_ref[(*lead, pl.ds(c, rows, stride=CHUNK), slice(None))] for c in range(CHUNK)], axis=1)


def _layer_norm(x, g, b):
    mu = jnp.mean(x, axis=-1, keepdims=True)
    xc = x - mu
    var = jnp.mean(xc * xc, axis=-1, keepdims=True)
    return xc * lax.rsqrt(var + LN_EPS) * g + b


def _mix_body(d_ref, o_ref, g_ref, xp_ref, xs_ref, pw_ref, ps_ref, wbp_ref, wba_ref, wo_ref,
              lg_ref, lb_ref, wr_ref, br_ref, x1_ref, x1f_ref, lo_ref, *, n_prompt_tiles, alpha):
    i = pl.program_id(0)
    tm, dm = x1_ref.shape
    ng = len(POOL_WINDOWS)
    gw = d_ref.shape[1] // ng
    ys = []
    for g in range(ng):
        ys.append(jnp.dot(d_ref[:, g * gw:(g + 1) * gw], pw_ref[g], preferred_element_type=F32))
    pool_y = (jnp.concatenate(ys, axis=1) * ps_ref[...]).astype(BF16)
    a = jnp.dot(pool_y, wbp_ref[...], preferred_element_type=F32)
    bb = jnp.dot(o_ref[...], wba_ref[...], preferred_element_type=F32)
    gates = g_ref[...].astype(F32)
    merged = (gates[:, :dm] * a + gates[:, dm:] * bb).astype(BF16)
    mix = jnp.dot(merged, wo_ref[...], preferred_element_type=F32)

    def finish(x):
        x1 = _layer_norm(alpha * x + mix, lg_ref[...], lb_ref[...])
        x1_ref[...] = x1
        _store_flat(x1f_ref, x1)
        hi = x1.astype(BF16)
        lo = (x1 - hi.astype(F32)).astype(BF16)
        dot = functools.partial(jnp.dot, preferred_element_type=F32)
        lo_ref[...] = dot(hi, wr_ref[0]) + (dot(lo, wr_ref[0]) + dot(hi, wr_ref[1])) + br_ref[...]

    @pl.when(i < n_prompt_tiles)
    def _():
        finish(xp_ref[...])

    @pl.when(i >= n_prompt_tiles)
    def _():
        ns = xs_ref.shape[0]
        finish(jnp.concatenate([xs_ref[...], jnp.zeros((tm - ns, dm), F32)], axis=0))


def _mix(d_all, o_all, gates, x_p, x_s, pool_w_b, pool_scale, wbp, wba, wo, ln_g, ln_b, wr, br, alpha):
    ta = d_all.shape[0]
    t, dm = x_p.shape
    ns = x_s.shape[0]
    tm = TM_ROW
    assert ta % tm == 0 and t % tm == 0 and ns % SUBLANE == 0 and ns <= tm and ta == t + tm
    npt = t // tm
    wp = d_all.shape[1]
    wa = o_all.shape[1]
    row = lambda w: pl.BlockSpec((tm, w), lambda i: (i, 0))
    kern = functools.partial(_mix_body, n_prompt_tiles=npt, alpha=alpha)
    return pl.pallas_call(
        kern,
        grid=(ta // tm,),
        in_specs=[row(wp), row(wa), row(2 * dm),
                  pl.BlockSpec((tm, dm), lambda i: (jnp.minimum(i, npt - 1), 0)),
                  _const_spec(x_s.shape),
                  _const_spec(pool_w_b.shape), _const_spec(pool_scale.shape),
                  _const_spec(wbp.shape), _const_spec(wba.shape), _const_spec(wo.shape),
                  _const_spec(ln_g.shape), _const_spec(ln_b.shape),
                  _const_spec(wr.shape), _const_spec(br.shape)],
        out_specs=[row(dm), pl.BlockSpec((tm * CHUNK, LANE), lambda i: (i, 0)), row(LANE)],
        out_shape=[jax.ShapeDtypeStruct((ta, dm), F32),
                   jax.ShapeDtypeStruct((ta * CHUNK, LANE), F32),
                   jax.ShapeDtypeStruct((ta, LANE), F32)],
        compiler_params=_cparams(("arbitrary",)),
        name="branch_mix",
    )(d_all, o_all, gates, x_p, x_s, pool_w_b, pool_scale, wbp, wba, wo, ln_g, ln_b, wr, br)


def _ple_body(x1_ref, pe_ref, gw_ref, gb_ref, pp_ref, r_ref, *, alpha):
    x1 = x1_ref[...]
    gate = jax.nn.sigmoid(jnp.dot(x1.astype(BF16), gw_ref[...], preferred_element_type=F32) + gb_ref[...])
    emb = jnp.dot(pe_ref[...], pp_ref[...], preferred_element_type=F32)
    r_ref[...] = alpha * x1 + gate * emb


def _ple(x1, pe_b, gw_b, gb, pp_b, alpha):
    ta, dm = x1.shape
    tm = TM_ROW
    row = lambda w: pl.BlockSpec((tm, w), lambda i: (i, 0))
    return pl.pallas_call(
        functools.partial(_ple_body, alpha=alpha),
        grid=(ta // tm,),
        in_specs=[row(dm), row(pe_b.shape[1]), _const_spec(gw_b.shape),
                  _const_spec(gb.shape), _const_spec(pp_b.shape)],
        out_specs=row(dm),
        out_shape=jax.ShapeDtypeStruct((ta, dm), F32),
        compiler_params=_cparams(("arbitrary",)),
        name="ple",
    )(x1, pe_b, gw_b, gb, pp_b)


def _route_body(lo_ref, eid_ref, wt_ref):
    lo = lo_ref[...]
    lane_i = lax.broadcasted_iota(I32, lo.shape, 1)
    lane = lane_i.astype(F32)
    ninf = -jnp.inf
    big = float(LANE)
    is_g = lane_i < N_GROUPS
    gl = jnp.where(is_g, lo, ninf)
    gmax = jnp.max(gl, axis=1, keepdims=True)
    gidx = jnp.min(jnp.where(gl == gmax, lane, big), axis=1, keepdims=True)
    gsum = jnp.sum(jnp.where(is_g, jnp.exp(gl - gmax), 0.0), axis=1, keepdims=True)
    gw = 1.0 / gsum
    lo_e = N_GROUPS + EXPERTS_PER_GROUP * gidx
    el = jnp.where(jnp.logical_and(lane >= lo_e, lane < lo_e + EXPERTS_PER_GROUP), lo, ninf)
    v1 = jnp.max(el, axis=1, keepdims=True)
    i1 = jnp.min(jnp.where(el == v1, lane, big), axis=1, keepdims=True)
    el2 = jnp.where(lane == i1, ninf, el)
    v2 = jnp.max(el2, axis=1, keepdims=True)
    i2 = jnp.min(jnp.where(el2 == v2, lane, big), axis=1, keepdims=True)
    e21 = jnp.exp(v2 - v1)
    w1 = gw / (1.0 + e21)
    w2 = gw * e21 / (1.0 + e21)
    eid = jnp.where(lane_i == 0, i1 - N_GROUPS, jnp.where(lane_i == 1, i2 - N_GROUPS, 0.0))
    eid_ref[...] = eid.astype(I32)
    wt_ref[...] = jnp.where(lane_i == 0, w1, jnp.where(lane_i == 1, w2, 0.0))


def _route(logits):
    ta = logits.shape[0]
    tm = TM_ROW
    row = pl.BlockSpec((tm, LANE), lambda i: (i, 0))
    return pl.pallas_call(
        _route_body,
        grid=(ta // tm,),
        in_specs=[row],
        out_specs=[row, row],
        out_shape=[jax.ShapeDtypeStruct((ta, LANE), I32), jax.ShapeDtypeStruct((ta, LANE), F32)],
        compiler_params=_cparams(("arbitrary",)),
        name="route",
    )(logits)


def _slots_body(eid_ref, pos_ref, meta_ref, emeta_ref, cnt_ref, run_ref, off_ref, *, tm, tr):
    ph = pl.program_id(0)
    i = pl.program_id(1)
    eid = eid_ref[...]
    lane = lax.broadcasted_iota(I32, (tm, LANE), 1)
    e0 = eid[:, 0:1]
    e1 = eid[:, 1:2]
    oh0 = (lane == e0).astype(F32)
    oh1 = (lane == e1).astype(F32)
    both = oh0 + oh1

    @pl.when(jnp.logical_and(ph == 0, i == 0))
    def _():
        cnt_ref[...] = jnp.zeros_like(cnt_ref)

    @pl.when(ph == 0)
    def _():
        cnt_ref[...] = cnt_ref[...] + jnp.sum(both, axis=0, keepdims=True)

    @pl.when(jnp.logical_and(ph == 1, i == 0))
    def _():
        cnt = cnt_ref[...]
        tiles = jnp.floor((cnt + (tr - 1)) * (1.0 / tr))
        a = lax.broadcasted_iota(I32, (LANE, LANE), 0)
        b = lax.broadcasted_iota(I32, (LANE, LANE), 1)
        upper = (a < b).astype(BF16)
        t8 = jnp.broadcast_to(tiles, (SUBLANE, LANE)).astype(BF16)
        off_tiles = jnp.dot(t8, upper, preferred_element_type=F32)[0:1]
        off_ref[...] = off_tiles * tr
        run_ref[...] = jnp.zeros_like(run_ref)
        end_tiles = off_tiles + tiles
        lane1 = lax.broadcasted_iota(I32, (LANE, LANE), 1)
        jrow = lax.broadcasted_iota(I32, (LANE, LANE), 0).astype(F32)
        done = jnp.logical_and(lane1 < N_EXPERTS, jnp.broadcast_to(end_tiles, (LANE, LANE)) <= jrow)
        te = jnp.sum(done.astype(F32), axis=1, keepdims=True)
        owns = jnp.logical_and(jnp.broadcast_to(off_tiles, (LANE, LANE)) <= jrow,
                               jrow < jnp.broadcast_to(end_tiles, (LANE, LANE)))
        left = jnp.broadcast_to(cnt, (LANE, LANE)) - (jrow - jnp.broadcast_to(off_tiles, (LANE, LANE))) * tr
        nv = jnp.sum(jnp.where(owns, jnp.minimum(left, float(tr)), 0.0), axis=1, keepdims=True)
        meta_ref[...] = jnp.where(lane1 == 0, jnp.broadcast_to(te, (LANE, LANE)),
                                  jnp.broadcast_to(nv, (LANE, LANE))).astype(I32)
        erow = lax.broadcasted_iota(I32, (SUBLANE, LANE), 0)
        emeta_ref[...] = jnp.where(erow == 0, jnp.broadcast_to(off_tiles, (SUBLANE, LANE)),
                                   jnp.broadcast_to(tiles, (SUBLANE, LANE))).astype(I32)

    @pl.when(ph == 1)
    def _():
        r = lax.broadcasted_iota(I32, (tm, tm), 0)
        c = lax.broadcasted_iota(I32, (tm, tm), 1)
        lower = (c < r).astype(BF16)
        before = jnp.dot(lower, both.astype(BF16), preferred_element_type=F32) + run_ref[...] + off_ref[...]
        p0 = jnp.sum(oh0 * before, axis=1, keepdims=True)
        p1 = jnp.sum(oh1 * before, axis=1, keepdims=True)
        pos_ref[...] = jnp.where(lane == 0, p0, jnp.where(lane == 1, p1, 0.0)).astype(I32)
        run_ref[...] = run_ref[...] + jnp.sum(both, axis=0, keepdims=True)


def _slots(eid):
    ta = eid.shape[0]
    tm = TM_ROW
    kern = functools.partial(_slots_body, tm=tm, tr=TR)
    return pl.pallas_call(
        kern,
        grid=(2, ta // tm),
        in_specs=[pl.BlockSpec((tm, LANE), lambda ph, i: (i, 0))],
        out_specs=[pl.BlockSpec((tm, LANE), lambda ph, i: (ph * i, 0)),
                   pl.BlockSpec((LANE, LANE), lambda ph, i: (0, 0)),
                   pl.BlockSpec((SUBLANE, LANE), lambda ph, i: (0, 0))],
        out_shape=[jax.ShapeDtypeStruct((ta, LANE), I32),
                   jax.ShapeDtypeStruct((LANE, LANE), I32),
                   jax.ShapeDtypeStruct((SUBLANE, LANE), I32)],
        scratch_shapes=[pltpu.VMEM((1, LANE), F32), pltpu.VMEM((1, LANE), F32), pltpu.VMEM((1, LANE), F32)],
        compiler_params=_cparams(("arbitrary", "arbitrary")),
        name="slots",
    )(eid)


def _expert_body(first_ref, ntile_ref, nv_ref, p0_ref, p1_ref, x_hbm, wg_ref, wu_ref, wd_ref, y_hbm,
                 xbuf, ybuf, wgb, wub, wdb, src_ref, dst_ref, gsem, ssem, *, tr, ta):
    e = pl.program_id(0)
    ne = pl.num_programs(0)
    nt = first_ref[ne - 1] + ntile_ref[ne - 1]

    GATHER, SCATTER = 0, 1
    UNROLL = 8

    def row_copy(jj, sl, r, which):
        n = jj * tr + r
        stage = pl.ds(pl.multiple_of(r * CHUNK, CHUNK), CHUNK)
        if which == GATHER:
            src = pl.ds(pl.multiple_of(src_ref[n] * CHUNK, CHUNK), CHUNK)
            return pltpu.make_async_copy(x_hbm.at[src], xbuf.at[sl, stage], gsem.at[sl])
        dst = pl.ds(pl.multiple_of(dst_ref[n] * CHUNK, CHUNK), CHUNK)
        return pltpu.make_async_copy(ybuf.at[sl, stage], y_hbm.at[dst], ssem.at[sl])

    def for_rows(jj, sl, which, wait):
        def act(r):
            c = row_copy(jj, sl, r, which)
            c.wait() if wait else c.start()

        def block(rb, _):
            for q in range(UNROLL):
                act(rb * UNROLL + q)
            return 0

        if which == GATHER:
            lax.fori_loop(0, tr // UNROLL, block, 0)
            return
        nv = nv_ref[jj]
        full = nv // UNROLL
        lax.fori_loop(0, full, block, 0)
        for q in range(UNROLL - 1):
            @pl.when(full * UNROLL + q < nv)
            def _():
                act(full * UNROLL + q)

    @pl.when(e == 0)
    def _():
        def pads(j, _):
            def one(r, _):
                src_ref[j * tr + r] = 0
                return 0
            return lax.fori_loop(nv_ref[j], tr, one, 0)

        def fill(t, _):
            a = p0_ref[t]
            b = p1_ref[t]
            src_ref[a] = t
            dst_ref[a] = t
            src_ref[b] = t
            dst_ref[b] = ta + t
            return 0

        lax.fori_loop(0, nt, pads, 0)
        lax.fori_loop(0, ta, fill, 0, unroll=4)
        for_rows(0, 0, GATHER, False)

    @pl.when(ntile_ref[e] > 0)
    def _():
        wgb[...] = wg_ref[0].astype(BF16)
        wub[...] = wu_ref[0].astype(BF16)
        wdb[...] = wd_ref[0].astype(BF16)

        def tile(jj, _):
            j = first_ref[e] + jj
            slot = j % 2
            for_rows(j, slot, GATHER, True)
            nxt = jnp.minimum(j + 1, nt - 1)
            for r in range(tr):
                row_copy(nxt, 1 - slot, r, GATHER).start()
            x = _load_flat(xbuf, tr, (slot,)).astype(BF16)
            hg = jnp.dot(x, wgb[...], preferred_element_type=F32)
            hu = jnp.dot(x, wub[...], preferred_element_type=F32)
            h = (hg * jax.nn.sigmoid(hg) * hu).astype(BF16)
            y = jnp.dot(h, wdb[...], preferred_element_type=F32)

            @pl.when(j >= 2)
            def _():
                for_rows(j - 2, slot, SCATTER, True)

            _store_flat(ybuf, y, (slot,))
            for_rows(j, slot, SCATTER, False)
            return 0

        lax.fori_loop(0, ntile_ref[e], tile, 0)

    @pl.when(e == ne - 1)
    def _():
        last = nt - 1
        for_rows(last, last % 2, SCATTER, True)

        @pl.when(nt >= 2)
        def _():
            for_rows(last - 1, 1 - last % 2, SCATTER, True)

        for_rows(last, 1 - last % 2, GATHER, True)


def _experts(first_tile, n_tile, tile_valid, pos0, pos1, x_flat, w_gate, w_up, w_down):
    ta = pos0.shape[0]
    ne, dm, ff = w_gate.shape
    tr = TR
    n_rows = tile_valid.shape[0] * tr
    assert dm == CHUNK * LANE and first_tile.shape == (ne,)
    sp = lambda e, *_: (e, 0, 0)
    any_spec = pl.BlockSpec(memory_space=pl.ANY)
    grid_spec = pltpu.PrefetchScalarGridSpec(
        num_scalar_prefetch=5,
        grid=(ne,),
        in_specs=[any_spec,
                  pl.BlockSpec((1, dm, ff), sp), pl.BlockSpec((1, dm, ff), sp), pl.BlockSpec((1, ff, dm), sp)],
        out_specs=any_spec,
        scratch_shapes=[pltpu.VMEM((2, tr * CHUNK, LANE), F32), pltpu.VMEM((2, tr * CHUNK, LANE), F32),
                        pltpu.VMEM((dm, ff), BF16), pltpu.VMEM((dm, ff), BF16), pltpu.VMEM((ff, dm), BF16),
                        pltpu.SMEM((n_rows,), I32), pltpu.SMEM((n_rows,), I32),
                        pltpu.SemaphoreType.DMA((2,)), pltpu.SemaphoreType.DMA((2,))],
    )
    return pl.pallas_call(
        functools.partial(_expert_body, tr=tr, ta=ta),
        grid_spec=grid_spec,
        out_shape=jax.ShapeDtypeStruct((TOP_K * ta * CHUNK, LANE), F32),
        compiler_params=_cparams(("arbitrary",)),
        name="experts",
    )(first_tile, n_tile, tile_valid, pos0, pos1, x_flat, w_gate, w_up, w_down)


def _final_body(r_ref, y0_ref, y1_ref, wt_ref, g_ref, b_ref, yp_ref, ys_ref, *, n_prompt_tiles):
    i = pl.program_id(0)
    wt = wt_ref[...]
    tm = r_ref.shape[0]
    x = r_ref[...] + wt[:, 0:1] * _load_flat(y0_ref, tm) + wt[:, 1:2] * _load_flat(y1_ref, tm)
    y = _layer_norm(x, g_ref[...], b_ref[...])

    @pl.when(i < n_prompt_tiles)
    def _():
        yp_ref[...] = y

    @pl.when(i >= n_prompt_tiles)
    def _():
        ys_ref[...] = y


def _final(r, y01_flat, wts, ln_g, ln_b, t):
    ta, dm = r.shape
    tm = TM_ROW
    npt = t // tm
    nta = ta // tm
    row = lambda w: pl.BlockSpec((tm, w), lambda i: (i, 0))
    return pl.pallas_call(
        functools.partial(_final_body, n_prompt_tiles=npt),
        grid=(nta,),
        in_specs=[row(dm),
                  pl.BlockSpec((tm * CHUNK, LANE), lambda i: (i, 0)),
                  pl.BlockSpec((tm * CHUNK, LANE), lambda i: (nta + i, 0)),
                  row(LANE), _const_spec(ln_g.shape), _const_spec(ln_b.shape)],
        out_specs=[pl.BlockSpec((tm, dm), lambda i: (jnp.minimum(i, npt - 1), 0)),
                   pl.BlockSpec((tm, dm), lambda i: (0, 0))],
        out_shape=[jax.ShapeDtypeStruct((t, dm), F32), jax.ShapeDtypeStruct((tm, dm), F32)],
        compiler_params=_cparams(("arbitrary",)),
        name="final",
    )(r, y01_flat, y01_flat, wts, ln_g, ln_b)


def _layer(layer, x_p, x_s, pe_p, pe_s, cache_k, cache_v, state, page_table, w):
    t, dm = x_p.shape
    nb = x_s.shape[0]
    wn = N_HEADS * HEAD_W
    ta = t + ROW_PAD
    assert nb <= ROW_PAD and t % ROW_PAD == 0
    lam0 = _lambda_init(layer)
    depth_total = w["depth"]
    alpha = (2.0 * depth_total) ** 0.25
    page = cache_k.shape[1]
    start_pos = page_table.shape[1] * page

    pad = ta - t - nb
    xb = jnp.concatenate([x_p.astype(BF16), x_s.astype(BF16), jnp.zeros((pad, dm), BF16)], axis=0)
    u, qb, k, kb, v, vb, gates = _inproj(xb, w["w_in"].astype(BF16))

    lamv = jnp.concatenate([w["lam_q1"], w["lam_k1"], w["lam_q2"], w["lam_k2"]], axis=0)
    subln_g = w["subln_g"]

    o_p = _prompt_attn(qb, kb, vb, lamv, subln_g, t, lam0)
    kt_pool = jnp.transpose(cache_k, (0, 2, 3, 4, 1)).reshape(cache_k.shape[0], wn, page)
    v_pool = cache_v.reshape(cache_v.shape[0] * page * N_HEADS, HEAD_W)
    o_s = _decode_attn(page_table, lamv, subln_g, qb[t:t + nb], k[t:t + nb], v[t:t + nb],
                       kt_pool, v_pool, lam0)
    o_all = jnp.concatenate([o_p, o_s.astype(BF16), jnp.zeros((pad, wn), BF16)], axis=0)

    d_p = _pool_prompt(u, t)
    d_s = _pool_sample(u[t:t + nb], state, start_pos)
    d_all = jnp.concatenate([d_p, d_s, jnp.zeros((pad, d_p.shape[1]), BF16)], axis=0)

    x1, x1_flat, logits = _mix(d_all, o_all, gates, x_p, x_s, w["pool_w"].astype(BF16), w["pool_scale"],
                               w["w_branch_pool"].astype(BF16), w["w_branch_attn"].astype(BF16),
                               w["w_out"].astype(BF16), w["ln1_g"], w["ln1_b"], w["router_w"], w["router_b"], alpha)

    pe_b = jnp.concatenate([pe_p.astype(BF16), pe_s.astype(BF16), jnp.zeros((pad, pe_p.shape[1]), BF16)], axis=0)
    r = _ple(x1, pe_b, w["ple_gate_w"].astype(BF16), w["ple_gate_b"], w["ple_proj"].astype(BF16), alpha)

    eid, wts = _route(logits)
    pos, meta, emeta = _slots(eid)
    n_row_tiles = (TOP_K * ta) // TR + N_EXPERTS
    assert n_row_tiles <= LANE
    y01_flat = _experts(emeta[0, :N_EXPERTS], emeta[1, :N_EXPERTS], meta[:n_row_tiles, 1], pos[:, 0], pos[:, 1],
                        x1_flat, w["expert_gate"], w["expert_up"], w["expert_down"])
    y_p, y_s = _final(r, y01_flat, wts, w["ln2_g"], w["ln2_b"], t)
    return y_p, y_s[:nb], k, v, u


def kernel(x_prompt, x_sample, p_prompt, p_sample, cache_k, cache_v, state_pool, page_table, w_in, pool_w, pool_scale, lam_q1, lam_k1, lam_q2, lam_k2, subln_g, w_branch_pool, w_branch_attn, w_out, ln1_g, ln1_b, router_group_w, router_group_b, router_expert_w, router_expert_b, expert_gate, expert_up, expert_down, ple_proj, ple_gate_w, ple_gate_b, ln2_g, ln2_b):
    depth = w_in.shape[0]
    bp, t, dm = x_prompt.shape
    nb, ts, _ = x_sample.shape
    assert bp == 1 and ts == 1 and dm // 256 == N_HEADS
    assert dm % CHUNK == 0 and dm // CHUNK == LANE
    hp = x_prompt.reshape(t, dm)
    hs = x_sample.reshape(nb, dm)
    outs = {n: [] for n in ("kp", "vp", "sp", "ks", "vs", "ss")}
    for l in range(depth):
        wr = jnp.concatenate([router_group_w[l],
                              jnp.transpose(router_expert_w[l], (1, 0, 2)).reshape(dm, N_EXPERTS)], axis=1)
        wr = jnp.pad(wr, ((0, 0), (0, LANE - wr.shape[1])))
        wr_hi = wr.astype(BF16)
        wr = jnp.stack([wr_hi, (wr - wr_hi.astype(F32)).astype(BF16)])
        br = jnp.pad(jnp.concatenate([router_group_b[l], router_expert_b[l].reshape(-1)]),
                     (0, LANE - N_GROUPS - N_EXPERTS)).reshape(1, LANE)
        row = lambda a: a[l].reshape(1, -1)
        w = dict(depth=depth, w_in=w_in[l], pool_w=pool_w[l], pool_scale=row(pool_scale),
                 lam_q1=row(lam_q1), lam_k1=row(lam_k1), lam_q2=row(lam_q2), lam_k2=row(lam_k2),
                 subln_g=row(subln_g), w_branch_pool=w_branch_pool[l], w_branch_attn=w_branch_attn[l],
                 w_out=w_out[l], ln1_g=row(ln1_g), ln1_b=row(ln1_b), router_w=wr, router_b=br,
                 expert_gate=expert_gate[l], expert_up=expert_up[l], expert_down=expert_down[l],
                 ple_proj=ple_proj[l], ple_gate_w=ple_gate_w[l], ple_gate_b=row(ple_gate_b),
                 ln2_g=row(ln2_g), ln2_b=row(ln2_b))
        state = state_pool[l]
        hp, hs, k, v, u = _layer(l, hp, hs, p_prompt[l, 0], p_sample[l, :, 0], cache_k[l], cache_v[l],
                                 state, page_table, w)
        outs["kp"].append(k[:t].reshape(1, t, N_HEADS, 2, HEAD_DIM))
        outs["vp"].append(v[:t].reshape(1, t, N_HEADS, HEAD_W))
        outs["sp"].append(u[t - POOL_STATE:t].reshape(1, POOL_STATE, -1))
        outs["ks"].append(k[t:t + nb].reshape(nb, 1, N_HEADS, 2, HEAD_DIM))
        outs["vs"].append(v[t:t + nb].reshape(nb, 1, N_HEADS, HEAD_W))
        outs["ss"].append(jnp.concatenate([state[:, 1:], u[t:t + nb][:, None, :]], axis=1))
    st = lambda n: jnp.stack(outs[n])
    return (hp.reshape(1, t, dm), hs.reshape(nb, 1, dm), st("kp"), st("vp"), st("sp"),
            st("ks"), st("vs"), st("ss"))
```

```python
import functools
import math

import jax
import jax.numpy as jnp
from jax import lax
from jax.experimental import pallas as pl
from jax.experimental.pallas import tpu as pltpu

F32 = jnp.float32
BF16 = jnp.bfloat16
I32 = jnp.int32

N_HEADS = 8
HEAD_DIM = 64
HEAD_W = 2 * HEAD_DIM
ATTN_SCALE = HEAD_DIM ** -0.5
LOG2E = math.log2(math.e)
SUBLN_EPS = 1e-5
LN_EPS = 1e-5
POOL_WINDOWS = (2, 4, 8, 16)
POOL_STATE = max(POOL_WINDOWS) - 1
N_GROUPS = 4
EXPERTS_PER_GROUP = 8
N_EXPERTS = N_GROUPS * EXPERTS_PER_GROUP
TOP_K = 2

LANE = 128
SUBLANE = 8
VMEM_LIMIT = 56 * 1024 * 1024

ROW_PAD = 256
TM_INPROJ = 768
TM_ROW = 256
TQ = 512
TK = 512
HEADS_PER_STEP = 1
TR = 256
HALO = 16
CHUNK = 16
DECODE_GROUP = 4
DECODE_SLOTS = 12


def _lambda_init(layer):
    return 0.8 - 0.6 * math.exp(-0.3 * layer)


def _cparams(sem, vmem=VMEM_LIMIT):
    return pltpu.CompilerParams(dimension_semantics=sem, vmem_limit_bytes=vmem)


def _const_spec(shape):
    nd = len(shape)
    return pl.BlockSpec(shape, lambda *_: (0,) * nd, pipeline_mode=pl.Buffered(1))


def _inproj_body(x_ref, w_ref, u_ref, q_ref, k_ref, kb_ref, v_ref, vb_ref, g_ref):
    j = pl.program_id(0)

    def z():
        return jnp.dot(x_ref[...], w_ref[...], preferred_element_type=F32)

    @pl.when(j == 0)
    def _():
        u_ref[...] = z()

    @pl.when(j == 1)
    def _():
        q_ref[...] = (z() * (ATTN_SCALE * LOG2E)).astype(BF16)

    @pl.when(j == 2)
    def _():
        kz = z()
        k_ref[...] = kz
        kb_ref[...] = kz.astype(BF16)

    @pl.when(j == 3)
    def _():
        vz = z()
        v_ref[...] = vz
        vb_ref[...] = vz.astype(BF16)

    @pl.when(j >= 4)
    def _():
        g_ref[...] = jax.nn.sigmoid(z()).astype(BF16)


def _inproj(xb, w_in_b):
    ta, d = xb.shape
    n_in = w_in_b.shape[1]
    wn = N_HEADS * HEAD_W
    assert n_in == 4 * wn + 2 * d and (2 * d) % wn == 0
    tm = TM_INPROJ
    assert ta % tm == 0
    ni = ta // tm
    nj = n_in // wn

    def once(jo):
        return lambda j, i: (jnp.where(j < jo, 0, jnp.where(j == jo, i, ni - 1)), 0)

    blk = lambda jo: pl.BlockSpec((tm, wn), once(jo))
    g_spec = pl.BlockSpec((tm, wn), lambda j, i: (jnp.where(j < 4, 0, i), jnp.maximum(j - 4, 0)))
    return pl.pallas_call(
        _inproj_body,
        grid=(nj, ni),
        in_specs=[pl.BlockSpec((tm, d), lambda j, i: (i, 0)),
                  pl.BlockSpec((d, wn), lambda j, i: (0, j))],
        out_specs=[blk(0), blk(1), blk(2), blk(2), blk(3), blk(3), g_spec],
        out_shape=[jax.ShapeDtypeStruct((ta, wn), F32),
                   jax.ShapeDtypeStruct((ta, wn), BF16),
                   jax.ShapeDtypeStruct((ta, wn), F32),
                   jax.ShapeDtypeStruct((ta, wn), BF16),
                   jax.ShapeDtypeStruct((ta, wn), F32),
                   jax.ShapeDtypeStruct((ta, wn), BF16),
                   jax.ShapeDtypeStruct((ta, 2 * d), BF16)],
        compiler_params=_cparams(("arbitrary", "arbitrary")),
        name="inproj",
    )(xb, w_in_b)


def _lam_value(lamv_ref, lam0):
    lv = lamv_ref[...]
    a = jnp.sum(lv[0:1] * lv[1:2], axis=1, keepdims=True)
    b = jnp.sum(lv[2:3] * lv[3:4], axis=1, keepdims=True)
    return jnp.exp(a) - jnp.exp(b) + lam0


def _subln(o, g, lam0):
    ms = jnp.mean(o * o, axis=-1, keepdims=True)
    return o * lax.rsqrt(ms + SUBLN_EPS) * g * (1.0 - lam0)


def _prompt_attn_body(lamv_ref, gcol_ref, q_ref, k_ref, v_ref, o_ref, vt_ref, q2t_ref, s_even, s_odd,
                      m_ref, l_ref, acc_ref, *, tq, tk, hp, lam0):
    i = pl.program_id(1)
    nkv = v_ref.shape[0] // tk
    heads = [pl.ds(hh * HEAD_W, HEAD_W) for hh in range(hp)]

    @pl.when(i == 0)
    def _():
        def tr(jb, _):
            for hh in range(hp):
                blk = v_ref[pl.ds(pl.multiple_of(jb * tk, tk), tk), heads[hh]].astype(F32)
                vt_ref[hh, jb] = blk.T.astype(BF16)
            return 0
        lax.fori_loop(0, nkv, tr, 0)

    drow = lax.broadcasted_iota(I32, (HEAD_W, tq), 0)
    for hh in range(hp):
        qt = q_ref[:, heads[hh]].astype(F32).T
        q2t_ref[hh] = jnp.concatenate([jnp.where(drow < HEAD_DIM, qt, 0.0),
                                       jnp.where(drow >= HEAD_DIM, qt, 0.0)], axis=1).astype(BF16)
    m_ref[...] = jnp.full(m_ref.shape, -jnp.inf, F32)
    l_ref[...] = jnp.zeros(l_ref.shape, F32)
    acc_ref[...] = jnp.zeros(acc_ref.shape, F32)
    n_full = (i * tq) // tk

    def scores(j, s_ref):
        rows = pl.ds(pl.multiple_of(j * tk, tk), tk)
        for hh in range(hp):
            s_ref[hh] = jnp.dot(k_ref[rows, heads[hh]], q2t_ref[hh], preferred_element_type=F32)

    def absorb(j, s_ref, masked):
        for hh in range(hp):
            s = s_ref[hh]
            if masked:
                kpos = n_full * tk + lax.broadcasted_iota(I32, (tk, 2 * tq), 0)
                qcol = lax.broadcasted_iota(I32, (tk, 2 * tq), 1)
                qpos = i * tq + jnp.where(qcol >= tq, qcol - tq, qcol)
                s = jnp.where(kpos <= qpos, s, -jnp.inf)
            m = m_ref[hh]
            m_new = jnp.maximum(m, jnp.max(s, axis=0, keepdims=True))
            alpha = jnp.exp2(m - m_new)
            p = jnp.exp2(s - m_new)
            l_ref[hh] = alpha * l_ref[hh] + jnp.sum(p, axis=0, keepdims=True)
            acc_ref[hh] = alpha * acc_ref[hh] + jnp.dot(vt_ref[hh, j], p.astype(BF16), preferred_element_type=F32)
            m_ref[hh] = m_new

    def by_parity(j, fn):
        @pl.when(j % 2 == 0)
        def _():
            fn(s_even, s_odd)

        @pl.when(j % 2 == 1)
        def _():
            fn(s_odd, s_even)

    scores(0, s_even)

    def body(j, _):
        def stage(cur, nxt):
            scores(j + 1, nxt)
            absorb(j, cur, False)
        by_parity(j, stage)
        return 0

    lax.fori_loop(0, n_full, body, 0)
    by_parity(n_full, lambda cur, nxt: absorb(n_full, cur, True))

    lam = _lam_value(lamv_ref, lam0)
    for hh in range(hp):
        on = acc_ref[hh] / l_ref[hh]
        ot = on[:, :tq] - lam * on[:, tq:]
        ms = jnp.mean(ot * ot, axis=0, keepdims=True)
        ot = ot * lax.rsqrt(ms + SUBLN_EPS) * gcol_ref[...] * (1.0 - lam0)
        o_ref[:, heads[hh]] = ot.T.astype(o_ref.dtype)


def _prompt_attn(qb, kb, vb, lamv, subln_g, t, lam0):
    tq, tk, hp = TQ, TK, HEADS_PER_STEP
    assert t % tk == 0 and tk % tq == 0 and N_HEADS % hp == 0
    kern = functools.partial(_prompt_attn_body, tq=tq, tk=tk, hp=hp, lam0=lam0)
    gcol = subln_g.reshape(HEAD_W, 1)
    return pl.pallas_call(
        kern,
        grid=(N_HEADS // hp, t // tq),
        in_specs=[_const_spec(lamv.shape), _const_spec(gcol.shape),
                  pl.BlockSpec((tq, hp * HEAD_W), lambda h, i: (i, h)),
                  pl.BlockSpec((t, hp * HEAD_W), lambda h, i: (0, h)),
                  pl.BlockSpec((t, hp * HEAD_W), lambda h, i: (0, h))],
        out_specs=pl.BlockSpec((tq, hp * HEAD_W), lambda h, i: (i, h)),
        out_shape=jax.ShapeDtypeStruct((t, N_HEADS * HEAD_W), BF16),
        scratch_shapes=[pltpu.VMEM((hp, t // tk, HEAD_W, tk), BF16),
                        pltpu.VMEM((hp, HEAD_W, 2 * tq), BF16),
                        pltpu.VMEM((hp, tk, 2 * tq), F32), pltpu.VMEM((hp, tk, 2 * tq), F32),
                        pltpu.VMEM((hp, 1, 2 * tq), F32), pltpu.VMEM((hp, 1, 2 * tq), F32),
                        pltpu.VMEM((hp, HEAD_W, 2 * tq), F32)],
        compiler_params=_cparams(("arbitrary", "arbitrary")),
        name="prompt_attn",
    )(lamv, gcol, qb, kb, vb)


def _decode_attn_body(pt_ref, lamv_ref, g_ref, q_ref, qc_ref, kn_ref, vn_ref, kt_hbm, v_hbm, o_ref,
                      kbuf, vbuf, sem, *, n_pages, page, n_slots, group, lam0):
    b = pl.program_id(0)
    nb = pl.num_programs(0)
    nr = 2 * N_HEADS
    wn = N_HEADS * HEAD_W
    vrows = page * N_HEADS
    ahead = n_slots - group
    total = nb * n_pages
    assert n_pages % group == 0 and n_slots % group == 0 and ahead >= group

    def copies(g):
        slot = g % n_slots
        pg = pt_ref[g // n_pages, g % n_pages]
        ck = pltpu.make_async_copy(kt_hbm.at[pg], kbuf.at[slot], sem.at[0, slot])
        cv = pltpu.make_async_copy(v_hbm.at[pl.ds(pl.multiple_of(pg * vrows, vrows), vrows)],
                                   vbuf.at[slot], sem.at[1, slot])
        return ck, cv

    def start(g):
        for c in copies(g):
            c.start()

    @pl.when(b == 0)
    def _():
        for g in range(ahead):
            start(g)

    rr = lax.broadcasted_iota(I32, (nr, wn), 0)
    cc = lax.broadcasted_iota(I32, (nr, wn), 1)
    qrow = q_ref[0]
    qbd = jnp.where(cc // HEAD_DIM == rr, jnp.broadcast_to(qrow.astype(F32), (nr, wn)), 0.0).astype(BF16)
    rhead = lax.broadcasted_iota(I32, (nr, HEAD_W), 0) // 2
    qcol = jnp.broadcast_to(qc_ref[0], (wn, page))
    tok = lax.broadcasted_iota(I32, (page, vrows), 0)
    vrow = lax.broadcasted_iota(I32, (page, vrows), 1)
    spread = jnp.where(vrow // N_HEADS == tok, 1.0, 0.0).astype(BF16)
    own_head = lax.broadcasted_iota(I32, (nr, vrows), 1) % N_HEADS == lax.broadcasted_iota(I32, (nr, vrows), 0) // 2

    def body(jg, carry):
        m, l, acc = carry
        g0 = b * n_pages + jg * group
        slots = []
        ss = []
        for a in range(group):
            g = g0 + a

            @pl.when(g + ahead < total)
            def _():
                start(g + ahead)

            for c in copies(g):
                c.wait()
            slot = g % n_slots
            slots.append(slot)
            prod = kbuf[slot] * qcol
            ss.append(jnp.sum(prod.reshape(nr, HEAD_DIM, page), axis=1))
        m_new = m
        for s in ss:
            m_new = jnp.maximum(m_new, jnp.max(s, axis=1, keepdims=True))
        alpha = jnp.exp2(m - m_new)
        l = alpha * l
        acc = alpha * acc
        for s, slot in zip(ss, slots):
            p = jnp.exp2(s - m_new)
            l = l + jnp.sum(p, axis=1, keepdims=True)
            pe = jnp.dot(p.astype(BF16), spread, preferred_element_type=F32)
            pe = jnp.where(own_head, pe, 0.0).astype(BF16)
            acc = acc + jnp.dot(pe, vbuf[slot].astype(BF16), preferred_element_type=F32)
        return m_new, l, acc

    init = (jnp.full((nr, 1), -jnp.inf, F32), jnp.zeros((nr, 1), F32), jnp.zeros((nr, HEAD_W), F32))
    m, l, acc = lax.fori_loop(0, n_pages // group, body, init)

    kn = jnp.broadcast_to(kn_ref[0], (nr, wn))
    s_new = jnp.sum(qbd.astype(F32) * kn, axis=1, keepdims=True)
    vn = vn_ref[0]
    vnew = jnp.zeros((nr, HEAD_W), F32)
    for h in range(N_HEADS):
        vnew = vnew + jnp.where(rhead == h, jnp.broadcast_to(vn[:, h * HEAD_W:(h + 1) * HEAD_W], (nr, HEAD_W)), 0.0)
    m_new = jnp.maximum(m, s_new)
    alpha = jnp.exp2(m - m_new)
    p_new = jnp.exp2(s_new - m_new)
    l = alpha * l + p_new
    acc = alpha * acc + p_new * vnew
    on = acc / l
    lam = _lam_value(lamv_ref, lam0)
    o = on - lam * pltpu.roll(on, nr - 1, 0)
    o_ref[0] = _subln(o, g_ref[...], lam0)


def _decode_attn(page_table, lamv, subln_g, q_s, k_s, v_s, kt_pool, v_pool, lam0):
    nb, n_pages = page_table.shape
    wn = N_HEADS * HEAD_W
    page = kt_pool.shape[2]
    nr = 2 * N_HEADS
    n_slots = DECODE_SLOTS
    kern = functools.partial(_decode_attn_body, n_pages=n_pages, page=page, n_slots=n_slots,
                             group=DECODE_GROUP, lam0=lam0)
    grid_spec = pltpu.PrefetchScalarGridSpec(
        num_scalar_prefetch=1,
        grid=(nb,),
        in_specs=[pl.BlockSpec(lamv.shape, lambda b, pt: (0, 0)),
                  pl.BlockSpec(subln_g.shape, lambda b, pt: (0, 0)),
                  pl.BlockSpec((1, 1, wn), lambda b, pt: (b, 0, 0)),
                  pl.BlockSpec((1, wn, 1), lambda b, pt: (b, 0, 0)),
                  pl.BlockSpec((1, 1, wn), lambda b, pt: (b, 0, 0)),
                  pl.BlockSpec((1, 1, wn), lambda b, pt: (b, 0, 0)),
                  pl.BlockSpec(memory_space=pl.ANY),
                  pl.BlockSpec(memory_space=pl.ANY)],
        out_specs=pl.BlockSpec((1, nr, HEAD_W), lambda b, pt: (b, 0, 0)),
        scratch_shapes=[pltpu.VMEM((n_slots, wn, page), F32),
                        pltpu.VMEM((n_slots, page * N_HEADS, HEAD_W), F32),
                        pltpu.SemaphoreType.DMA((2, n_slots))],
    )
    out = pl.pallas_call(
        kern,
        grid_spec=grid_spec,
        out_shape=jax.ShapeDtypeStruct((nb, nr, HEAD_W), F32),
        compiler_params=_cparams(("arbitrary",)),
        name="decode_attn",
    )(page_table, lamv, subln_g, q_s.reshape(nb, 1, wn), q_s.astype(F32).reshape(nb, wn, 1),
      k_s.reshape(nb, 1, wn), v_s.reshape(nb, 1, wn), kt_pool, v_pool)
    return out[:, 0::2, :].reshape(nb, wn)


def _pool_prompt_body(u_ref, halo_ref, d_ref, ext_ref, *, tm, gw):
    i = pl.program_id(0)
    ext_ref[pl.ds(HALO, tm), :] = u_ref[...]

    @pl.when(i == 0)
    def _():
        ext_ref[pl.ds(0, HALO), :] = jnp.zeros((HALO, u_ref.shape[1]), F32)

    @pl.when(i > 0)
    def _():
        ext_ref[pl.ds(0, HALO), :] = halo_ref[...]

    pos = i * tm + lax.broadcasted_iota(I32, (tm, 1), 0)
    for g, w in enumerate(POOL_WINDOWS):
        cols = pl.ds(g * gw, gw)
        tok = ext_ref[pl.ds(HALO, tm), cols]
        s = tok
        for back in range(1, w):
            s = s + ext_ref[pl.ds(HALO - back, tm), cols]
        inv = 1.0 / jnp.minimum(pos + 1, w).astype(F32)
        d_ref[:, cols] = (s * inv - tok).astype(d_ref.dtype)


def _pool_prompt(u_all, t):
    wp = u_all.shape[1]
    gw = wp // len(POOL_WINDOWS)
    tm = TM_ROW
    assert t % tm == 0 and tm % HALO == 0 and gw % LANE == 0
    kern = functools.partial(_pool_prompt_body, tm=tm, gw=gw)
    return pl.pallas_call(
        kern,
        grid=(t // tm,),
        in_specs=[pl.BlockSpec((tm, wp), lambda i: (i, 0)),
                  pl.BlockSpec((HALO, wp), lambda i: (jnp.maximum(i * (tm // HALO) - 1, 0), 0))],
        out_specs=pl.BlockSpec((tm, wp), lambda i: (i, 0)),
        out_shape=jax.ShapeDtypeStruct((t, wp), BF16),
        scratch_shapes=[pltpu.VMEM((tm + HALO, wp), F32)],
        compiler_params=_cparams(("arbitrary",)),
        name="pool_prompt",
    )(u_all, u_all)


def _pool_sample_body(u_ref, st_ref, d_ref, *, gw, start_pos):
    st = st_ref[...]
    u = u_ref[...]
    ridx = lax.broadcasted_iota(I32, st.shape, 1)
    for g, w in enumerate(POOL_WINDOWS):
        hist = jnp.sum(jnp.where(ridx >= POOL_STATE - (w - 1), st, 0.0), axis=1)
        cnt = float(min(start_pos + 1, w))
        d = (hist + u) * (1.0 / cnt) - u
        d_ref[:, g * gw:(g + 1) * gw] = d[:, g * gw:(g + 1) * gw].astype(d_ref.dtype)


def _pool_sample(u_s, state, start_pos):
    nb, wp = u_s.shape
    gw = wp // len(POOL_WINDOWS)
    kern = functools.partial(_pool_sample_body, gw=gw, start_pos=start_pos)
    return pl.pallas_call(
        kern,
        grid=(1,),
        in_specs=[_const_spec(u_s.shape), _const_spec(state.shape)],
        out_specs=_const_spec((nb, wp)),
        out_shape=jax.ShapeDtypeStruct((nb, wp), BF16),
        compiler_params=_cparams(("arbitrary",)),
        name="pool_sample",
    )(u_s, state)


def _store_flat(flat_ref, x, lead=()):
    rows = x.shape[0]
    for c in range(CHUNK):
        flat_ref[(*lead, pl.ds(c, rows, stride=CHUNK), slice(None))] = x[:, c * LANE:(c + 1) * LANE]


def _load_flat(flat_ref, rows, lead=(), first=0):
    return jnp.concatenate(
        [flat_ref[(*lead, pl.ds(first * CHUNK + c, rows, stride=CHUNK), slice(None))] for c in range(CHUNK)],
        axis=1)


def _layer_norm(x, g, b):
    mu = jnp.mean(x, axis=-1, keepdims=True)
    xc = x - mu
    var = jnp.mean(xc * xc, axis=-1, keepdims=True)
    return xc * lax.rsqrt(var + LN_EPS) * g + b


def _mix_body(d_ref, o_ref, g_ref, xp_ref, xs_ref, pw_ref, ps_ref, wbp_ref, wba_ref, wo_ref,
              lg_ref, lb_ref, wr_ref, br_ref, x1f_ref, lo_ref, *, n_prompt_tiles, alpha):
    i = pl.program_id(0)
    tm, dm = xp_ref.shape
    ng = len(POOL_WINDOWS)
    gw = d_ref.shape[1] // ng
    ys = []
    for g in range(ng):
        ys.append(jnp.dot(d_ref[:, g * gw:(g + 1) * gw], pw_ref[g], preferred_element_type=F32))
    pool_y = (jnp.concatenate(ys, axis=1) * ps_ref[...]).astype(BF16)
    a = jnp.dot(pool_y, wbp_ref[...], preferred_element_type=F32)
    bb = jnp.dot(o_ref[...], wba_ref[...], preferred_element_type=F32)
    gates = g_ref[...].astype(F32)
    merged = (gates[:, :dm] * a + gates[:, dm:] * bb).astype(BF16)
    mix = jnp.dot(merged, wo_ref[...], preferred_element_type=F32)

    def finish(x):
        x1 = _layer_norm(alpha * x + mix, lg_ref[...], lb_ref[...])
        _store_flat(x1f_ref, x1)
        hi = x1.astype(BF16)
        lo = (x1 - hi.astype(F32)).astype(BF16)
        dot = functools.partial(jnp.dot, preferred_element_type=F32)
        lo_ref[...] = dot(hi, wr_ref[0]) + (dot(lo, wr_ref[0]) + dot(hi, wr_ref[1])) + br_ref[...]

    @pl.when(i < n_prompt_tiles)
    def _():
        finish(xp_ref[...])

    @pl.when(i >= n_prompt_tiles)
    def _():
        ns = xs_ref.shape[0]
        finish(jnp.concatenate([xs_ref[...], jnp.zeros((tm - ns, dm), F32)], axis=0))


def _mix(d_all, o_all, gates, x_p, x_s, pool_w_b, pool_scale, wbp, wba, wo, ln_g, ln_b, wr, br, alpha):
    ta = d_all.shape[0]
    t, dm = x_p.shape
    ns = x_s.shape[0]
    tm = TM_ROW
    assert ta % tm == 0 and t % tm == 0 and ns % SUBLANE == 0 and ns <= tm and ta == t + tm
    npt = t // tm
    wp = d_all.shape[1]
    wa = o_all.shape[1]
    row = lambda w: pl.BlockSpec((tm, w), lambda i: (i, 0))
    kern = functools.partial(_mix_body, n_prompt_tiles=npt, alpha=alpha)
    return pl.pallas_call(
        kern,
        grid=(ta // tm,),
        in_specs=[row(wp), row(wa), row(2 * dm),
                  pl.BlockSpec((tm, dm), lambda i: (jnp.minimum(i, npt - 1), 0)),
                  _const_spec(x_s.shape),
                  _const_spec(pool_w_b.shape), _const_spec(pool_scale.shape),
                  _const_spec(wbp.shape), _const_spec(wba.shape), _const_spec(wo.shape),
                  _const_spec(ln_g.shape), _const_spec(ln_b.shape),
                  _const_spec(wr.shape), _const_spec(br.shape)],
        out_specs=[pl.BlockSpec((tm * CHUNK, LANE), lambda i: (i, 0)), row(LANE)],
        out_shape=[jax.ShapeDtypeStruct((ta * CHUNK, LANE), F32),
                   jax.ShapeDtypeStruct((ta, LANE), F32)],
        compiler_params=_cparams(("arbitrary",)),
        name="branch_mix",
    )(d_all, o_all, gates, x_p, x_s, pool_w_b, pool_scale, wbp, wba, wo, ln_g, ln_b, wr, br)


def _route_body(lo_ref, eid_ref, wt_ref):
    lo = lo_ref[...]
    lane_i = lax.broadcasted_iota(I32, lo.shape, 1)
    lane = lane_i.astype(F32)
    ninf = -jnp.inf
    big = float(LANE)
    is_g = lane_i < N_GROUPS
    gl = jnp.where(is_g, lo, ninf)
    gmax = jnp.max(gl, axis=1, keepdims=True)
    gidx = jnp.min(jnp.where(gl == gmax, lane, big), axis=1, keepdims=True)
    gsum = jnp.sum(jnp.where(is_g, jnp.exp(gl - gmax), 0.0), axis=1, keepdims=True)
    gw = 1.0 / gsum
    lo_e = N_GROUPS + EXPERTS_PER_GROUP * gidx
    el = jnp.where(jnp.logical_and(lane >= lo_e, lane < lo_e + EXPERTS_PER_GROUP), lo, ninf)
    v1 = jnp.max(el, axis=1, keepdims=True)
    i1 = jnp.min(jnp.where(el == v1, lane, big), axis=1, keepdims=True)
    el2 = jnp.where(lane == i1, ninf, el)
    v2 = jnp.max(el2, axis=1, keepdims=True)
    i2 = jnp.min(jnp.where(el2 == v2, lane, big), axis=1, keepdims=True)
    e21 = jnp.exp(v2 - v1)
    w1 = gw / (1.0 + e21)
    w2 = gw * e21 / (1.0 + e21)
    eid = jnp.where(lane_i == 0, i1 - N_GROUPS, jnp.where(lane_i == 1, i2 - N_GROUPS, 0.0))
    eid_ref[...] = eid.astype(I32)
    wt_ref[...] = jnp.where(lane_i == 0, w1, jnp.where(lane_i == 1, w2, 0.0))


def _route(logits):
    ta = logits.shape[0]
    tm = TM_ROW
    row = pl.BlockSpec((tm, LANE), lambda i: (i, 0))
    return pl.pallas_call(
        _route_body,
        grid=(ta // tm,),
        in_specs=[row],
        out_specs=[row, row],
        out_shape=[jax.ShapeDtypeStruct((ta, LANE), I32), jax.ShapeDtypeStruct((ta, LANE), F32)],
        compiler_params=_cparams(("arbitrary",)),
        name="route",
    )(logits)


def _slots_body(eid_ref, pos_ref, meta_ref, emeta_ref, cnt_ref, run_ref, off_ref, *, tm, tr):
    ph = pl.program_id(0)
    i = pl.program_id(1)
    eid = eid_ref[...]
    lane = lax.broadcasted_iota(I32, (tm, LANE), 1)
    e0 = eid[:, 0:1]
    e1 = eid[:, 1:2]
    oh0 = (lane == e0).astype(F32)
    oh1 = (lane == e1).astype(F32)
    both = oh0 + oh1

    @pl.when(jnp.logical_and(ph == 0, i == 0))
    def _():
        cnt_ref[...] = jnp.zeros_like(cnt_ref)

    @pl.when(ph == 0)
    def _():
        cnt_ref[...] = cnt_ref[...] + jnp.sum(both, axis=0, keepdims=True)

    @pl.when(jnp.logical_and(ph == 1, i == 0))
    def _():
        cnt = cnt_ref[...]
        tiles = jnp.floor((cnt + (tr - 1)) * (1.0 / tr))
        a = lax.broadcasted_iota(I32, (LANE, LANE), 0)
        b = lax.broadcasted_iota(I32, (LANE, LANE), 1)
        upper = (a < b).astype(BF16)
        prefix = lambda v: jnp.dot(jnp.broadcast_to(v, (SUBLANE, LANE)).astype(BF16), upper,
                                   preferred_element_type=F32)[0:1]
        off_tiles = prefix(tiles)
        cnt_hi = jnp.floor(cnt * (1.0 / 256.0))
        off_rows = 256.0 * prefix(cnt_hi) + prefix(cnt - 256.0 * cnt_hi)
        off_ref[...] = off_rows
        run_ref[...] = jnp.zeros_like(run_ref)
        end_tiles = off_tiles + tiles
        lane1 = lax.broadcasted_iota(I32, (LANE, LANE), 1)
        jrow = lax.broadcasted_iota(I32, (LANE, LANE), 0).astype(F32)
        done = jnp.logical_and(lane1 < N_EXPERTS, jnp.broadcast_to(end_tiles, (LANE, LANE)) <= jrow)
        te = jnp.sum(done.astype(F32), axis=1, keepdims=True)
        owns = jnp.logical_and(jnp.broadcast_to(off_tiles, (LANE, LANE)) <= jrow,
                               jrow < jnp.broadcast_to(end_tiles, (LANE, LANE)))
        left = jnp.broadcast_to(cnt, (LANE, LANE)) - (jrow - jnp.broadcast_to(off_tiles, (LANE, LANE))) * tr
        nv = jnp.sum(jnp.where(owns, jnp.minimum(left, float(tr)), 0.0), axis=1, keepdims=True)
        first = jnp.broadcast_to(off_rows, (LANE, LANE)) + (jrow - jnp.broadcast_to(off_tiles, (LANE, LANE))) * tr
        start = jnp.sum(jnp.where(owns, first, 0.0), axis=1, keepdims=True)
        meta_ref[...] = jnp.where(lane1 == 0, jnp.broadcast_to(te, (LANE, LANE)),
                                  jnp.where(lane1 == 1, jnp.broadcast_to(nv, (LANE, LANE)),
                                            jnp.broadcast_to(start, (LANE, LANE)))).astype(I32)
        erow = lax.broadcasted_iota(I32, (SUBLANE, LANE), 0)
        emeta_ref[...] = jnp.where(erow == 0, jnp.broadcast_to(off_tiles, (SUBLANE, LANE)),
                                   jnp.broadcast_to(tiles, (SUBLANE, LANE))).astype(I32)

    @pl.when(ph == 1)
    def _():
        r = lax.broadcasted_iota(I32, (tm, tm), 0)
        c = lax.broadcasted_iota(I32, (tm, tm), 1)
        lower = (c < r).astype(BF16)
        before = jnp.dot(lower, both.astype(BF16), preferred_element_type=F32) + run_ref[...] + off_ref[...]
        p0 = jnp.sum(oh0 * before, axis=1, keepdims=True)
        p1 = jnp.sum(oh1 * before, axis=1, keepdims=True)
        pos_ref[...] = jnp.where(lane == 0, p0, jnp.where(lane == 1, p1, 0.0)).astype(I32)
        run_ref[...] = run_ref[...] + jnp.sum(both, axis=0, keepdims=True)


def _slots(eid):
    ta = eid.shape[0]
    tm = TM_ROW
    kern = functools.partial(_slots_body, tm=tm, tr=TR)
    return pl.pallas_call(
        kern,
        grid=(2, ta // tm),
        in_specs=[pl.BlockSpec((tm, LANE), lambda ph, i: (i, 0))],
        out_specs=[pl.BlockSpec((tm, LANE), lambda ph, i: (ph * i, 0)),
                   pl.BlockSpec((LANE, LANE), lambda ph, i: (0, 0)),
                   pl.BlockSpec((SUBLANE, LANE), lambda ph, i: (0, 0))],
        out_shape=[jax.ShapeDtypeStruct((ta, LANE), I32),
                   jax.ShapeDtypeStruct((LANE, LANE), I32),
                   jax.ShapeDtypeStruct((SUBLANE, LANE), I32)],
        scratch_shapes=[pltpu.VMEM((1, LANE), F32), pltpu.VMEM((1, LANE), F32), pltpu.VMEM((1, LANE), F32)],
        compiler_params=_cparams(("arbitrary", "arbitrary")),
        name="slots",
    )(eid)


def _ple_body(p0_ref, p1_ref, x1f_ref, pe_ref, gw_ref, gb_ref, pp_ref, r_ref, xs_hbm, zbuf, sem, zsem,
              *, alpha, n_pairs):
    i = pl.program_id(0)
    tm = r_ref.shape[0]

    def send_row(r, choice):
        pos = (p0_ref, p1_ref)[choice][i * tm + r]
        dst = pl.ds(pl.multiple_of(pos * CHUNK, CHUNK), CHUNK)
        src = pl.ds(r * CHUNK, CHUNK)
        return pltpu.make_async_copy(x1f_ref.at[src], xs_hbm.at[dst], sem)

    for r in range(tm):
        for choice in range(TOP_K):
            send_row(r, choice).start()
    x1 = _load_flat(x1f_ref, tm)
    gate = jax.nn.sigmoid(jnp.dot(x1.astype(BF16), gw_ref[...], preferred_element_type=F32) + gb_ref[...])
    emb = jnp.dot(pe_ref[...], pp_ref[...], preferred_element_type=F32)
    r_ref[...] = alpha * x1 + gate * emb

    @pl.when(i == 0)
    def _():
        zbuf[...] = jnp.zeros(zbuf.shape, F32)
        tail = pltpu.make_async_copy(zbuf, xs_hbm.at[pl.ds(n_pairs * CHUNK, zbuf.shape[0])], zsem)
        tail.start()
        tail.wait()

    pltpu.make_async_copy(x1f_ref, xs_hbm.at[pl.ds(0, tm * CHUNK)], sem).wait()
    pltpu.make_async_copy(x1f_ref, xs_hbm.at[pl.ds(0, tm * CHUNK)], sem).wait()


def _ple(pos0, pos1, x1_flat, pe_b, gw_b, gb, pp_b, alpha):
    ta = pos0.shape[0]
    dm = gw_b.shape[0]
    tm = TM_ROW
    tr = TR
    n_pairs = TOP_K * ta
    row = lambda w: pl.BlockSpec((tm, w), lambda i, *_: (i, 0))
    const = lambda a: pl.BlockSpec(a.shape, lambda i, *_: (0,) * a.ndim, pipeline_mode=pl.Buffered(1))
    grid_spec = pltpu.PrefetchScalarGridSpec(
        num_scalar_prefetch=2,
        grid=(ta // tm,),
        in_specs=[pl.BlockSpec((tm * CHUNK, LANE), lambda i, *_: (i, 0)), row(pe_b.shape[1]),
                  const(gw_b), const(gb), const(pp_b)],
        out_specs=[row(dm), pl.BlockSpec(memory_space=pl.ANY)],
        scratch_shapes=[pltpu.VMEM((tr * CHUNK, LANE), F32), pltpu.SemaphoreType.DMA(()),
                        pltpu.SemaphoreType.DMA(())],
    )
    return pl.pallas_call(
        functools.partial(_ple_body, alpha=alpha, n_pairs=n_pairs),
        grid_spec=grid_spec,
        out_shape=[jax.ShapeDtypeStruct((ta, dm), F32),
                   jax.ShapeDtypeStruct(((n_pairs + tr) * CHUNK, LANE), F32)],
        compiler_params=_cparams(("arbitrary",)),
        name="ple_dispatch",
    )(pos0, pos1, x1_flat, pe_b, gw_b, gb, pp_b)


def _expert_body(first_ref, ntile_ref, start_ref, nv_ref, p0_ref, p1_ref, x_hbm, wg_ref, wu_ref, wd_ref, y_hbm,
                 xbuf, ybuf, wgb, wub, wdb, dst_ref, gsem, ssem, *, tr, ta):
    e = pl.program_id(0)
    ne = pl.num_programs(0)
    nt = first_ref[ne - 1] + ntile_ref[ne - 1]
    UNROLL = 8

    def get_tile(jj, sl):
        rows = pl.ds(pl.multiple_of(start_ref[jj] * CHUNK, CHUNK), tr * CHUNK)
        return pltpu.make_async_copy(x_hbm.at[rows], xbuf.at[sl], gsem.at[sl])

    def put_row(jj, sl, r):
        stage = pl.ds(pl.multiple_of(r * CHUNK, CHUNK), CHUNK)
        dst = pl.ds(pl.multiple_of(dst_ref[start_ref[jj] + r] * CHUNK, CHUNK), CHUNK)
        return pltpu.make_async_copy(ybuf.at[sl, stage], y_hbm.at[dst], ssem.at[sl])

    def put_rows(jj, sl):
        def block(rb, _):
            for q in range(UNROLL):
                put_row(jj, sl, rb * UNROLL + q).start()
            return 0

        nv = nv_ref[jj]
        full = nv // UNROLL
        lax.fori_loop(0, full, block, 0)
        for q in range(UNROLL - 1):
            @pl.when(full * UNROLL + q < nv)
            def _():
                put_row(jj, sl, full * UNROLL + q).start()

    def put_wait(jj, sl):
        rows = pl.ds(0, nv_ref[jj] * CHUNK)
        pltpu.make_async_copy(ybuf.at[sl, rows], y_hbm.at[rows], ssem.at[sl]).wait()

    @pl.when(e == 0)
    def _():
        def fill(t, _):
            dst_ref[p0_ref[t]] = t
            dst_ref[p1_ref[t]] = ta + t
            return 0

        lax.fori_loop(0, ta, fill, 0, unroll=8)
        get_tile(0, 0).start()

    @pl.when(ntile_ref[e] > 0)
    def _():
        wgb[...] = wg_ref[0].astype(BF16)
        wub[...] = wu_ref[0].astype(BF16)
        wdb[...] = wd_ref[0].astype(BF16)

        def tile(jj, _):
            j = first_ref[e] + jj
            slot = j % 2
            get_tile(j, slot).wait()

            @pl.when(j + 1 < nt)
            def _():
                get_tile(j + 1, 1 - slot).start()

            x = _load_flat(xbuf, tr, (slot,)).astype(BF16)
            hg = jnp.dot(x, wgb[...], preferred_element_type=F32)
            hu = jnp.dot(x, wub[...], preferred_element_type=F32)
            h = (hg * jax.nn.sigmoid(hg) * hu).astype(BF16)
            y = jnp.dot(h, wdb[...], preferred_element_type=F32)

            @pl.when(j >= 2)
            def _():
                put_wait(j - 2, slot)

            _store_flat(ybuf, y, (slot,))
            put_rows(j, slot)
            return 0

        lax.fori_loop(0, ntile_ref[e], tile, 0)

    @pl.when(e == ne - 1)
    def _():
        last = nt - 1
        put_wait(last, last % 2)

        @pl.when(nt >= 2)
        def _():
            put_wait(last - 1, 1 - last % 2)


def _experts(first_tile, n_tile, tile_start, tile_valid, pos0, pos1, xs_flat, w_gate, w_up, w_down):
    ta = pos0.shape[0]
    ne, dm, ff = w_gate.shape
    tr = TR
    assert dm == CHUNK * LANE and first_tile.shape == (ne,)
    sp = lambda e, *_: (e, 0, 0)
    any_spec = pl.BlockSpec(memory_space=pl.ANY)
    grid_spec = pltpu.PrefetchScalarGridSpec(
        num_scalar_prefetch=6,
        grid=(ne,),
        in_specs=[any_spec,
                  pl.BlockSpec((1, dm, ff), sp), pl.BlockSpec((1, dm, ff), sp), pl.BlockSpec((1, ff, dm), sp)],
        out_specs=any_spec,
        scratch_shapes=[pltpu.VMEM((2, tr * CHUNK, LANE), F32), pltpu.VMEM((2, tr * CHUNK, LANE), F32),
                        pltpu.VMEM((dm, ff), BF16), pltpu.VMEM((dm, ff), BF16), pltpu.VMEM((ff, dm), BF16),
                        pltpu.SMEM((TOP_K * ta,), I32),
                        pltpu.SemaphoreType.DMA((2,)), pltpu.SemaphoreType.DMA((2,))],
    )
    return pl.pallas_call(
        functools.partial(_expert_body, tr=tr, ta=ta),
        grid_spec=grid_spec,
        out_shape=jax.ShapeDtypeStruct((TOP_K * ta * CHUNK, LANE), F32),
        compiler_params=_cparams(("arbitrary",)),
        name="experts",
    )(first_tile, n_tile, tile_start, tile_valid, pos0, pos1, xs_flat, w_gate, w_up, w_down)


def _final_body(r_ref, y0_ref, y1_ref, wt_ref, g_ref, b_ref, yp_ref, ys_ref, *, n_prompt_tiles):
    i = pl.program_id(0)
    wt = wt_ref[...]
    tm = r_ref.shape[0]
    x = r_ref[...] + wt[:, 0:1] * _load_flat(y0_ref, tm) + wt[:, 1:2] * _load_flat(y1_ref, tm)
    y = _layer_norm(x, g_ref[...], b_ref[...])

    @pl.when(i < n_prompt_tiles)
    def _():
        yp_ref[...] = y

    @pl.when(i >= n_prompt_tiles)
    def _():
        ys_ref[...] = y


def _final(r, y01_flat, wts, ln_g, ln_b, t):
    ta, dm = r.shape
    tm = TM_ROW
    npt = t // tm
    nta = ta // tm
    row = lambda w: pl.BlockSpec((tm, w), lambda i: (i, 0))
    return pl.pallas_call(
        functools.partial(_final_body, n_prompt_tiles=npt),
        grid=(nta,),
        in_specs=[row(dm),
                  pl.BlockSpec((tm * CHUNK, LANE), lambda i: (i, 0)),
                  pl.BlockSpec((tm * CHUNK, LANE), lambda i: (nta + i, 0)),
                  row(LANE), _const_spec(ln_g.shape), _const_spec(ln_b.shape)],
        out_specs=[pl.BlockSpec((tm, dm), lambda i: (jnp.minimum(i, npt - 1), 0)),
                   pl.BlockSpec((tm, dm), lambda i: (0, 0))],
        out_shape=[jax.ShapeDtypeStruct((t, dm), F32), jax.ShapeDtypeStruct((tm, dm), F32)],
        compiler_params=_cparams(("arbitrary",)),
        name="final",
    )(r, y01_flat, y01_flat, wts, ln_g, ln_b)


def _layer(layer, x_p, x_s, pe_p, pe_s, cache_k, cache_v, state, page_table, w):
    t, dm = x_p.shape
    nb = x_s.shape[0]
    wn = N_HEADS * HEAD_W
    ta = t + ROW_PAD
    assert nb <= ROW_PAD and t % ROW_PAD == 0
    lam0 = _lambda_init(layer)
    depth_total = w["depth"]
    alpha = (2.0 * depth_total) ** 0.25
    page = cache_k.shape[1]
    start_pos = page_table.shape[1] * page

    pad = ta - t - nb
    xb = jnp.concatenate([x_p.astype(BF16), x_s.astype(BF16), jnp.zeros((pad, dm), BF16)], axis=0)
    u, qb, k, kb, v, vb, gates = _inproj(xb, w["w_in"].astype(BF16))

    lamv = jnp.concatenate([w["lam_q1"], w["lam_k1"], w["lam_q2"], w["lam_k2"]], axis=0)
    subln_g = w["subln_g"]

    o_p = _prompt_attn(qb, kb, vb, lamv, subln_g, t, lam0)
    kt_pool = jnp.transpose(cache_k, (0, 2, 3, 4, 1)).reshape(cache_k.shape[0], wn, page)
    v_pool = cache_v.reshape(cache_v.shape[0] * page * N_HEADS, HEAD_W)
    o_s = _decode_attn(page_table, lamv, subln_g, qb[t:t + nb], k[t:t + nb], v[t:t + nb],
                       kt_pool, v_pool, lam0)
    o_all = jnp.concatenate([o_p, o_s.astype(BF16), jnp.zeros((pad, wn), BF16)], axis=0)

    d_p = _pool_prompt(u, t)
    d_s = _pool_sample(u[t:t + nb], state, start_pos)
    d_all = jnp.concatenate([d_p, d_s, jnp.zeros((pad, d_p.shape[1]), BF16)], axis=0)

    x1_flat, logits = _mix(d_all, o_all, gates, x_p, x_s, w["pool_w"].astype(BF16), w["pool_scale"],
                           w["w_branch_pool"].astype(BF16), w["w_branch_attn"].astype(BF16),
                           w["w_out"].astype(BF16), w["ln1_g"], w["ln1_b"], w["router_w"], w["router_b"], alpha)

    eid, wts = _route(logits)
    pos, meta, emeta = _slots(eid)
    n_row_tiles = (TOP_K * ta) // TR + N_EXPERTS
    assert n_row_tiles <= LANE
    pos0, pos1 = pos[:, 0], pos[:, 1]

    pe_b = jnp.concatenate([pe_p.astype(BF16), pe_s.astype(BF16), jnp.zeros((pad, pe_p.shape[1]), BF16)], axis=0)
    r, xs_flat = _ple(pos0, pos1, x1_flat, pe_b, w["ple_gate_w"].astype(BF16), w["ple_gate_b"],
                      w["ple_proj"].astype(BF16), alpha)
    y01_flat = _experts(emeta[0, :N_EXPERTS], emeta[1, :N_EXPERTS], meta[:n_row_tiles, 2], meta[:n_row_tiles, 1],
                        pos0, pos1, xs_flat, w["expert_gate"], w["expert_up"], w["expert_down"])
    y_p, y_s = _final(r, y01_flat, wts, w["ln2_g"], w["ln2_b"], t)
    return y_p, y_s[:nb], k, v, u


def kernel(x_prompt, x_sample, p_prompt, p_sample, cache_k, cache_v, state_pool, page_table, w_in, pool_w, pool_scale, lam_q1, lam_k1, lam_q2, lam_k2, subln_g, w_branch_pool, w_branch_attn, w_out, ln1_g, ln1_b, router_group_w, router_group_b, router_expert_w, router_expert_b, expert_gate, expert_up, expert_down, ple_proj, ple_gate_w, ple_gate_b, ln2_g, ln2_b):
    depth = w_in.shape[0]
    bp, t, dm = x_prompt.shape
    nb, ts, _ = x_sample.shape
    assert bp == 1 and ts == 1 and dm // 256 == N_HEADS
    assert dm % CHUNK == 0 and dm // CHUNK == LANE
    hp = x_prompt.reshape(t, dm)
    hs = x_sample.reshape(nb, dm)
    outs = {n: [] for n in ("kp", "vp", "sp", "ks", "vs", "ss")}
    for l in range(depth):
        wr = jnp.concatenate([router_group_w[l],
                              jnp.transpose(router_expert_w[l], (1, 0, 2)).reshape(dm, N_EXPERTS)], axis=1)
        wr = jnp.pad(wr, ((0, 0), (0, LANE - wr.shape[1])))
        wr_hi = wr.astype(BF16)
        wr = jnp.stack([wr_hi, (wr - wr_hi.astype(F32)).astype(BF16)])
        br = jnp.pad(jnp.concatenate([router_group_b[l], router_expert_b[l].reshape(-1)]),
                     (0, LANE - N_GROUPS - N_EXPERTS)).reshape(1, LANE)
        row = lambda a: a[l].reshape(1, -1)
        w = dict(depth=depth, w_in=w_in[l], pool_w=pool_w[l], pool_scale=row(pool_scale),
                 lam_q1=row(lam_q1), lam_k1=row(lam_k1), lam_q2=row(lam_q2), lam_k2=row(lam_k2),
                 subln_g=row(subln_g), w_branch_pool=w_branch_pool[l], w_branch_attn=w_branch_attn[l],
                 w_out=w_out[l], ln1_g=row(ln1_g), ln1_b=row(ln1_b), router_w=wr, router_b=br,
                 expert_gate=expert_gate[l], expert_up=expert_up[l], expert_down=expert_down[l],
                 ple_proj=ple_proj[l], ple_gate_w=ple_gate_w[l], ple_gate_b=row(ple_gate_b),
                 ln2_g=row(ln2_g), ln2_b=row(ln2_b))
        state = state_pool[l]
        hp, hs, k, v, u = _layer(l, hp, hs, p_prompt[l, 0], p_sample[l, :, 0], cache_k[l], cache_v[l],
                                 state, page_table, w)
        outs["kp"].append(k[:t].reshape(1, t, N_HEADS, 2, HEAD_DIM))
        outs["vp"].append(v[:t].reshape(1, t, N_HEADS, HEAD_W))
        outs["sp"].append(u[t - POOL_STATE:t].reshape(1, POOL_STATE, -1))
        outs["ks"].append(k[t:t + nb].reshape(nb, 1, N_HEADS, 2, HEAD_DIM))
        outs["vs"].append(v[t:t + nb].reshape(nb, 1, N_HEADS, HEAD_W))
        outs["ss"].append(jnp.concatenate([state[:, 1:], u[t:t + nb][:, None, :]], axis=1))
    st = lambda n: jnp.stack(outs[n])
    return (hp.reshape(1, t, dm), hs.reshape(nb, 1, dm), st("kp"), st("vp"), st("sp"),
            st("ks"), st("vs"), st("ss"))
```

```python
import functools
import math

import jax
import jax.numpy as jnp
from jax import lax
from jax.experimental import pallas as pl
from jax.experimental.pallas import tpu as pltpu

F32 = jnp.float32
BF16 = jnp.bfloat16
I32 = jnp.int32

N_HEADS = 8
HEAD_DIM = 64
HEAD_W = 2 * HEAD_DIM
ATTN_SCALE = HEAD_DIM ** -0.5
LOG2E = math.log2(math.e)
SUBLN_EPS = 1e-5
LN_EPS = 1e-5
POOL_WINDOWS = (2, 4, 8, 16)
POOL_STATE = max(POOL_WINDOWS) - 1
N_GROUPS = 4
EXPERTS_PER_GROUP = 8
N_EXPERTS = N_GROUPS * EXPERTS_PER_GROUP
TOP_K = 2

LANE = 128
SUBLANE = 8
VMEM_LIMIT = 56 * 1024 * 1024

ROW_PAD = 256
TM_INPROJ = 768
TM_ROW = 256
TM_SORT = 768
TQ = 512
TK = 512
HEADS_PER_STEP = 1
TR = 256
HALO = 16
CHUNK = 16
DECODE_GROUP = 4
DECODE_SLOTS = 12


def _lambda_init(layer):
    return 0.8 - 0.6 * math.exp(-0.3 * layer)


def _cparams(sem, vmem=VMEM_LIMIT):
    return pltpu.CompilerParams(dimension_semantics=sem, vmem_limit_bytes=vmem)


def _const_spec(shape):
    nd = len(shape)
    return pl.BlockSpec(shape, lambda *_: (0,) * nd, pipeline_mode=pl.Buffered(1))


def _inproj_body(x_ref, w_ref, u_ref, q_ref, kb_ref, kt_ref, ks_ref, vb_ref, vf_ref, vs_ref, g_ref, *, s_off):
    j = pl.program_id(0)
    tm = x_ref.shape[0]
    ns = ks_ref.shape[0]

    def z():
        return jnp.dot(x_ref[...], w_ref[...], preferred_element_type=F32)

    @pl.when(j == 0)
    def _():
        u_ref[...] = z()

    @pl.when(j == 1)
    def _():
        q_ref[...] = (z() * (ATTN_SCALE * LOG2E)).astype(BF16)

    @pl.when(j == 2)
    def _():
        kz = z()
        kb_ref[...] = kz.astype(BF16)
        kt_ref[...] = kz.T
        ks_ref[...] = kz[s_off:s_off + ns]

    @pl.when(j == 3)
    def _():
        vz = z()
        vb_ref[...] = vz.astype(BF16)
        for h in range(N_HEADS):
            vf_ref[pl.ds(h, tm, stride=N_HEADS), :] = vz[:, h * HEAD_W:(h + 1) * HEAD_W]
        vs_ref[...] = vz[s_off:s_off + ns]

    @pl.when(j >= 4)
    def _():
        g_ref[...] = jax.nn.sigmoid(z()).astype(BF16)


def _inproj(xb, w_in_b, t, ns):
    ta, d = xb.shape
    n_in = w_in_b.shape[1]
    wn = N_HEADS * HEAD_W
    assert n_in == 4 * wn + 2 * d and (2 * d) % wn == 0
    tm = TM_INPROJ
    assert ta % tm == 0
    ni = ta // tm
    nj = n_in // wn
    s_off = t - (ni - 1) * tm
    assert 0 <= s_off and s_off + ns <= tm and s_off % SUBLANE == 0 and ns % SUBLANE == 0

    def at(jo):
        return lambda j, i: jnp.where(j < jo, 0, jnp.where(j == jo, i, ni - 1))

    blk = lambda jo: pl.BlockSpec((tm, wn), lambda j, i: (at(jo)(j, i), 0))
    last = lambda: pl.BlockSpec((ns, wn), lambda j, i: (0, 0))
    g_spec = pl.BlockSpec((tm, wn), lambda j, i: (jnp.where(j < 4, 0, i), jnp.maximum(j - 4, 0)))
    return pl.pallas_call(
        functools.partial(_inproj_body, s_off=s_off),
        grid=(nj, ni),
        in_specs=[pl.BlockSpec((tm, d), lambda j, i: (i, 0)),
                  pl.BlockSpec((d, wn), lambda j, i: (0, j))],
        out_specs=[blk(0), blk(1),
                   blk(2), pl.BlockSpec((wn, tm), lambda j, i: (0, at(2)(j, i))), last(),
                   blk(3), pl.BlockSpec((tm * N_HEADS, HEAD_W), lambda j, i: (at(3)(j, i), 0)), last(),
                   g_spec],
        out_shape=[jax.ShapeDtypeStruct((ta, wn), F32),
                   jax.ShapeDtypeStruct((ta, wn), BF16),
                   jax.ShapeDtypeStruct((ta, wn), BF16),
                   jax.ShapeDtypeStruct((wn, t), F32),
                   jax.ShapeDtypeStruct((ns, wn), F32),
                   jax.ShapeDtypeStruct((ta, wn), BF16),
                   jax.ShapeDtypeStruct((t * N_HEADS, HEAD_W), F32),
                   jax.ShapeDtypeStruct((ns, wn), F32),
                   jax.ShapeDtypeStruct((ta, 2 * d), BF16)],
        compiler_params=_cparams(("arbitrary", "arbitrary")),
        name="inproj",
    )(xb, w_in_b)


def _lam_value(lamv_ref, lam0):
    lv = lamv_ref[...]
    a = jnp.sum(lv[0:1] * lv[1:2], axis=1, keepdims=True)
    b = jnp.sum(lv[2:3] * lv[3:4], axis=1, keepdims=True)
    return jnp.exp(a) - jnp.exp(b) + lam0


def _subln(o, g, lam0):
    ms = jnp.mean(o * o, axis=-1, keepdims=True)
    return o * lax.rsqrt(ms + SUBLN_EPS) * g * (1.0 - lam0)


def _prompt_attn_body(lamv_ref, gcol_ref, q_ref, k_ref, v_ref, o_ref, vt_ref, q2t_ref, s_even, s_odd,
                      m_ref, l_ref, acc_ref, *, tq, tk, hp, lam0):
    i = pl.program_id(1)
    nkv = v_ref.shape[0] // tk
    heads = [pl.ds(hh * HEAD_W, HEAD_W) for hh in range(hp)]

    @pl.when(i == 0)
    def _():
        def tr(jb, _):
            for hh in range(hp):
                blk = v_ref[pl.ds(pl.multiple_of(jb * tk, tk), tk), heads[hh]].astype(F32)
                vt_ref[hh, jb] = blk.T.astype(BF16)
            return 0
        lax.fori_loop(0, nkv, tr, 0)

    drow = lax.broadcasted_iota(I32, (HEAD_W, tq), 0)
    for hh in range(hp):
        qt = q_ref[:, heads[hh]].astype(F32).T
        q2t_ref[hh] = jnp.concatenate([jnp.where(drow < HEAD_DIM, qt, 0.0),
                                       jnp.where(drow >= HEAD_DIM, qt, 0.0)], axis=1).astype(BF16)
    m_ref[...] = jnp.full(m_ref.shape, -jnp.inf, F32)
    l_ref[...] = jnp.zeros(l_ref.shape, F32)
    acc_ref[...] = jnp.zeros(acc_ref.shape, F32)
    n_full = (i * tq) // tk

    def scores(j, s_ref):
        rows = pl.ds(pl.multiple_of(j * tk, tk), tk)
        for hh in range(hp):
            s_ref[hh] = jnp.dot(k_ref[rows, heads[hh]], q2t_ref[hh], preferred_element_type=F32)

    def absorb(j, s_ref, masked):
        for hh in range(hp):
            s = s_ref[hh]
            if masked:
                kpos = n_full * tk + lax.broadcasted_iota(I32, (tk, 2 * tq), 0)
                qcol = lax.broadcasted_iota(I32, (tk, 2 * tq), 1)
                qpos = i * tq + jnp.where(qcol >= tq, qcol - tq, qcol)
                s = jnp.where(kpos <= qpos, s, -jnp.inf)
            m = m_ref[hh]
            m_new = jnp.maximum(m, jnp.max(s, axis=0, keepdims=True))
            alpha = jnp.exp2(m - m_new)
            p = jnp.exp2(s - m_new)
            l_ref[hh] = alpha * l_ref[hh] + jnp.sum(p, axis=0, keepdims=True)
            acc_ref[hh] = alpha * acc_ref[hh] + jnp.dot(vt_ref[hh, j], p.astype(BF16), preferred_element_type=F32)
            m_ref[hh] = m_new

    def by_parity(j, fn):
        @pl.when(j % 2 == 0)
        def _():
            fn(s_even, s_odd)

        @pl.when(j % 2 == 1)
        def _():
            fn(s_odd, s_even)

    scores(0, s_even)

    def body(j, _):
        def stage(cur, nxt):
            scores(j + 1, nxt)
            absorb(j, cur, False)
        by_parity(j, stage)
        return 0

    lax.fori_loop(0, n_full, body, 0)
    by_parity(n_full, lambda cur, nxt: absorb(n_full, cur, True))

    lam = _lam_value(lamv_ref, lam0)
    for hh in range(hp):
        on = acc_ref[hh] / l_ref[hh]
        ot = on[:, :tq] - lam * on[:, tq:]
        ms = jnp.mean(ot * ot, axis=0, keepdims=True)
        ot = ot * lax.rsqrt(ms + SUBLN_EPS) * gcol_ref[...] * (1.0 - lam0)
        o_ref[:, heads[hh]] = ot.T.astype(o_ref.dtype)


def _prompt_attn(qb, kb, vb, lamv, subln_g, t, lam0):
    tq, tk, hp = TQ, TK, HEADS_PER_STEP
    assert t % tk == 0 and tk % tq == 0 and N_HEADS % hp == 0
    kern = functools.partial(_prompt_attn_body, tq=tq, tk=tk, hp=hp, lam0=lam0)
    gcol = subln_g.reshape(HEAD_W, 1)
    return pl.pallas_call(
        kern,
        grid=(N_HEADS // hp, t // tq),
        in_specs=[_const_spec(lamv.shape), _const_spec(gcol.shape),
                  pl.BlockSpec((tq, hp * HEAD_W), lambda h, i: (i, h)),
                  pl.BlockSpec((t, hp * HEAD_W), lambda h, i: (0, h)),
                  pl.BlockSpec((t, hp * HEAD_W), lambda h, i: (0, h))],
        out_specs=pl.BlockSpec((tq, hp * HEAD_W), lambda h, i: (i, h)),
        out_shape=jax.ShapeDtypeStruct((t, N_HEADS * HEAD_W), BF16),
        scratch_shapes=[pltpu.VMEM((hp, t // tk, HEAD_W, tk), BF16),
                        pltpu.VMEM((hp, HEAD_W, 2 * tq), BF16),
                        pltpu.VMEM((hp, tk, 2 * tq), F32), pltpu.VMEM((hp, tk, 2 * tq), F32),
                        pltpu.VMEM((hp, 1, 2 * tq), F32), pltpu.VMEM((hp, 1, 2 * tq), F32),
                        pltpu.VMEM((hp, HEAD_W, 2 * tq), F32)],
        compiler_params=_cparams(("arbitrary", "arbitrary")),
        name="prompt_attn",
    )(lamv, gcol, qb, kb, vb)


def _decode_attn_body(pt_ref, lamv_ref, g_ref, q_ref, qc_ref, kn_ref, vn_ref, kt_hbm, v_hbm, o_ref,
                      kbuf, vbuf, sem, *, n_pages, page, n_slots, group, lam0):
    b = pl.program_id(0)
    nb = pl.num_programs(0)
    nr = 2 * N_HEADS
    wn = N_HEADS * HEAD_W
    vrows = page * N_HEADS
    ahead = n_slots - group
    total = nb * n_pages
    assert n_pages % group == 0 and n_slots % group == 0 and ahead >= group

    def copies(g):
        slot = g % n_slots
        pg = pt_ref[g // n_pages, g % n_pages]
        ck = pltpu.make_async_copy(kt_hbm.at[pg], kbuf.at[slot], sem.at[0, slot])
        cv = pltpu.make_async_copy(v_hbm.at[pl.ds(pl.multiple_of(pg * vrows, vrows), vrows)],
                                   vbuf.at[slot], sem.at[1, slot])
        return ck, cv

    def start(g):
        for c in copies(g):
            c.start()

    @pl.when(b == 0)
    def _():
        for g in range(ahead):
            start(g)

    rr = lax.broadcasted_iota(I32, (nr, wn), 0)
    cc = lax.broadcasted_iota(I32, (nr, wn), 1)
    qrow = q_ref[0]
    qbd = jnp.where(cc // HEAD_DIM == rr, jnp.broadcast_to(qrow.astype(F32), (nr, wn)), 0.0).astype(BF16)
    rhead = lax.broadcasted_iota(I32, (nr, HEAD_W), 0) // 2
    qcol = jnp.broadcast_to(qc_ref[0], (wn, page))
    tok = lax.broadcasted_iota(I32, (page, vrows), 0)
    vrow = lax.broadcasted_iota(I32, (page, vrows), 1)
    spread = jnp.where(vrow // N_HEADS == tok, 1.0, 0.0).astype(BF16)
    own_head = lax.broadcasted_iota(I32, (nr, vrows), 1) % N_HEADS == lax.broadcasted_iota(I32, (nr, vrows), 0) // 2

    def body(jg, carry):
        m, l, acc = carry
        g0 = b * n_pages + jg * group
        slots = []
        ss = []
        for a in range(group):
            g = g0 + a

            @pl.when(g + ahead < total)
            def _():
                start(g + ahead)

            for c in copies(g):
                c.wait()
            slot = g % n_slots
            slots.append(slot)
            prod = kbuf[slot] * qcol
            ss.append(jnp.sum(prod.reshape(nr, HEAD_DIM, page), axis=1))
        m_new = m
        for s in ss:
            m_new = jnp.maximum(m_new, jnp.max(s, axis=1, keepdims=True))
        alpha = jnp.exp2(m - m_new)
        l = alpha * l
        acc = alpha * acc
        for s, slot in zip(ss, slots):
            p = jnp.exp2(s - m_new)
            l = l + jnp.sum(p, axis=1, keepdims=True)
            pe = jnp.dot(p.astype(BF16), spread, preferred_element_type=F32)
            pe = jnp.where(own_head, pe, 0.0).astype(BF16)
            acc = acc + jnp.dot(pe, vbuf[slot].astype(BF16), preferred_element_type=F32)
        return m_new, l, acc

    init = (jnp.full((nr, 1), -jnp.inf, F32), jnp.zeros((nr, 1), F32), jnp.zeros((nr, HEAD_W), F32))
    m, l, acc = lax.fori_loop(0, n_pages // group, body, init)

    kn = jnp.broadcast_to(kn_ref[0], (nr, wn))
    s_new = jnp.sum(qbd.astype(F32) * kn, axis=1, keepdims=True)
    vn = vn_ref[0]
    vnew = jnp.zeros((nr, HEAD_W), F32)
    for h in range(N_HEADS):
        vnew = vnew + jnp.where(rhead == h, jnp.broadcast_to(vn[:, h * HEAD_W:(h + 1) * HEAD_W], (nr, HEAD_W)), 0.0)
    m_new = jnp.maximum(m, s_new)
    alpha = jnp.exp2(m - m_new)
    p_new = jnp.exp2(s_new - m_new)
    l = alpha * l + p_new
    acc = alpha * acc + p_new * vnew
    on = acc / l
    lam = _lam_value(lamv_ref, lam0)
    o = on - lam * pltpu.roll(on, nr - 1, 0)
    o_ref[0] = _subln(o, g_ref[...], lam0)


def _decode_attn(page_table, lamv, subln_g, q_s, k_s, v_s, kt_pool, v_pool, lam0):
    nb, n_pages = page_table.shape
    wn = N_HEADS * HEAD_W
    page = kt_pool.shape[2]
    nr = 2 * N_HEADS
    n_slots = DECODE_SLOTS
    kern = functools.partial(_decode_attn_body, n_pages=n_pages, page=page, n_slots=n_slots,
                             group=DECODE_GROUP, lam0=lam0)
    grid_spec = pltpu.PrefetchScalarGridSpec(
        num_scalar_prefetch=1,
        grid=(nb,),
        in_specs=[pl.BlockSpec(lamv.shape, lambda b, pt: (0, 0)),
                  pl.BlockSpec(subln_g.shape, lambda b, pt: (0, 0)),
                  pl.BlockSpec((1, 1, wn), lambda b, pt: (b, 0, 0)),
                  pl.BlockSpec((1, wn, 1), lambda b, pt: (b, 0, 0)),
                  pl.BlockSpec((1, 1, wn), lambda b, pt: (b, 0, 0)),
                  pl.BlockSpec((1, 1, wn), lambda b, pt: (b, 0, 0)),
                  pl.BlockSpec(memory_space=pl.ANY),
                  pl.BlockSpec(memory_space=pl.ANY)],
        out_specs=pl.BlockSpec((1, nr, HEAD_W), lambda b, pt: (b, 0, 0)),
        scratch_shapes=[pltpu.VMEM((n_slots, wn, page), F32),
                        pltpu.VMEM((n_slots, page * N_HEADS, HEAD_W), F32),
                        pltpu.SemaphoreType.DMA((2, n_slots))],
    )
    out = pl.pallas_call(
        kern,
        grid_spec=grid_spec,
        out_shape=jax.ShapeDtypeStruct((nb, nr, HEAD_W), F32),
        compiler_params=_cparams(("arbitrary",)),
        name="decode_attn",
    )(page_table, lamv, subln_g, q_s.reshape(nb, 1, wn), q_s.astype(F32).reshape(nb, wn, 1),
      k_s.reshape(nb, 1, wn), v_s.reshape(nb, 1, wn), kt_pool, v_pool)
    return out[:, 0::2, :].reshape(nb, wn)


def _pool_prompt_body(u_ref, halo_ref, d_ref, ext_ref, *, tm, gw):
    i = pl.program_id(0)
    ext_ref[pl.ds(HALO, tm), :] = u_ref[...]

    @pl.when(i == 0)
    def _():
        ext_ref[pl.ds(0, HALO), :] = jnp.zeros((HALO, u_ref.shape[1]), F32)

    @pl.when(i > 0)
    def _():
        ext_ref[pl.ds(0, HALO), :] = halo_ref[...]

    pos = i * tm + lax.broadcasted_iota(I32, (tm, 1), 0)
    for g, w in enumerate(POOL_WINDOWS):
        cols = pl.ds(g * gw, gw)
        tok = ext_ref[pl.ds(HALO, tm), cols]
        s = tok
        for back in range(1, w):
            s = s + ext_ref[pl.ds(HALO - back, tm), cols]
        inv = 1.0 / jnp.minimum(pos + 1, w).astype(F32)
        d_ref[:, cols] = (s * inv - tok).astype(d_ref.dtype)


def _pool_prompt(u_all, t):
    wp = u_all.shape[1]
    gw = wp // len(POOL_WINDOWS)
    tm = TM_ROW
    assert t % tm == 0 and tm % HALO == 0 and gw % LANE == 0
    kern = functools.partial(_pool_prompt_body, tm=tm, gw=gw)
    return pl.pallas_call(
        kern,
        grid=(t // tm,),
        in_specs=[pl.BlockSpec((tm, wp), lambda i: (i, 0)),
                  pl.BlockSpec((HALO, wp), lambda i: (jnp.maximum(i * (tm // HALO) - 1, 0), 0))],
        out_specs=pl.BlockSpec((tm, wp), lambda i: (i, 0)),
        out_shape=jax.ShapeDtypeStruct((t, wp), BF16),
        scratch_shapes=[pltpu.VMEM((tm + HALO, wp), F32)],
        compiler_params=_cparams(("arbitrary",)),
        name="pool_prompt",
    )(u_all, u_all)


def _pool_sample_body(u_ref, st_ref, d_ref, *, gw, start_pos):
    st = st_ref[...]
    u = u_ref[...]
    ridx = lax.broadcasted_iota(I32, st.shape, 1)
    for g, w in enumerate(POOL_WINDOWS):
        hist = jnp.sum(jnp.where(ridx >= POOL_STATE - (w - 1), st, 0.0), axis=1)
        cnt = float(min(start_pos + 1, w))
        d = (hist + u) * (1.0 / cnt) - u
        d_ref[:, g * gw:(g + 1) * gw] = d[:, g * gw:(g + 1) * gw].astype(d_ref.dtype)


def _pool_sample(u_s, state, start_pos):
    nb, wp = u_s.shape
    gw = wp // len(POOL_WINDOWS)
    kern = functools.partial(_pool_sample_body, gw=gw, start_pos=start_pos)
    return pl.pallas_call(
        kern,
        grid=(1,),
        in_specs=[_const_spec(u_s.shape), _const_spec(state.shape)],
        out_specs=_const_spec((nb, wp)),
        out_shape=jax.ShapeDtypeStruct((nb, wp), BF16),
        compiler_params=_cparams(("arbitrary",)),
        name="pool_sample",
    )(u_s, state)


def _store_flat(flat_ref, x, lead=()):
    rows = x.shape[0]
    for c in range(CHUNK):
        flat_ref[(*lead, pl.ds(c, rows, stride=CHUNK), slice(None))] = x[:, c * LANE:(c + 1) * LANE]


def _load_flat(flat_ref, rows, lead=(), first=0):
    return jnp.concatenate(
        [flat_ref[(*lead, pl.ds(first * CHUNK + c, rows, stride=CHUNK), slice(None))] for c in range(CHUNK)],
        axis=1)


def _layer_norm(x, g, b):
    mu = jnp.mean(x, axis=-1, keepdims=True)
    xc = x - mu
    var = jnp.mean(xc * xc, axis=-1, keepdims=True)
    return xc * lax.rsqrt(var + LN_EPS) * g + b


def _mix_body(d_ref, o_ref, g_ref, xp_ref, xs_ref, pw_ref, ps_ref, wbp_ref, wba_ref, wo_ref,
              lg_ref, lb_ref, wr_ref, br_ref, x1f_ref, lo_ref, *, n_prompt_tiles, alpha):
    i = pl.program_id(0)
    tm, dm = xp_ref.shape
    ng = len(POOL_WINDOWS)
    gw = d_ref.shape[1] // ng
    ys = []
    for g in range(ng):
        ys.append(jnp.dot(d_ref[:, g * gw:(g + 1) * gw], pw_ref[g], preferred_element_type=F32))
    pool_y = (jnp.concatenate(ys, axis=1) * ps_ref[...]).astype(BF16)
    a = jnp.dot(pool_y, wbp_ref[...], preferred_element_type=F32)
    bb = jnp.dot(o_ref[...], wba_ref[...], preferred_element_type=F32)
    gates = g_ref[...].astype(F32)
    merged = (gates[:, :dm] * a + gates[:, dm:] * bb).astype(BF16)
    mix = jnp.dot(merged, wo_ref[...], preferred_element_type=F32)

    def finish(x):
        x1 = _layer_norm(alpha * x + mix, lg_ref[...], lb_ref[...])
        _store_flat(x1f_ref, x1)
        hi = x1.astype(BF16)
        lo = (x1 - hi.astype(F32)).astype(BF16)
        dot = functools.partial(jnp.dot, preferred_element_type=F32)
        lo_ref[...] = dot(hi, wr_ref[0]) + (dot(lo, wr_ref[0]) + dot(hi, wr_ref[1])) + br_ref[...]

    @pl.when(i < n_prompt_tiles)
    def _():
        finish(xp_ref[...])

    @pl.when(i >= n_prompt_tiles)
    def _():
        ns = xs_ref.shape[0]
        finish(jnp.concatenate([xs_ref[...], jnp.zeros((tm - ns, dm), F32)], axis=0))


def _mix(d_all, o_all, gates, x_p, x_s, pool_w_b, pool_scale, wbp, wba, wo, ln_g, ln_b, wr, br, alpha):
    ta = d_all.shape[0]
    t, dm = x_p.shape
    ns = x_s.shape[0]
    tm = TM_ROW
    assert ta % tm == 0 and t % tm == 0 and ns % SUBLANE == 0 and ns <= tm and ta == t + tm
    npt = t // tm
    wp = d_all.shape[1]
    wa = o_all.shape[1]
    row = lambda w: pl.BlockSpec((tm, w), lambda i: (i, 0))
    kern = functools.partial(_mix_body, n_prompt_tiles=npt, alpha=alpha)
    return pl.pallas_call(
        kern,
        grid=(ta // tm,),
        in_specs=[row(wp), row(wa), row(2 * dm),
                  pl.BlockSpec((tm, dm), lambda i: (jnp.minimum(i, npt - 1), 0)),
                  _const_spec(x_s.shape),
                  _const_spec(pool_w_b.shape), _const_spec(pool_scale.shape),
                  _const_spec(wbp.shape), _const_spec(wba.shape), _const_spec(wo.shape),
                  _const_spec(ln_g.shape), _const_spec(ln_b.shape),
                  _const_spec(wr.shape), _const_spec(br.shape)],
        out_specs=[pl.BlockSpec((tm * CHUNK, LANE), lambda i: (i, 0)), row(LANE)],
        out_shape=[jax.ShapeDtypeStruct((ta * CHUNK, LANE), F32),
                   jax.ShapeDtypeStruct((ta, LANE), F32)],
        compiler_params=_cparams(("arbitrary",)),
        name="branch_mix",
    )(d_all, o_all, gates, x_p, x_s, pool_w_b, pool_scale, wbp, wba, wo, ln_g, ln_b, wr, br)


def _route_body(lo_ref, eid_ref, wt_ref):
    lo = lo_ref[...]
    lane_i = lax.broadcasted_iota(I32, lo.shape, 1)
    lane = lane_i.astype(F32)
    ninf = -jnp.inf
    big = float(LANE)
    is_g = lane_i < N_GROUPS
    gl = jnp.where(is_g, lo, ninf)
    gmax = jnp.max(gl, axis=1, keepdims=True)
    gidx = jnp.min(jnp.where(gl == gmax, lane, big), axis=1, keepdims=True)
    gsum = jnp.sum(jnp.where(is_g, jnp.exp(gl - gmax), 0.0), axis=1, keepdims=True)
    gw = 1.0 / gsum
    lo_e = N_GROUPS + EXPERTS_PER_GROUP * gidx
    el = jnp.where(jnp.logical_and(lane >= lo_e, lane < lo_e + EXPERTS_PER_GROUP), lo, ninf)
    v1 = jnp.max(el, axis=1, keepdims=True)
    i1 = jnp.min(jnp.where(el == v1, lane, big), axis=1, keepdims=True)
    el2 = jnp.where(lane == i1, ninf, el)
    v2 = jnp.max(el2, axis=1, keepdims=True)
    i2 = jnp.min(jnp.where(el2 == v2, lane, big), axis=1, keepdims=True)
    e21 = jnp.exp(v2 - v1)
    w1 = gw / (1.0 + e21)
    w2 = gw * e21 / (1.0 + e21)
    eid = jnp.where(lane_i == 0, i1 - N_GROUPS, jnp.where(lane_i == 1, i2 - N_GROUPS, 0.0))
    eid_ref[...] = eid.astype(I32)
    wt_ref[...] = jnp.where(lane_i == 0, w1, jnp.where(lane_i == 1, w2, 0.0))


def _route(logits):
    ta = logits.shape[0]
    tm = TM_SORT
    assert ta % tm == 0
    row = pl.BlockSpec((tm, LANE), lambda i: (i, 0))
    return pl.pallas_call(
        _route_body,
        grid=(ta // tm,),
        in_specs=[row],
        out_specs=[row, row],
        out_shape=[jax.ShapeDtypeStruct((ta, LANE), I32), jax.ShapeDtypeStruct((ta, LANE), F32)],
        compiler_params=_cparams(("arbitrary",)),
        name="route",
    )(logits)


def _slots_body(eid_ref, pos_ref, meta_ref, emeta_ref, cnt_ref, run_ref, off_ref, *, tm, tr):
    ph = pl.program_id(0)
    i = pl.program_id(1)
    eid = eid_ref[...]
    lane = lax.broadcasted_iota(I32, (tm, LANE), 1)
    e0 = eid[:, 0:1]
    e1 = eid[:, 1:2]
    oh0 = (lane == e0).astype(F32)
    oh1 = (lane == e1).astype(F32)
    both = oh0 + oh1

    @pl.when(jnp.logical_and(ph == 0, i == 0))
    def _():
        cnt_ref[...] = jnp.zeros_like(cnt_ref)

    @pl.when(ph == 0)
    def _():
        cnt_ref[...] = cnt_ref[...] + jnp.sum(both, axis=0, keepdims=True)

    @pl.when(jnp.logical_and(ph == 1, i == 0))
    def _():
        cnt = cnt_ref[...]
        tiles = jnp.floor((cnt + (tr - 1)) * (1.0 / tr))
        a = lax.broadcasted_iota(I32, (LANE, LANE), 0)
        b = lax.broadcasted_iota(I32, (LANE, LANE), 1)
        upper = (a < b).astype(BF16)
        prefix = lambda v: jnp.dot(jnp.broadcast_to(v, (SUBLANE, LANE)).astype(BF16), upper,
                                   preferred_element_type=F32)[0:1]
        off_tiles = prefix(tiles)
        cnt_hi = jnp.floor(cnt * (1.0 / 256.0))
        off_rows = 256.0 * prefix(cnt_hi) + prefix(cnt - 256.0 * cnt_hi)
        off_ref[...] = off_rows
        run_ref[...] = jnp.zeros_like(run_ref)
        end_tiles = off_tiles + tiles
        lane1 = lax.broadcasted_iota(I32, (LANE, LANE), 1)
        jrow = lax.broadcasted_iota(I32, (LANE, LANE), 0).astype(F32)
        done = jnp.logical_and(lane1 < N_EXPERTS, jnp.broadcast_to(end_tiles, (LANE, LANE)) <= jrow)
        te = jnp.sum(done.astype(F32), axis=1, keepdims=True)
        owns = jnp.logical_and(jnp.broadcast_to(off_tiles, (LANE, LANE)) <= jrow,
                               jrow < jnp.broadcast_to(end_tiles, (LANE, LANE)))
        left = jnp.broadcast_to(cnt, (LANE, LANE)) - (jrow - jnp.broadcast_to(off_tiles, (LANE, LANE))) * tr
        nv = jnp.sum(jnp.where(owns, jnp.minimum(left, float(tr)), 0.0), axis=1, keepdims=True)
        first = jnp.broadcast_to(off_rows, (LANE, LANE)) + (jrow - jnp.broadcast_to(off_tiles, (LANE, LANE))) * tr
        start = jnp.sum(jnp.where(owns, first, 0.0), axis=1, keepdims=True)
        meta_ref[...] = jnp.where(lane1 == 0, jnp.broadcast_to(te, (LANE, LANE)),
                                  jnp.where(lane1 == 1, jnp.broadcast_to(nv, (LANE, LANE)),
                                            jnp.broadcast_to(start, (LANE, LANE)))).astype(I32)
        erow = lax.broadcasted_iota(I32, (SUBLANE, LANE), 0)
        emeta_ref[...] = jnp.where(erow == 0, jnp.broadcast_to(off_tiles, (SUBLANE, LANE)),
                                   jnp.broadcast_to(tiles, (SUBLANE, LANE))).astype(I32)

    @pl.when(ph == 1)
    def _():
        r = lax.broadcasted_iota(I32, (tm, tm), 0)
        c = lax.broadcasted_iota(I32, (tm, tm), 1)
        lower = (c < r).astype(BF16)
        before = jnp.dot(lower, both.astype(BF16), preferred_element_type=F32) + run_ref[...] + off_ref[...]
        p0 = jnp.sum(oh0 * before, axis=1, keepdims=True)
        p1 = jnp.sum(oh1 * before, axis=1, keepdims=True)
        pos_ref[...] = jnp.where(lane == 0, p0, jnp.where(lane == 1, p1, 0.0)).astype(I32)
        run_ref[...] = run_ref[...] + jnp.sum(both, axis=0, keepdims=True)


def _slots(eid):
    ta = eid.shape[0]
    tm = TM_SORT
    assert ta % tm == 0
    kern = functools.partial(_slots_body, tm=tm, tr=TR)
    return pl.pallas_call(
        kern,
        grid=(2, ta // tm),
        in_specs=[pl.BlockSpec((tm, LANE), lambda ph, i: (i, 0))],
        out_specs=[pl.BlockSpec((tm, LANE), lambda ph, i: (ph * i, 0)),
                   pl.BlockSpec((LANE, LANE), lambda ph, i: (0, 0)),
                   pl.BlockSpec((SUBLANE, LANE), lambda ph, i: (0, 0))],
        out_shape=[jax.ShapeDtypeStruct((ta, LANE), I32),
                   jax.ShapeDtypeStruct((LANE, LANE), I32),
                   jax.ShapeDtypeStruct((SUBLANE, LANE), I32)],
        scratch_shapes=[pltpu.VMEM((1, LANE), F32), pltpu.VMEM((1, LANE), F32), pltpu.VMEM((1, LANE), F32)],
        compiler_params=_cparams(("arbitrary", "arbitrary")),
        name="slots",
    )(eid)


def _ple_body(p0_ref, p1_ref, x1f_ref, pe_ref, gw_ref, gb_ref, pp_ref, r_ref, xs_hbm, back_ref, zbuf, sem, zsem,
              *, alpha, n_pairs):
    i = pl.program_id(0)
    tm = r_ref.shape[0]
    ta = n_pairs // TOP_K

    for r in range(tm):
        tok = i * tm + r
        for choice, pos_ref in enumerate((p0_ref, p1_ref)):
            pos = pos_ref[tok]
            back_ref[pos] = choice * ta + tok
            dst = pl.ds(pl.multiple_of(pos * CHUNK, CHUNK), CHUNK)
            pltpu.make_async_copy(x1f_ref.at[pl.ds(r * CHUNK, CHUNK)], xs_hbm.at[dst], sem).start()
    x1 = _load_flat(x1f_ref, tm)
    gate = jax.nn.sigmoid(jnp.dot(x1.astype(BF16), gw_ref[...], preferred_element_type=F32) + gb_ref[...])
    emb = jnp.dot(pe_ref[...], pp_ref[...], preferred_element_type=F32)
    r_ref[...] = alpha * x1 + gate * emb

    @pl.when(i == 0)
    def _():
        zbuf[...] = jnp.zeros(zbuf.shape, F32)
        tail = pltpu.make_async_copy(zbuf, xs_hbm.at[pl.ds(n_pairs * CHUNK, zbuf.shape[0])], zsem)
        tail.start()
        tail.wait()

    pltpu.make_async_copy(x1f_ref, xs_hbm.at[pl.ds(0, tm * CHUNK)], sem).wait()
    pltpu.make_async_copy(x1f_ref, xs_hbm.at[pl.ds(0, tm * CHUNK)], sem).wait()


def _ple(pos0, pos1, x1_flat, pe_b, gw_b, gb, pp_b, alpha):
    ta = pos0.shape[0]
    dm = gw_b.shape[0]
    tm = TM_ROW
    tr = TR
    n_pairs = TOP_K * ta
    row = lambda w: pl.BlockSpec((tm, w), lambda i, *_: (i, 0))
    const = lambda a: pl.BlockSpec(a.shape, lambda i, *_: (0,) * a.ndim, pipeline_mode=pl.Buffered(1))
    grid_spec = pltpu.PrefetchScalarGridSpec(
        num_scalar_prefetch=2,
        grid=(ta // tm,),
        in_specs=[pl.BlockSpec((tm * CHUNK, LANE), lambda i, *_: (i, 0)), row(pe_b.shape[1]),
                  const(gw_b), const(gb), const(pp_b)],
        out_specs=[row(dm), pl.BlockSpec(memory_space=pl.ANY), pl.BlockSpec(memory_space=pltpu.SMEM)],
        scratch_shapes=[pltpu.VMEM((tr * CHUNK, LANE), F32), pltpu.SemaphoreType.DMA(()),
                        pltpu.SemaphoreType.DMA(())],
    )
    return pl.pallas_call(
        functools.partial(_ple_body, alpha=alpha, n_pairs=n_pairs),
        grid_spec=grid_spec,
        out_shape=[jax.ShapeDtypeStruct((ta, dm), F32),
                   jax.ShapeDtypeStruct(((n_pairs + tr) * CHUNK, LANE), F32),
                   jax.ShapeDtypeStruct((n_pairs,), I32)],
        compiler_params=_cparams(("arbitrary",)),
        name="ple_dispatch",
    )(pos0, pos1, x1_flat, pe_b, gw_b, gb, pp_b)


def _expert_body(first_ref, ntile_ref, start_ref, nv_ref, dst_ref, x_hbm, wg_ref, wu_ref, wd_ref, y_hbm,
                 xbuf, ybuf, wgb, wub, wdb, gsem, ssem, *, tr):
    e = pl.program_id(0)
    ne = pl.num_programs(0)
    nt = first_ref[ne - 1] + ntile_ref[ne - 1]
    UNROLL = 8

    def get_tile(jj, sl):
        rows = pl.ds(pl.multiple_of(start_ref[jj] * CHUNK, CHUNK), tr * CHUNK)
        return pltpu.make_async_copy(x_hbm.at[rows], xbuf.at[sl], gsem.at[sl])

    def put_row(jj, sl, r):
        stage = pl.ds(pl.multiple_of(r * CHUNK, CHUNK), CHUNK)
        dst = pl.ds(pl.multiple_of(dst_ref[start_ref[jj] + r] * CHUNK, CHUNK), CHUNK)
        return pltpu.make_async_copy(ybuf.at[sl, stage], y_hbm.at[dst], ssem.at[sl])

    def put_rows(jj, sl):
        def block(rb, _):
            for q in range(UNROLL):
                put_row(jj, sl, rb * UNROLL + q).start()
            return 0

        nv = nv_ref[jj]
        full = nv // UNROLL
        lax.fori_loop(0, full, block, 0)
        for q in range(UNROLL - 1):
            @pl.when(full * UNROLL + q < nv)
            def _():
                put_row(jj, sl, full * UNROLL + q).start()

    def put_wait(jj, sl):
        rows = pl.ds(0, nv_ref[jj] * CHUNK)
        pltpu.make_async_copy(ybuf.at[sl, rows], y_hbm.at[rows], ssem.at[sl]).wait()

    @pl.when(e == 0)
    def _():
        get_tile(0, 0).start()

    @pl.when(ntile_ref[e] > 0)
    def _():
        wgb[...] = wg_ref[0].astype(BF16)
        wub[...] = wu_ref[0].astype(BF16)
        wdb[...] = wd_ref[0].astype(BF16)

        def tile(jj, _):
            j = first_ref[e] + jj
            slot = j % 2
            get_tile(j, slot).wait()

            @pl.when(j + 1 < nt)
            def _():
                get_tile(j + 1, 1 - slot).start()

            x = _load_flat(xbuf, tr, (slot,)).astype(BF16)
            hg = jnp.dot(x, wgb[...], preferred_element_type=F32)
            hu = jnp.dot(x, wub[...], preferred_element_type=F32)
            h = (hg * jax.nn.sigmoid(hg) * hu).astype(BF16)
            y = jnp.dot(h, wdb[...], preferred_element_type=F32)

            @pl.when(j >= 2)
            def _():
                put_wait(j - 2, slot)

            _store_flat(ybuf, y, (slot,))
            put_rows(j, slot)
            return 0

        lax.fori_loop(0, ntile_ref[e], tile, 0)

    @pl.when(e == ne - 1)
    def _():
        last = nt - 1
        put_wait(last, last % 2)

        @pl.when(nt >= 2)
        def _():
            put_wait(last - 1, 1 - last % 2)


def _experts(first_tile, n_tile, tile_start, tile_valid, back, xs_flat, w_gate, w_up, w_down):
    n_pairs = back.shape[0]
    ne, dm, ff = w_gate.shape
    tr = TR
    assert dm == CHUNK * LANE and first_tile.shape == (ne,)
    sp = lambda e, *_: (e, 0, 0)
    any_spec = pl.BlockSpec(memory_space=pl.ANY)
    grid_spec = pltpu.PrefetchScalarGridSpec(
        num_scalar_prefetch=5,
        grid=(ne,),
        in_specs=[any_spec,
                  pl.BlockSpec((1, dm, ff), sp), pl.BlockSpec((1, dm, ff), sp), pl.BlockSpec((1, ff, dm), sp)],
        out_specs=any_spec,
        scratch_shapes=[pltpu.VMEM((2, tr * CHUNK, LANE), F32), pltpu.VMEM((2, tr * CHUNK, LANE), F32),
                        pltpu.VMEM((dm, ff), BF16), pltpu.VMEM((dm, ff), BF16), pltpu.VMEM((ff, dm), BF16),
                        pltpu.SemaphoreType.DMA((2,)), pltpu.SemaphoreType.DMA((2,))],
    )
    return pl.pallas_call(
        functools.partial(_expert_body, tr=tr),
        grid_spec=grid_spec,
        out_shape=jax.ShapeDtypeStruct((n_pairs * CHUNK, LANE), F32),
        compiler_params=_cparams(("arbitrary",)),
        name="experts",
    )(first_tile, n_tile, tile_start, tile_valid, back, xs_flat, w_gate, w_up, w_down)


def _final_body(r_ref, y0_ref, y1_ref, wt_ref, g_ref, b_ref, yp_ref, ys_ref, *, n_prompt_tiles):
    i = pl.program_id(0)
    wt = wt_ref[...]
    tm = r_ref.shape[0]
    x = r_ref[...] + wt[:, 0:1] * _load_flat(y0_ref, tm) + wt[:, 1:2] * _load_flat(y1_ref, tm)
    y = _layer_norm(x, g_ref[...], b_ref[...])

    @pl.when(i < n_prompt_tiles)
    def _():
        yp_ref[...] = y

    @pl.when(i >= n_prompt_tiles)
    def _():
        ys_ref[...] = y


def _final(r, y01_flat, wts, ln_g, ln_b, t):
    ta, dm = r.shape
    tm = TM_ROW
    npt = t // tm
    nta = ta // tm
    row = lambda w: pl.BlockSpec((tm, w), lambda i: (i, 0))
    return pl.pallas_call(
        functools.partial(_final_body, n_prompt_tiles=npt),
        grid=(nta,),
        in_specs=[row(dm),
                  pl.BlockSpec((tm * CHUNK, LANE), lambda i: (i, 0)),
                  pl.BlockSpec((tm * CHUNK, LANE), lambda i: (nta + i, 0)),
                  row(LANE), _const_spec(ln_g.shape), _const_spec(ln_b.shape)],
        out_specs=[pl.BlockSpec((tm, dm), lambda i: (jnp.minimum(i, npt - 1), 0)),
                   pl.BlockSpec((tm, dm), lambda i: (0, 0))],
        out_shape=[jax.ShapeDtypeStruct((t, dm), F32), jax.ShapeDtypeStruct((tm, dm), F32)],
        compiler_params=_cparams(("arbitrary",)),
        name="final",
    )(r, y01_flat, y01_flat, wts, ln_g, ln_b)


def _layer(layer, x_p, x_s, pe_p, pe_s, cache_k, cache_v, state, page_table, w):
    t, dm = x_p.shape
    nb = x_s.shape[0]
    wn = N_HEADS * HEAD_W
    ta = t + ROW_PAD
    assert nb <= ROW_PAD and t % ROW_PAD == 0
    lam0 = _lambda_init(layer)
    depth_total = w["depth"]
    alpha = (2.0 * depth_total) ** 0.25
    page = cache_k.shape[1]
    start_pos = page_table.shape[1] * page

    pad = ta - t - nb
    xb = jnp.concatenate([x_p.astype(BF16), x_s.astype(BF16), jnp.zeros((pad, dm), BF16)], axis=0)
    u, qb, kb, kt_p, k_s, vb, vf_p, v_s, gates = _inproj(xb, w["w_in"].astype(BF16), t, nb)

    lamv = jnp.concatenate([w["lam_q1"], w["lam_k1"], w["lam_q2"], w["lam_k2"]], axis=0)
    subln_g = w["subln_g"]

    o_p = _prompt_attn(qb, kb, vb, lamv, subln_g, t, lam0)
    kt_pool = jnp.transpose(cache_k, (0, 2, 3, 4, 1)).reshape(cache_k.shape[0], wn, page)
    v_pool = cache_v.reshape(cache_v.shape[0] * page * N_HEADS, HEAD_W)
    o_s = _decode_attn(page_table, lamv, subln_g, qb[t:t + nb], k_s, v_s, kt_pool, v_pool, lam0)
    o_all = jnp.concatenate([o_p, o_s.astype(BF16), jnp.zeros((pad, wn), BF16)], axis=0)

    d_p = _pool_prompt(u, t)
    d_s = _pool_sample(u[t:t + nb], state, start_pos)
    d_all = jnp.concatenate([d_p, d_s, jnp.zeros((pad, d_p.shape[1]), BF16)], axis=0)

    x1_flat, logits = _mix(d_all, o_all, gates, x_p, x_s, w["pool_w"].astype(BF16), w["pool_scale"],
                           w["w_branch_pool"].astype(BF16), w["w_branch_attn"].astype(BF16),
                           w["w_out"].astype(BF16), w["ln1_g"], w["ln1_b"], w["router_w"], w["router_b"], alpha)

    eid, wts = _route(logits)
    pos, meta, emeta = _slots(eid)
    n_row_tiles = (TOP_K * ta) // TR + N_EXPERTS
    assert n_row_tiles <= LANE
    pos0, pos1 = pos[:, 0], pos[:, 1]

    pe_b = jnp.concatenate([pe_p.astype(BF16), pe_s.astype(BF16), jnp.zeros((pad, pe_p.shape[1]), BF16)], axis=0)
    r, xs_flat, back = _ple(pos0, pos1, x1_flat, pe_b, w["ple_gate_w"].astype(BF16), w["ple_gate_b"],
                            w["ple_proj"].astype(BF16), alpha)
    y01_flat = _experts(emeta[0, :N_EXPERTS], emeta[1, :N_EXPERTS], meta[:n_row_tiles, 2], meta[:n_row_tiles, 1],
                        back, xs_flat, w["expert_gate"], w["expert_up"], w["expert_down"])
    y_p, y_s = _final(r, y01_flat, wts, w["ln2_g"], w["ln2_b"], t)
    k_prompt = jnp.transpose(kt_p.reshape(N_HEADS, 2, HEAD_DIM, t), (3, 0, 1, 2))
    v_prompt = vf_p.reshape(t, N_HEADS, HEAD_W)
    return y_p, y_s[:nb], k_prompt, v_prompt, k_s, v_s, u


def kernel(x_prompt, x_sample, p_prompt, p_sample, cache_k, cache_v, state_pool, page_table, w_in, pool_w, pool_scale, lam_q1, lam_k1, lam_q2, lam_k2, subln_g, w_branch_pool, w_branch_attn, w_out, ln1_g, ln1_b, router_group_w, router_group_b, router_expert_w, router_expert_b, expert_gate, expert_up, expert_down, ple_proj, ple_gate_w, ple_gate_b, ln2_g, ln2_b):
    depth = w_in.shape[0]
    bp, t, dm = x_prompt.shape
    nb, ts, _ = x_sample.shape
    assert bp == 1 and ts == 1 and dm // 256 == N_HEADS
    assert dm % CHUNK == 0 and dm // CHUNK == LANE
    hp = x_prompt.reshape(t, dm)
    hs = x_sample.reshape(nb, dm)
    outs = {n: [] for n in ("kp", "vp", "sp", "ks", "vs", "ss")}
    for l in range(depth):
        wr = jnp.concatenate([router_group_w[l],
                              jnp.transpose(router_expert_w[l], (1, 0, 2)).reshape(dm, N_EXPERTS)], axis=1)
        wr = jnp.pad(wr, ((0, 0), (0, LANE - wr.shape[1])))
        wr_hi = wr.astype(BF16)
        wr = jnp.stack([wr_hi, (wr - wr_hi.astype(F32)).astype(BF16)])
        br = jnp.pad(jnp.concatenate([router_group_b[l], router_expert_b[l].reshape(-1)]),
                     (0, LANE - N_GROUPS - N_EXPERTS)).reshape(1, LANE)
        row = lambda a: a[l].reshape(1, -1)
        w = dict(depth=depth, w_in=w_in[l], pool_w=pool_w[l], pool_scale=row(pool_scale),
                 lam_q1=row(lam_q1), lam_k1=row(lam_k1), lam_q2=row(lam_q2), lam_k2=row(lam_k2),
                 subln_g=row(subln_g), w_branch_pool=w_branch_pool[l], w_branch_attn=w_branch_attn[l],
                 w_out=w_out[l], ln1_g=row(ln1_g), ln1_b=row(ln1_b), router_w=wr, router_b=br,
                 expert_gate=expert_gate[l], expert_up=expert_up[l], expert_down=expert_down[l],
                 ple_proj=ple_proj[l], ple_gate_w=ple_gate_w[l], ple_gate_b=row(ple_gate_b),
                 ln2_g=row(ln2_g), ln2_b=row(ln2_b))
        state = state_pool[l]
        hp, hs, k_p, v_p, k_s, v_s, u = _layer(l, hp, hs, p_prompt[l, 0], p_sample[l, :, 0], cache_k[l],
                                               cache_v[l], state, page_table, w)
        outs["kp"].append(k_p[None])
        outs["vp"].append(v_p[None])
        outs["sp"].append(u[t - POOL_STATE:t].reshape(1, POOL_STATE, -1))
        outs["ks"].append(k_s.reshape(nb, 1, N_HEADS, 2, HEAD_DIM))
        outs["vs"].append(v_s.reshape(nb, 1, N_HEADS, HEAD_W))
        outs["ss"].append(jnp.concatenate([state[:, 1:], u[t:t + nb][:, None, :]], axis=1))
    st = lambda n: jnp.stack(outs[n])
    return (hp.reshape(1, t, dm), hs.reshape(nb, 1, dm), st("kp"), st("vp"), st("sp"),
            st("ks"), st("vs"), st("ss"))
```

```python
import functools
import math

import jax
import jax.numpy as jnp
from jax import lax
from jax.experimental import pallas as pl
from jax.experimental.pallas import tpu as pltpu

F32 = jnp.float32
BF16 = jnp.bfloat16
I32 = jnp.int32

N_HEADS = 8
HEAD_DIM = 64
HEAD_W = 2 * HEAD_DIM
ATTN_SCALE = HEAD_DIM ** -0.5
LOG2E = math.log2(math.e)
SUBLN_EPS = 1e-5
LN_EPS = 1e-5
POOL_WINDOWS = (2, 4, 8, 16)
POOL_STATE = max(POOL_WINDOWS) - 1
N_GROUPS = 4
EXPERTS_PER_GROUP = 8
N_EXPERTS = N_GROUPS * EXPERTS_PER_GROUP
TOP_K = 2

LANE = 128
SUBLANE = 8
VMEM_LIMIT = 56 * 1024 * 1024
INPROJ_VMEM_LIMIT = 60 * 1024 * 1024

ROW_PAD = 256
TM_ROW = 256
TM_SORT = 768
TQ = 512
TK = 512
HEADS_PER_STEP = 1
TR = 256
HALO = 16
CHUNK = 16
DECODE_GROUP = 4
DECODE_SLOTS = 12


def _lambda_init(layer):
    return 0.8 - 0.6 * math.exp(-0.3 * layer)


def _cparams(sem, vmem=VMEM_LIMIT):
    return pltpu.CompilerParams(dimension_semantics=sem, vmem_limit_bytes=vmem)


def _const_spec(shape):
    nd = len(shape)
    return pl.BlockSpec(shape, lambda *_: (0,) * nd, pipeline_mode=pl.Buffered(1))


def _inproj_body(xp_ref, xs_ref, w_ref, u_ref, q_ref, kb_ref, kt_ref, ks_ref, vb_ref, vf_ref, vs_ref, g_ref,
                 *, n_prompt_tiles):
    i = pl.program_id(0)
    tm, d = xp_ref.shape
    ns = xs_ref.shape[0]
    wn = u_ref.shape[1]

    def project(x, prompt):
        xb = x.astype(BF16)
        col = lambda c: jnp.dot(xb, w_ref[:, c * wn:(c + 1) * wn], preferred_element_type=F32)
        u_ref[...] = col(0)
        q_ref[...] = (col(1) * (ATTN_SCALE * LOG2E)).astype(BF16)
        kz = col(2)
        kb_ref[...] = kz.astype(BF16)
        vz = col(3)
        vb_ref[...] = vz.astype(BF16)
        if prompt:
            kt_ref[...] = kz.T
            for h in range(N_HEADS):
                vf_ref[pl.ds(h, tm, stride=N_HEADS), :] = vz[:, h * HEAD_W:(h + 1) * HEAD_W]
        else:
            ks_ref[...] = kz[:ns]
            vs_ref[...] = vz[:ns]
        for c in range(4, w_ref.shape[1] // wn):
            g_ref[:, (c - 4) * wn:(c - 3) * wn] = jax.nn.sigmoid(col(c)).astype(BF16)

    @pl.when(i < n_prompt_tiles)
    def _():
        project(xp_ref[...], True)

    @pl.when(i >= n_prompt_tiles)
    def _():
        project(jnp.concatenate([xs_ref[...], jnp.zeros((tm - ns, d), F32)], axis=0), False)


def _inproj(x_p, x_s, w_in_b, ta):
    t, d = x_p.shape
    ns = x_s.shape[0]
    n_in = w_in_b.shape[1]
    wn = N_HEADS * HEAD_W
    assert n_in == 4 * wn + 2 * d and (2 * d) % wn == 0
    tm = TM_ROW
    assert t % tm == 0 and ta == t + tm and ns <= tm and ns % SUBLANE == 0
    npt = t // tm
    row = lambda w: pl.BlockSpec((tm, w), lambda i: (i, 0))
    prow = lambda i: jnp.minimum(i, npt - 1)
    last = lambda: pl.BlockSpec((ns, wn), lambda i: (0, 0))
    return pl.pallas_call(
        functools.partial(_inproj_body, n_prompt_tiles=npt),
        grid=(ta // tm,),
        in_specs=[pl.BlockSpec((tm, d), lambda i: (prow(i), 0)), _const_spec(x_s.shape), _const_spec(w_in_b.shape)],
        out_specs=[row(wn), row(wn),
                   row(wn), pl.BlockSpec((wn, tm), lambda i: (0, prow(i))), last(),
                   row(wn), pl.BlockSpec((tm * N_HEADS, HEAD_W), lambda i: (prow(i), 0)), last(),
                   row(2 * d)],
        out_shape=[jax.ShapeDtypeStruct((ta, wn), F32),
                   jax.ShapeDtypeStruct((ta, wn), BF16),
                   jax.ShapeDtypeStruct((ta, wn), BF16),
                   jax.ShapeDtypeStruct((wn, t), F32),
                   jax.ShapeDtypeStruct((ns, wn), F32),
                   jax.ShapeDtypeStruct((ta, wn), BF16),
                   jax.ShapeDtypeStruct((t * N_HEADS, HEAD_W), F32),
                   jax.ShapeDtypeStruct((ns, wn), F32),
                   jax.ShapeDtypeStruct((ta, 2 * d), BF16)],
        compiler_params=_cparams(("arbitrary",), INPROJ_VMEM_LIMIT),
        name="inproj",
    )(x_p, x_s, w_in_b)


def _lam_value(lamv_ref, lam0):
    lv = lamv_ref[...]
    a = jnp.sum(lv[0:1] * lv[1:2], axis=1, keepdims=True)
    b = jnp.sum(lv[2:3] * lv[3:4], axis=1, keepdims=True)
    return jnp.exp(a) - jnp.exp(b) + lam0


def _subln(o, g, lam0):
    ms = jnp.mean(o * o, axis=-1, keepdims=True)
    return o * lax.rsqrt(ms + SUBLN_EPS) * g * (1.0 - lam0)


def _prompt_attn_body(lamv_ref, gcol_ref, q_ref, k_ref, v_ref, o_ref, vt_ref, q2t_ref, s_even, s_odd,
                      m_ref, l_ref, acc_ref, *, tq, tk, hp, lam0):
    i = pl.program_id(1)
    nkv = v_ref.shape[0] // tk
    heads = [pl.ds(hh * HEAD_W, HEAD_W) for hh in range(hp)]

    @pl.when(i == 0)
    def _():
        def tr(jb, _):
            for hh in range(hp):
                blk = v_ref[pl.ds(pl.multiple_of(jb * tk, tk), tk), heads[hh]].astype(F32)
                vt_ref[hh, jb] = blk.T.astype(BF16)
            return 0
        lax.fori_loop(0, nkv, tr, 0)

    drow = lax.broadcasted_iota(I32, (HEAD_W, tq), 0)
    for hh in range(hp):
        qt = q_ref[:, heads[hh]].astype(F32).T
        q2t_ref[hh] = jnp.concatenate([jnp.where(drow < HEAD_DIM, qt, 0.0),
                                       jnp.where(drow >= HEAD_DIM, qt, 0.0)], axis=1).astype(BF16)
    m_ref[...] = jnp.full(m_ref.shape, -jnp.inf, F32)
    l_ref[...] = jnp.zeros(l_ref.shape, F32)
    acc_ref[...] = jnp.zeros(acc_ref.shape, F32)
    n_full = (i * tq) // tk

    def scores(j, s_ref):
        rows = pl.ds(pl.multiple_of(j * tk, tk), tk)
        for hh in range(hp):
            s_ref[hh] = jnp.dot(k_ref[rows, heads[hh]], q2t_ref[hh], preferred_element_type=F32)

    def absorb(j, s_ref, masked):
        for hh in range(hp):
            s = s_ref[hh]
            if masked:
                kpos = n_full * tk + lax.broadcasted_iota(I32, (tk, 2 * tq), 0)
                qcol = lax.broadcasted_iota(I32, (tk, 2 * tq), 1)
                qpos = i * tq + jnp.where(qcol >= tq, qcol - tq, qcol)
                s = jnp.where(kpos <= qpos, s, -jnp.inf)
            m = m_ref[hh]
            m_new = jnp.maximum(m, jnp.max(s, axis=0, keepdims=True))
            alpha = jnp.exp2(m - m_new)
            p = jnp.exp2(s - m_new)
            l_ref[hh] = alpha * l_ref[hh] + jnp.sum(p, axis=0, keepdims=True)
            acc_ref[hh] = alpha * acc_ref[hh] + jnp.dot(vt_ref[hh, j], p.astype(BF16), preferred_element_type=F32)
            m_ref[hh] = m_new

    def by_parity(j, fn):
        @pl.when(j % 2 == 0)
        def _():
            fn(s_even, s_odd)

        @pl.when(j % 2 == 1)
        def _():
            fn(s_odd, s_even)

    scores(0, s_even)

    def body(j, _):
        def stage(cur, nxt):
            scores(j + 1, nxt)
            absorb(j, cur, False)
        by_parity(j, stage)
        return 0

    lax.fori_loop(0, n_full, body, 0)
    by_parity(n_full, lambda cur, nxt: absorb(n_full, cur, True))

    lam = _lam_value(lamv_ref, lam0)
    for hh in range(hp):
        on = acc_ref[hh] / l_ref[hh]
        ot = on[:, :tq] - lam * on[:, tq:]
        ms = jnp.mean(ot * ot, axis=0, keepdims=True)
        ot = ot * lax.rsqrt(ms + SUBLN_EPS) * gcol_ref[...] * (1.0 - lam0)
        o_ref[:, heads[hh]] = ot.T.astype(o_ref.dtype)


def _prompt_attn(qb, kb, vb, lamv, subln_g, t, lam0):
    tq, tk, hp = TQ, TK, HEADS_PER_STEP
    assert t % tk == 0 and tk % tq == 0 and N_HEADS % hp == 0
    kern = functools.partial(_prompt_attn_body, tq=tq, tk=tk, hp=hp, lam0=lam0)
    gcol = subln_g.reshape(HEAD_W, 1)
    return pl.pallas_call(
        kern,
        grid=(N_HEADS // hp, t // tq),
        in_specs=[_const_spec(lamv.shape), _const_spec(gcol.shape),
                  pl.BlockSpec((tq, hp * HEAD_W), lambda h, i: (i, h)),
                  pl.BlockSpec((t, hp * HEAD_W), lambda h, i: (0, h)),
                  pl.BlockSpec((t, hp * HEAD_W), lambda h, i: (0, h))],
        out_specs=pl.BlockSpec((tq, hp * HEAD_W), lambda h, i: (i, h)),
        out_shape=jax.ShapeDtypeStruct((t, N_HEADS * HEAD_W), BF16),
        scratch_shapes=[pltpu.VMEM((hp, t // tk, HEAD_W, tk), BF16),
                        pltpu.VMEM((hp, HEAD_W, 2 * tq), BF16),
                        pltpu.VMEM((hp, tk, 2 * tq), F32), pltpu.VMEM((hp, tk, 2 * tq), F32),
                        pltpu.VMEM((hp, 1, 2 * tq), F32), pltpu.VMEM((hp, 1, 2 * tq), F32),
                        pltpu.VMEM((hp, HEAD_W, 2 * tq), F32)],
        compiler_params=_cparams(("arbitrary", "arbitrary")),
        name="prompt_attn",
    )(lamv, gcol, qb, kb, vb)


def _decode_attn_body(pt_ref, lamv_ref, g_ref, q_ref, qc_ref, kn_ref, vn_ref, kt_hbm, v_hbm, o_ref,
                      kbuf, vbuf, sem, *, n_pages, page, n_slots, group, lam0):
    b = pl.program_id(0)
    nb = pl.num_programs(0)
    nr = 2 * N_HEADS
    wn = N_HEADS * HEAD_W
    vrows = page * N_HEADS
    ahead = n_slots - group
    total = nb * n_pages
    assert n_pages % group == 0 and n_slots % group == 0 and ahead >= group

    def copies(g):
        slot = g % n_slots
        pg = pt_ref[g // n_pages, g % n_pages]
        ck = pltpu.make_async_copy(kt_hbm.at[pg], kbuf.at[slot], sem.at[0, slot])
        cv = pltpu.make_async_copy(v_hbm.at[pl.ds(pl.multiple_of(pg * vrows, vrows), vrows)],
                                   vbuf.at[slot], sem.at[1, slot])
        return ck, cv

    def start(g):
        for c in copies(g):
            c.start()

    @pl.when(b == 0)
    def _():
        for g in range(ahead):
            start(g)

    rr = lax.broadcasted_iota(I32, (nr, wn), 0)
    cc = lax.broadcasted_iota(I32, (nr, wn), 1)
    qrow = q_ref[0]
    qbd = jnp.where(cc // HEAD_DIM == rr, jnp.broadcast_to(qrow.astype(F32), (nr, wn)), 0.0).astype(BF16)
    rhead = lax.broadcasted_iota(I32, (nr, HEAD_W), 0) // 2
    qcol = jnp.broadcast_to(qc_ref[0], (wn, page))
    tok = lax.broadcasted_iota(I32, (page, vrows), 0)
    vrow = lax.broadcasted_iota(I32, (page, vrows), 1)
    spread = jnp.where(vrow // N_HEADS == tok, 1.0, 0.0).astype(BF16)
    own_head = lax.broadcasted_iota(I32, (nr, vrows), 1) % N_HEADS == lax.broadcasted_iota(I32, (nr, vrows), 0) // 2

    def body(jg, carry):
        m, l, acc = carry
        g0 = b * n_pages + jg * group
        slots = []
        ss = []
        for a in range(group):
            g = g0 + a

            @pl.when(g + ahead < total)
            def _():
                start(g + ahead)

            for c in copies(g):
                c.wait()
            slot = g % n_slots
            slots.append(slot)
            prod = kbuf[slot] * qcol
            ss.append(jnp.sum(prod.reshape(nr, HEAD_DIM, page), axis=1))
        m_new = m
        for s in ss:
            m_new = jnp.maximum(m_new, jnp.max(s, axis=1, keepdims=True))
        alpha = jnp.exp2(m - m_new)
        l = alpha * l
        acc = alpha * acc
        for s, slot in zip(ss, slots):
            p = jnp.exp2(s - m_new)
            l = l + jnp.sum(p, axis=1, keepdims=True)
            pe = jnp.dot(p.astype(BF16), spread, preferred_element_type=F32)
            pe = jnp.where(own_head, pe, 0.0).astype(BF16)
            acc = acc + jnp.dot(pe, vbuf[slot].astype(BF16), preferred_element_type=F32)
        return m_new, l, acc

    init = (jnp.full((nr, 1), -jnp.inf, F32), jnp.zeros((nr, 1), F32), jnp.zeros((nr, HEAD_W), F32))
    m, l, acc = lax.fori_loop(0, n_pages // group, body, init)

    kn = jnp.broadcast_to(kn_ref[0], (nr, wn))
    s_new = jnp.sum(qbd.astype(F32) * kn, axis=1, keepdims=True)
    vn = vn_ref[0]
    vnew = jnp.zeros((nr, HEAD_W), F32)
    for h in range(N_HEADS):
        vnew = vnew + jnp.where(rhead == h, jnp.broadcast_to(vn[:, h * HEAD_W:(h + 1) * HEAD_W], (nr, HEAD_W)), 0.0)
    m_new = jnp.maximum(m, s_new)
    alpha = jnp.exp2(m - m_new)
    p_new = jnp.exp2(s_new - m_new)
    l = alpha * l + p_new
    acc = alpha * acc + p_new * vnew
    on = acc / l
    lam = _lam_value(lamv_ref, lam0)
    o = on - lam * pltpu.roll(on, nr - 1, 0)
    o_ref[0] = _subln(o, g_ref[...], lam0)


def _decode_attn(page_table, lamv, subln_g, q_s, k_s, v_s, kt_pool, v_pool, lam0):
    nb, n_pages = page_table.shape
    wn = N_HEADS * HEAD_W
    page = kt_pool.shape[2]
    nr = 2 * N_HEADS
    n_slots = DECODE_SLOTS
    kern = functools.partial(_decode_attn_body, n_pages=n_pages, page=page, n_slots=n_slots,
                             group=DECODE_GROUP, lam0=lam0)
    grid_spec = pltpu.PrefetchScalarGridSpec(
        num_scalar_prefetch=1,
        grid=(nb,),
        in_specs=[pl.BlockSpec(lamv.shape, lambda b, pt: (0, 0)),
                  pl.BlockSpec(subln_g.shape, lambda b, pt: (0, 0)),
                  pl.BlockSpec((1, 1, wn), lambda b, pt: (b, 0, 0)),
                  pl.BlockSpec((1, wn, 1), lambda b, pt: (b, 0, 0)),
                  pl.BlockSpec((1, 1, wn), lambda b, pt: (b, 0, 0)),
                  pl.BlockSpec((1, 1, wn), lambda b, pt: (b, 0, 0)),
                  pl.BlockSpec(memory_space=pl.ANY),
                  pl.BlockSpec(memory_space=pl.ANY)],
        out_specs=pl.BlockSpec((1, nr, HEAD_W), lambda b, pt: (b, 0, 0)),
        scratch_shapes=[pltpu.VMEM((n_slots, wn, page), F32),
                        pltpu.VMEM((n_slots, page * N_HEADS, HEAD_W), F32),
                        pltpu.SemaphoreType.DMA((2, n_slots))],
    )
    out = pl.pallas_call(
        kern,
        grid_spec=grid_spec,
        out_shape=jax.ShapeDtypeStruct((nb, nr, HEAD_W), F32),
        compiler_params=_cparams(("arbitrary",)),
        name="decode_attn",
    )(page_table, lamv, subln_g, q_s.reshape(nb, 1, wn), q_s.astype(F32).reshape(nb, wn, 1),
      k_s.reshape(nb, 1, wn), v_s.reshape(nb, 1, wn), kt_pool, v_pool)
    return out[:, 0::2, :].reshape(nb, wn)


def _pool_prompt_body(u_ref, halo_ref, d_ref, ext_ref, *, tm, gw):
    i = pl.program_id(0)
    ext_ref[pl.ds(HALO, tm), :] = u_ref[...]

    @pl.when(i == 0)
    def _():
        ext_ref[pl.ds(0, HALO), :] = jnp.zeros((HALO, u_ref.shape[1]), F32)

    @pl.when(i > 0)
    def _():
        ext_ref[pl.ds(0, HALO), :] = halo_ref[...]

    pos = i * tm + lax.broadcasted_iota(I32, (tm, 1), 0)
    for g, w in enumerate(POOL_WINDOWS):
        cols = pl.ds(g * gw, gw)
        tok = ext_ref[pl.ds(HALO, tm), cols]
        s = tok
        for back in range(1, w):
            s = s + ext_ref[pl.ds(HALO - back, tm), cols]
        inv = 1.0 / jnp.minimum(pos + 1, w).astype(F32)
        d_ref[:, cols] = (s * inv - tok).astype(d_ref.dtype)


def _pool_prompt(u_all, t):
    wp = u_all.shape[1]
    gw = wp // len(POOL_WINDOWS)
    tm = TM_ROW
    assert t % tm == 0 and tm % HALO == 0 and gw % LANE == 0
    kern = functools.partial(_pool_prompt_body, tm=tm, gw=gw)
    return pl.pallas_call(
        kern,
        grid=(t // tm,),
        in_specs=[pl.BlockSpec((tm, wp), lambda i: (i, 0)),
                  pl.BlockSpec((HALO, wp), lambda i: (jnp.maximum(i * (tm // HALO) - 1, 0), 0))],
        out_specs=pl.BlockSpec((tm, wp), lambda i: (i, 0)),
        out_shape=jax.ShapeDtypeStruct((t, wp), BF16),
        scratch_shapes=[pltpu.VMEM((tm + HALO, wp), F32)],
        compiler_params=_cparams(("arbitrary",)),
        name="pool_prompt",
    )(u_all, u_all)


def _pool_sample_body(u_ref, st_ref, d_ref, *, gw, start_pos):
    st = st_ref[...]
    u = u_ref[...]
    ridx = lax.broadcasted_iota(I32, st.shape, 1)
    for g, w in enumerate(POOL_WINDOWS):
        hist = jnp.sum(jnp.where(ridx >= POOL_STATE - (w - 1), st, 0.0), axis=1)
        cnt = float(min(start_pos + 1, w))
        d = (hist + u) * (1.0 / cnt) - u
        d_ref[:, g * gw:(g + 1) * gw] = d[:, g * gw:(g + 1) * gw].astype(d_ref.dtype)


def _pool_sample(u_s, state, start_pos):
    nb, wp = u_s.shape
    gw = wp // len(POOL_WINDOWS)
    kern = functools.partial(_pool_sample_body, gw=gw, start_pos=start_pos)
    return pl.pallas_call(
        kern,
        grid=(1,),
        in_specs=[_const_spec(u_s.shape), _const_spec(state.shape)],
        out_specs=_const_spec((nb, wp)),
        out_shape=jax.ShapeDtypeStruct((nb, wp), BF16),
        compiler_params=_cparams(("arbitrary",)),
        name="pool_sample",
    )(u_s, state)


def _store_flat(flat_ref, x, lead=()):
    rows = x.shape[0]
    for c in range(CHUNK):
        flat_ref[(*lead, pl.ds(c, rows, stride=CHUNK), slice(None))] = x[:, c * LANE:(c + 1) * LANE]


def _load_flat(flat_ref, rows, lead=(), first=0):
    return jnp.concatenate(
        [flat_ref[(*lead, pl.ds(first * CHUNK + c, rows, stride=CHUNK), slice(None))] for c in range(CHUNK)],
        axis=1)


def _layer_norm(x, g, b):
    mu = jnp.mean(x, axis=-1, keepdims=True)
    xc = x - mu
    var = jnp.mean(xc * xc, axis=-1, keepdims=True)
    return xc * lax.rsqrt(var + LN_EPS) * g + b


def _mix_body(d_ref, o_ref, g_ref, xp_ref, xs_ref, pw_ref, ps_ref, wbp_ref, wba_ref, wo_ref,
              lg_ref, lb_ref, wr_ref, br_ref, x1f_ref, lo_ref, *, n_prompt_tiles, alpha):
    i = pl.program_id(0)
    tm, dm = xp_ref.shape
    ng = len(POOL_WINDOWS)
    gw = d_ref.shape[1] // ng
    ys = []
    for g in range(ng):
        ys.append(jnp.dot(d_ref[:, g * gw:(g + 1) * gw], pw_ref[g], preferred_element_type=F32))
    pool_y = (jnp.concatenate(ys, axis=1) * ps_ref[...]).astype(BF16)
    a = jnp.dot(pool_y, wbp_ref[...], preferred_element_type=F32)
    bb = jnp.dot(o_ref[...], wba_ref[...], preferred_element_type=F32)
    gates = g_ref[...].astype(F32)
    merged = (gates[:, :dm] * a + gates[:, dm:] * bb).astype(BF16)
    mix = jnp.dot(merged, wo_ref[...], preferred_element_type=F32)

    def finish(x):
        x1 = _layer_norm(alpha * x + mix, lg_ref[...], lb_ref[...])
        _store_flat(x1f_ref, x1)
        hi = x1.astype(BF16)
        lo = (x1 - hi.astype(F32)).astype(BF16)
        dot = functools.partial(jnp.dot, preferred_element_type=F32)
        lo_ref[...] = dot(hi, wr_ref[0]) + (dot(lo, wr_ref[0]) + dot(hi, wr_ref[1])) + br_ref[...]

    @pl.when(i < n_prompt_tiles)
    def _():
        finish(xp_ref[...])

    @pl.when(i >= n_prompt_tiles)
    def _():
        ns = xs_ref.shape[0]
        finish(jnp.concatenate([xs_ref[...], jnp.zeros((tm - ns, dm), F32)], axis=0))


def _mix(d_all, o_all, gates, x_p, x_s, pool_w_b, pool_scale, wbp, wba, wo, ln_g, ln_b, wr, br, alpha):
    ta = d_all.shape[0]
    t, dm = x_p.shape
    ns = x_s.shape[0]
    tm = TM_ROW
    assert ta % tm == 0 and t % tm == 0 and ns % SUBLANE == 0 and ns <= tm and ta == t + tm
    npt = t // tm
    wp = d_all.shape[1]
    wa = o_all.shape[1]
    row = lambda w: pl.BlockSpec((tm, w), lambda i: (i, 0))
    kern = functools.partial(_mix_body, n_prompt_tiles=npt, alpha=alpha)
    return pl.pallas_call(
        kern,
        grid=(ta // tm,),
        in_specs=[row(wp), row(wa), row(2 * dm),
                  pl.BlockSpec((tm, dm), lambda i: (jnp.minimum(i, npt - 1), 0)),
                  _const_spec(x_s.shape),
                  _const_spec(pool_w_b.shape), _const_spec(pool_scale.shape),
                  _const_spec(wbp.shape), _const_spec(wba.shape), _const_spec(wo.shape),
                  _const_spec(ln_g.shape), _const_spec(ln_b.shape),
                  _const_spec(wr.shape), _const_spec(br.shape)],
        out_specs=[pl.BlockSpec((tm * CHUNK, LANE), lambda i: (i, 0)), row(LANE)],
        out_shape=[jax.ShapeDtypeStruct((ta * CHUNK, LANE), F32),
                   jax.ShapeDtypeStruct((ta, LANE), F32)],
        compiler_params=_cparams(("arbitrary",)),
        name="branch_mix",
    )(d_all, o_all, gates, x_p, x_s, pool_w_b, pool_scale, wbp, wba, wo, ln_g, ln_b, wr, br)


def _route_body(lo_ref, eid_ref, wt_ref):
    lo = lo_ref[...]
    lane_i = lax.broadcasted_iota(I32, lo.shape, 1)
    lane = lane_i.astype(F32)
    ninf = -jnp.inf
    big = float(LANE)
    is_g = lane_i < N_GROUPS
    gl = jnp.where(is_g, lo, ninf)
    gmax = jnp.max(gl, axis=1, keepdims=True)
    gidx = jnp.min(jnp.where(gl == gmax, lane, big), axis=1, keepdims=True)
    gsum = jnp.sum(jnp.where(is_g, jnp.exp(gl - gmax), 0.0), axis=1, keepdims=True)
    gw = 1.0 / gsum
    lo_e = N_GROUPS + EXPERTS_PER_GROUP * gidx
    el = jnp.where(jnp.logical_and(lane >= lo_e, lane < lo_e + EXPERTS_PER_GROUP), lo, ninf)
    v1 = jnp.max(el, axis=1, keepdims=True)
    i1 = jnp.min(jnp.where(el == v1, lane, big), axis=1, keepdims=True)
    el2 = jnp.where(lane == i1, ninf, el)
    v2 = jnp.max(el2, axis=1, keepdims=True)
    i2 = jnp.min(jnp.where(el2 == v2, lane, big), axis=1, keepdims=True)
    e21 = jnp.exp(v2 - v1)
    w1 = gw / (1.0 + e21)
    w2 = gw * e21 / (1.0 + e21)
    eid = jnp.where(lane_i == 0, i1 - N_GROUPS, jnp.where(lane_i == 1, i2 - N_GROUPS, 0.0))
    eid_ref[...] = eid.astype(I32)
    wt_ref[...] = jnp.where(lane_i == 0, w1, jnp.where(lane_i == 1, w2, 0.0))


def _route(logits):
    ta = logits.shape[0]
    tm = TM_SORT
    assert ta % tm == 0
    row = pl.BlockSpec((tm, LANE), lambda i: (i, 0))
    return pl.pallas_call(
        _route_body,
        grid=(ta // tm,),
        in_specs=[row],
        out_specs=[row, row],
        out_shape=[jax.ShapeDtypeStruct((ta, LANE), I32), jax.ShapeDtypeStruct((ta, LANE), F32)],
        compiler_params=_cparams(("arbitrary",)),
        name="route",
    )(logits)


def _slots_body(eid_ref, pos_ref, meta_ref, emeta_ref, cnt_ref, run_ref, off_ref, *, tm, tr):
    ph = pl.program_id(0)
    i = pl.program_id(1)
    eid = eid_ref[...]
    lane = lax.broadcasted_iota(I32, (tm, LANE), 1)
    e0 = eid[:, 0:1]
    e1 = eid[:, 1:2]
    oh0 = (lane == e0).astype(F32)
    oh1 = (lane == e1).astype(F32)
    both = oh0 + oh1

    @pl.when(jnp.logical_and(ph == 0, i == 0))
    def _():
        cnt_ref[...] = jnp.zeros_like(cnt_ref)

    @pl.when(ph == 0)
    def _():
        cnt_ref[...] = cnt_ref[...] + jnp.sum(both, axis=0, keepdims=True)

    @pl.when(jnp.logical_and(ph == 1, i == 0))
    def _():
        cnt = cnt_ref[...]
        tiles = jnp.floor((cnt + (tr - 1)) * (1.0 / tr))
        a = lax.broadcasted_iota(I32, (LANE, LANE), 0)
        b = lax.broadcasted_iota(I32, (LANE, LANE), 1)
        upper = (a < b).astype(BF16)
        prefix = lambda v: jnp.dot(jnp.broadcast_to(v, (SUBLANE, LANE)).astype(BF16), upper,
                                   preferred_element_type=F32)[0:1]
        off_tiles = prefix(tiles)
        cnt_hi = jnp.floor(cnt * (1.0 / 256.0))
        off_rows = 256.0 * prefix(cnt_hi) + prefix(cnt - 256.0 * cnt_hi)
        off_ref[...] = off_rows
        run_ref[...] = jnp.zeros_like(run_ref)
        end_tiles = off_tiles + tiles
        lane1 = lax.broadcasted_iota(I32, (LANE, LANE), 1)
        jrow = lax.broadcasted_iota(I32, (LANE, LANE), 0).astype(F32)
        done = jnp.logical_and(lane1 < N_EXPERTS, jnp.broadcast_to(end_tiles, (LANE, LANE)) <= jrow)
        te = jnp.sum(done.astype(F32), axis=1, keepdims=True)
        owns = jnp.logical_and(jnp.broadcast_to(off_tiles, (LANE, LANE)) <= jrow,
                               jrow < jnp.broadcast_to(end_tiles, (LANE, LANE)))
        left = jnp.broadcast_to(cnt, (LANE, LANE)) - (jrow - jnp.broadcast_to(off_tiles, (LANE, LANE))) * tr
        nv = jnp.sum(jnp.where(owns, jnp.minimum(left, float(tr)), 0.0), axis=1, keepdims=True)
        first = jnp.broadcast_to(off_rows, (LANE, LANE)) + (jrow - jnp.broadcast_to(off_tiles, (LANE, LANE))) * tr
        start = jnp.sum(jnp.where(owns, first, 0.0), axis=1, keepdims=True)
        meta_ref[...] = jnp.where(lane1 == 0, jnp.broadcast_to(te, (LANE, LANE)),
                                  jnp.where(lane1 == 1, jnp.broadcast_to(nv, (LANE, LANE)),
                                            jnp.broadcast_to(start, (LANE, LANE)))).astype(I32)
        erow = lax.broadcasted_iota(I32, (SUBLANE, LANE), 0)
        emeta_ref[...] = jnp.where(erow == 0, jnp.broadcast_to(off_tiles, (SUBLANE, LANE)),
                                   jnp.broadcast_to(tiles, (SUBLANE, LANE))).astype(I32)

    @pl.when(ph == 1)
    def _():
        r = lax.broadcasted_iota(I32, (tm, tm), 0)
        c = lax.broadcasted_iota(I32, (tm, tm), 1)
        lower = (c < r).astype(BF16)
        before = jnp.dot(lower, both.astype(BF16), preferred_element_type=F32) + run_ref[...] + off_ref[...]
        p0 = jnp.sum(oh0 * before, axis=1, keepdims=True)
        p1 = jnp.sum(oh1 * before, axis=1, keepdims=True)
        pos_ref[...] = jnp.where(lane == 0, p0, jnp.where(lane == 1, p1, 0.0)).astype(I32)
        run_ref[...] = run_ref[...] + jnp.sum(both, axis=0, keepdims=True)


def _slots(eid):
    ta = eid.shape[0]
    tm = TM_SORT
    assert ta % tm == 0
    kern = functools.partial(_slots_body, tm=tm, tr=TR)
    return pl.pallas_call(
        kern,
        grid=(2, ta // tm),
        in_specs=[pl.BlockSpec((tm, LANE), lambda ph, i: (i, 0))],
        out_specs=[pl.BlockSpec((tm, LANE), lambda ph, i: (ph * i, 0)),
                   pl.BlockSpec((LANE, LANE), lambda ph, i: (0, 0)),
                   pl.BlockSpec((SUBLANE, LANE), lambda ph, i: (0, 0))],
        out_shape=[jax.ShapeDtypeStruct((ta, LANE), I32),
                   jax.ShapeDtypeStruct((LANE, LANE), I32),
                   jax.ShapeDtypeStruct((SUBLANE, LANE), I32)],
        scratch_shapes=[pltpu.VMEM((1, LANE), F32), pltpu.VMEM((1, LANE), F32), pltpu.VMEM((1, LANE), F32)],
        compiler_params=_cparams(("arbitrary", "arbitrary")),
        name="slots",
    )(eid)


def _ple_body(p0_ref, p1_ref, x1f_ref, pe_ref, gw_ref, gb_ref, pp_ref, r_ref, xs_hbm, back_ref, zbuf, sem, zsem,
              *, alpha, n_pairs):
    i = pl.program_id(0)
    tm = r_ref.shape[0]
    ta = n_pairs // TOP_K

    for r in range(tm):
        tok = i * tm + r
        for choice, pos_ref in enumerate((p0_ref, p1_ref)):
            pos = pos_ref[tok]
            back_ref[pos] = choice * ta + tok
            dst = pl.ds(pl.multiple_of(pos * CHUNK, CHUNK), CHUNK)
            pltpu.make_async_copy(x1f_ref.at[pl.ds(r * CHUNK, CHUNK)], xs_hbm.at[dst], sem).start()
    x1 = _load_flat(x1f_ref, tm)
    gate = jax.nn.sigmoid(jnp.dot(x1.astype(BF16), gw_ref[...], preferred_element_type=F32) + gb_ref[...])
    emb = jnp.dot(pe_ref[...], pp_ref[...], preferred_element_type=F32)
    r_ref[...] = alpha * x1 + gate * emb

    @pl.when(i == 0)
    def _():
        zbuf[...] = jnp.zeros(zbuf.shape, F32)
        tail = pltpu.make_async_copy(zbuf, xs_hbm.at[pl.ds(n_pairs * CHUNK, zbuf.shape[0])], zsem)
        tail.start()
        tail.wait()

    pltpu.make_async_copy(x1f_ref, xs_hbm.at[pl.ds(0, tm * CHUNK)], sem).wait()
    pltpu.make_async_copy(x1f_ref, xs_hbm.at[pl.ds(0, tm * CHUNK)], sem).wait()


def _ple(pos0, pos1, x1_flat, pe_b, gw_b, gb, pp_b, alpha):
    ta = pos0.shape[0]
    dm = gw_b.shape[0]
    tm = TM_ROW
    tr = TR
    n_pairs = TOP_K * ta
    row = lambda w: pl.BlockSpec((tm, w), lambda i, *_: (i, 0))
    const = lambda a: pl.BlockSpec(a.shape, lambda i, *_: (0,) * a.ndim, pipeline_mode=pl.Buffered(1))
    grid_spec = pltpu.PrefetchScalarGridSpec(
        num_scalar_prefetch=2,
        grid=(ta // tm,),
        in_specs=[pl.BlockSpec((tm * CHUNK, LANE), lambda i, *_: (i, 0)), row(pe_b.shape[1]),
                  const(gw_b), const(gb), const(pp_b)],
        out_specs=[row(dm), pl.BlockSpec(memory_space=pl.ANY), pl.BlockSpec(memory_space=pltpu.SMEM)],
        scratch_shapes=[pltpu.VMEM((tr * CHUNK, LANE), F32), pltpu.SemaphoreType.DMA(()),
                        pltpu.SemaphoreType.DMA(())],
    )
    return pl.pallas_call(
        functools.partial(_ple_body, alpha=alpha, n_pairs=n_pairs),
        grid_spec=grid_spec,
        out_shape=[jax.ShapeDtypeStruct((ta, dm), F32),
                   jax.ShapeDtypeStruct(((n_pairs + tr) * CHUNK, LANE), F32),
                   jax.ShapeDtypeStruct((n_pairs,), I32)],
        compiler_params=_cparams(("arbitrary",)),
        name="ple_dispatch",
    )(pos0, pos1, x1_flat, pe_b, gw_b, gb, pp_b)


def _expert_body(first_ref, ntile_ref, start_ref, nv_ref, dst_ref, x_hbm, wg_ref, wu_ref, wd_ref, y_hbm,
                 xbuf, ybuf, wgb, wub, wdb, gsem, ssem, *, tr):
    e = pl.program_id(0)
    ne = pl.num_programs(0)
    nt = first_ref[ne - 1] + ntile_ref[ne - 1]
    UNROLL = 8

    def get_tile(jj, sl):
        rows = pl.ds(pl.multiple_of(start_ref[jj] * CHUNK, CHUNK), tr * CHUNK)
        return pltpu.make_async_copy(x_hbm.at[rows], xbuf.at[sl], gsem.at[sl])

    def put_row(jj, sl, r):
        stage = pl.ds(pl.multiple_of(r * CHUNK, CHUNK), CHUNK)
        dst = pl.ds(pl.multiple_of(dst_ref[start_ref[jj] + r] * CHUNK, CHUNK), CHUNK)
        return pltpu.make_async_copy(ybuf.at[sl, stage], y_hbm.at[dst], ssem.at[sl])

    def put_rows(jj, sl):
        def block(rb, _):
            for q in range(UNROLL):
                put_row(jj, sl, rb * UNROLL + q).start()
            return 0

        nv = nv_ref[jj]
        full = nv // UNROLL
        lax.fori_loop(0, full, block, 0)
        for q in range(UNROLL - 1):
            @pl.when(full * UNROLL + q < nv)
            def _():
                put_row(jj, sl, full * UNROLL + q).start()

    def put_wait(jj, sl):
        rows = pl.ds(0, nv_ref[jj] * CHUNK)
        pltpu.make_async_copy(ybuf.at[sl, rows], y_hbm.at[rows], ssem.at[sl]).wait()

    @pl.when(e == 0)
    def _():
        get_tile(0, 0).start()

    @pl.when(ntile_ref[e] > 0)
    def _():
        wgb[...] = wg_ref[0].astype(BF16)
        wub[...] = wu_ref[0].astype(BF16)
        wdb[...] = wd_ref[0].astype(BF16)

        def tile(jj, _):
            j = first_ref[e] + jj
            slot = j % 2
            get_tile(j, slot).wait()

            @pl.when(j + 1 < nt)
            def _():
                get_tile(j + 1, 1 - slot).start()

            x = _load_flat(xbuf, tr, (slot,)).astype(BF16)
            hg = jnp.dot(x, wgb[...], preferred_element_type=F32)
            hu = jnp.dot(x, wub[...], preferred_element_type=F32)
            h = (hg * jax.nn.sigmoid(hg) * hu).astype(BF16)
            y = jnp.dot(h, wdb[...], preferred_element_type=F32)

            @pl.when(j >= 2)
            def _():
                put_wait(j - 2, slot)

            _store_flat(ybuf, y, (slot,))
            put_rows(j, slot)
            return 0

        lax.fori_loop(0, ntile_ref[e], tile, 0)

    @pl.when(e == ne - 1)
    def _():
        last = nt - 1
        put_wait(last, last % 2)

        @pl.when(nt >= 2)
        def _():
            put_wait(last - 1, 1 - last % 2)


def _experts(first_tile, n_tile, tile_start, tile_valid, back, xs_flat, w_gate, w_up, w_down):
    n_pairs = back.shape[0]
    ne, dm, ff = w_gate.shape
    tr = TR
    assert dm == CHUNK * LANE and first_tile.shape == (ne,)
    sp = lambda e, *_: (e, 0, 0)
    any_spec = pl.BlockSpec(memory_space=pl.ANY)
    grid_spec = pltpu.PrefetchScalarGridSpec(
        num_scalar_prefetch=5,
        grid=(ne,),
        in_specs=[any_spec,
                  pl.BlockSpec((1, dm, ff), sp), pl.BlockSpec((1, dm, ff), sp), pl.BlockSpec((1, ff, dm), sp)],
        out_specs=any_spec,
        scratch_shapes=[pltpu.VMEM((2, tr * CHUNK, LANE), F32), pltpu.VMEM((2, tr * CHUNK, LANE), F32),
                        pltpu.VMEM((dm, ff), BF16), pltpu.VMEM((dm, ff), BF16), pltpu.VMEM((ff, dm), BF16),
                        pltpu.SemaphoreType.DMA((2,)), pltpu.SemaphoreType.DMA((2,))],
    )
    return pl.pallas_call(
        functools.partial(_expert_body, tr=tr),
        grid_spec=grid_spec,
        out_shape=jax.ShapeDtypeStruct((n_pairs * CHUNK, LANE), F32),
        compiler_params=_cparams(("arbitrary",)),
        name="experts",
    )(first_tile, n_tile, tile_start, tile_valid, back, xs_flat, w_gate, w_up, w_down)


def _final_body(r_ref, y0_ref, y1_ref, wt_ref, g_ref, b_ref, yp_ref, ys_ref, *, n_prompt_tiles):
    i = pl.program_id(0)
    wt = wt_ref[...]
    tm = r_ref.shape[0]
    x = r_ref[...] + wt[:, 0:1] * _load_flat(y0_ref, tm) + wt[:, 1:2] * _load_flat(y1_ref, tm)
    y = _layer_norm(x, g_ref[...], b_ref[...])

    @pl.when(i < n_prompt_tiles)
    def _():
        yp_ref[...] = y

    @pl.when(i >= n_prompt_tiles)
    def _():
        ys_ref[...] = y


def _final(r, y01_flat, wts, ln_g, ln_b, t):
    ta, dm = r.shape
    tm = TM_ROW
    npt = t // tm
    nta = ta // tm
    row = lambda w: pl.BlockSpec((tm, w), lambda i: (i, 0))
    return pl.pallas_call(
        functools.partial(_final_body, n_prompt_tiles=npt),
        grid=(nta,),
        in_specs=[row(dm),
                  pl.BlockSpec((tm * CHUNK, LANE), lambda i: (i, 0)),
                  pl.BlockSpec((tm * CHUNK, LANE), lambda i: (nta + i, 0)),
                  row(LANE), _const_spec(ln_g.shape), _const_spec(ln_b.shape)],
        out_specs=[pl.BlockSpec((tm, dm), lambda i: (jnp.minimum(i, npt - 1), 0)),
                   pl.BlockSpec((tm, dm), lambda i: (0, 0))],
        out_shape=[jax.ShapeDtypeStruct((t, dm), F32), jax.ShapeDtypeStruct((tm, dm), F32)],
        compiler_params=_cparams(("arbitrary",)),
        name="final",
    )(r, y01_flat, y01_flat, wts, ln_g, ln_b)


def _layer(layer, x_p, x_s, pe_p, pe_s, cache_k, cache_v, state, page_table, w):
    t, dm = x_p.shape
    nb = x_s.shape[0]
    wn = N_HEADS * HEAD_W
    ta = t + ROW_PAD
    assert nb <= ROW_PAD and t % ROW_PAD == 0
    lam0 = _lambda_init(layer)
    depth_total = w["depth"]
    alpha = (2.0 * depth_total) ** 0.25
    page = cache_k.shape[1]
    start_pos = page_table.shape[1] * page

    pad = ta - t - nb
    u, qb, kb, kt_p, k_s, vb, vf_p, v_s, gates = _inproj(x_p, x_s, w["w_in"].astype(BF16), ta)

    lamv = jnp.concatenate([w["lam_q1"], w["lam_k1"], w["lam_q2"], w["lam_k2"]], axis=0)
    subln_g = w["subln_g"]

    o_p = _prompt_attn(qb, kb, vb, lamv, subln_g, t, lam0)
    kt_pool = jnp.transpose(cache_k, (0, 2, 3, 4, 1)).reshape(cache_k.shape[0], wn, page)
    v_pool = cache_v.reshape(cache_v.shape[0] * page * N_HEADS, HEAD_W)
    o_s = _decode_attn(page_table, lamv, subln_g, qb[t:t + nb], k_s, v_s, kt_pool, v_pool, lam0)
    o_all = jnp.concatenate([o_p, o_s.astype(BF16), jnp.zeros((pad, wn), BF16)], axis=0)

    d_p = _pool_prompt(u, t)
    d_s = _pool_sample(u[t:t + nb], state, start_pos)
    d_all = jnp.concatenate([d_p, d_s, jnp.zeros((pad, d_p.shape[1]), BF16)], axis=0)

    x1_flat, logits = _mix(d_all, o_all, gates, x_p, x_s, w["pool_w"].astype(BF16), w["pool_scale"],
                           w["w_branch_pool"].astype(BF16), w["w_branch_attn"].astype(BF16),
                           w["w_out"].astype(BF16), w["ln1_g"], w["ln1_b"], w["router_w"], w["router_b"], alpha)

    eid, wts = _route(logits)
    pos, meta, emeta = _slots(eid)
    n_row_tiles = (TOP_K * ta) // TR + N_EXPERTS
    assert n_row_tiles <= LANE
    pos0, pos1 = pos[:, 0], pos[:, 1]

    pe_b = jnp.concatenate([pe_p.astype(BF16), pe_s.astype(BF16), jnp.zeros((pad, pe_p.shape[1]), BF16)], axis=0)
    r, xs_flat, back = _ple(pos0, pos1, x1_flat, pe_b, w["ple_gate_w"].astype(BF16), w["ple_gate_b"],
                            w["ple_proj"].astype(BF16), alpha)
    y01_flat = _experts(emeta[0, :N_EXPERTS], emeta[1, :N_EXPERTS], meta[:n_row_tiles, 2], meta[:n_row_tiles, 1],
                        back, xs_flat, w["expert_gate"], w["expert_up"], w["expert_down"])
    y_p, y_s = _final(r, y01_flat, wts, w["ln2_g"], w["ln2_b"], t)
    k_prompt = jnp.transpose(kt_p.reshape(N_HEADS, 2, HEAD_DIM, t), (3, 0, 1, 2))
    v_prompt = vf_p.reshape(t, N_HEADS, HEAD_W)
    return y_p, y_s[:nb], k_prompt, v_prompt, k_s, v_s, u


def kernel(x_prompt, x_sample, p_prompt, p_sample, cache_k, cache_v, state_pool, page_table, w_in, pool_w, pool_scale, lam_q1, lam_k1, lam_q2, lam_k2, subln_g, w_branch_pool, w_branch_attn, w_out, ln1_g, ln1_b, router_group_w, router_group_b, router_expert_w, router_expert_b, expert_gate, expert_up, expert_down, ple_proj, ple_gate_w, ple_gate_b, ln2_g, ln2_b):
    depth = w_in.shape[0]
    bp, t, dm = x_prompt.shape
    nb, ts, _ = x_sample.shape
    assert bp == 1 and ts == 1 and dm // 256 == N_HEADS
    assert dm % CHUNK == 0 and dm // CHUNK == LANE
    hp = x_prompt.reshape(t, dm)
    hs = x_sample.reshape(nb, dm)
    outs = {n: [] for n in ("kp", "vp", "sp", "ks", "vs", "ss")}
    for l in range(depth):
        wr = jnp.concatenate([router_group_w[l],
                              jnp.transpose(router_expert_w[l], (1, 0, 2)).reshape(dm, N_EXPERTS)], axis=1)
        wr = jnp.pad(wr, ((0, 0), (0, LANE - wr.shape[1])))
        wr_hi = wr.astype(BF16)
        wr = jnp.stack([wr_hi, (wr - wr_hi.astype(F32)).astype(BF16)])
        br = jnp.pad(jnp.concatenate([router_group_b[l], router_expert_b[l].reshape(-1)]),
                     (0, LANE - N_GROUPS - N_EXPERTS)).reshape(1, LANE)
        row = lambda a: a[l].reshape(1, -1)
        w = dict(depth=depth, w_in=w_in[l], pool_w=pool_w[l], pool_scale=row(pool_scale),
                 lam_q1=row(lam_q1), lam_k1=row(lam_k1), lam_q2=row(lam_q2), lam_k2=row(lam_k2),
                 subln_g=row(subln_g), w_branch_pool=w_branch_pool[l], w_branch_attn=w_branch_attn[l],
                 w_out=w_out[l], ln1_g=row(ln1_g), ln1_b=row(ln1_b), router_w=wr, router_b=br,
                 expert_gate=expert_gate[l], expert_up=expert_up[l], expert_down=expert_down[l],
                 ple_proj=ple_proj[l], ple_gate_w=ple_gate_w[l], ple_gate_b=row(ple_gate_b),
                 ln2_g=row(ln2_g), ln2_b=row(ln2_b))
        state = state_pool[l]
        hp, hs, k_p, v_p, k_s, v_s, u = _layer(l, hp, hs, p_prompt[l, 0], p_sample[l, :, 0], cache_k[l],
                                               cache_v[l], state, page_table, w)
        outs["kp"].append(k_p[None])
        outs["vp"].append(v_p[None])
        outs["sp"].append(u[t - POOL_STATE:t].reshape(1, POOL_STATE, -1))
        outs["ks"].append(k_s.reshape(nb, 1, N_HEADS, 2, HEAD_DIM))
        outs["vs"].append(v_s.reshape(nb, 1, N_HEADS, HEAD_W))
        outs["ss"].append(jnp.concatenate([state[:, 1:], u[t:t + nb][:, None, :]], axis=1))
    st = lambda n: jnp.stack(outs[n])
    return (hp.reshape(1, t, dm), hs.reshape(nb, 1, dm), st("kp"), st("vp"), st("sp"),
            st("ks"), st("vs"), st("ss"))
```

```python
import functools
import math

import jax
import jax.numpy as jnp
from jax import lax
from jax.experimental import pallas as pl
from jax.experimental.pallas import tpu as pltpu

F32 = jnp.float32
BF16 = jnp.bfloat16
I32 = jnp.int32

N_HEADS = 8
HEAD_DIM = 64
HEAD_W = 2 * HEAD_DIM
ATTN_SCALE = HEAD_DIM ** -0.5
LOG2E = math.log2(math.e)
SUBLN_EPS = 1e-5
LN_EPS = 1e-5
POOL_WINDOWS = (2, 4, 8, 16)
POOL_STATE = max(POOL_WINDOWS) - 1
N_GROUPS = 4
EXPERTS_PER_GROUP = 8
N_EXPERTS = N_GROUPS * EXPERTS_PER_GROUP
TOP_K = 2

LANE = 128
SUBLANE = 8
VMEM_LIMIT = 56 * 1024 * 1024
INPROJ_VMEM_LIMIT = 60 * 1024 * 1024

ROW_PAD = 256
TM_ROW = 256
TM_SORT = 768
TQ = 512
TK = 512
HEADS_PER_STEP = 1
TR = 256
HALO = 16
CHUNK = 16
DECODE_GROUP = 4
DECODE_SLOTS = 12


def _lambda_init(layer):
    return 0.8 - 0.6 * math.exp(-0.3 * layer)


def _cparams(sem, vmem=VMEM_LIMIT):
    return pltpu.CompilerParams(dimension_semantics=sem, vmem_limit_bytes=vmem)


def _const_spec(shape):
    nd = len(shape)
    return pl.BlockSpec(shape, lambda *_: (0,) * nd, pipeline_mode=pl.Buffered(1))


def _inproj_body(xp_ref, xs_ref, w_ref, u_ref, q_ref, kb_ref, kt_ref, ks_ref, vb_ref, vf_ref, vs_ref, g_ref,
                 *, n_prompt_tiles):
    i = pl.program_id(0)
    tm, d = xp_ref.shape
    ns = xs_ref.shape[0]
    wn = u_ref.shape[1]

    def project(x, prompt):
        xb = x.astype(BF16)
        col = lambda c: jnp.dot(xb, w_ref[:, c * wn:(c + 1) * wn], preferred_element_type=F32)
        u_ref[...] = col(0)
        q_ref[...] = (col(1) * (ATTN_SCALE * LOG2E)).astype(BF16)
        kz = col(2)
        kb_ref[...] = kz.astype(BF16)
        vz = col(3)
        vb_ref[...] = vz.astype(BF16)
        if prompt:
            kt_ref[...] = kz.T
            for h in range(N_HEADS):
                vf_ref[pl.ds(h, tm, stride=N_HEADS), :] = vz[:, h * HEAD_W:(h + 1) * HEAD_W]
        else:
            ks_ref[...] = kz[:ns]
            vs_ref[...] = vz[:ns]
        for c in range(4, w_ref.shape[1] // wn):
            g_ref[:, (c - 4) * wn:(c - 3) * wn] = jax.nn.sigmoid(col(c)).astype(BF16)

    @pl.when(i < n_prompt_tiles)
    def _():
        project(xp_ref[...], True)

    @pl.when(i >= n_prompt_tiles)
    def _():
        project(jnp.concatenate([xs_ref[...], jnp.zeros((tm - ns, d), F32)], axis=0), False)


def _inproj(x_p, x_s, w_in_b, ta):
    t, d = x_p.shape
    ns = x_s.shape[0]
    n_in = w_in_b.shape[1]
    wn = N_HEADS * HEAD_W
    assert n_in == 4 * wn + 2 * d and (2 * d) % wn == 0
    tm = TM_ROW
    assert t % tm == 0 and ta == t + tm and ns <= tm and ns % SUBLANE == 0
    npt = t // tm
    row = lambda w: pl.BlockSpec((tm, w), lambda i: (i, 0))
    prow = lambda i: jnp.minimum(i, npt - 1)
    last = lambda: pl.BlockSpec((ns, wn), lambda i: (0, 0))
    return pl.pallas_call(
        functools.partial(_inproj_body, n_prompt_tiles=npt),
        grid=(ta // tm,),
        in_specs=[pl.BlockSpec((tm, d), lambda i: (prow(i), 0)), _const_spec(x_s.shape), _const_spec(w_in_b.shape)],
        out_specs=[row(wn), row(wn),
                   row(wn), pl.BlockSpec((wn, tm), lambda i: (0, prow(i))), last(),
                   row(wn), pl.BlockSpec((tm * N_HEADS, HEAD_W), lambda i: (prow(i), 0)), last(),
                   row(2 * d)],
        out_shape=[jax.ShapeDtypeStruct((ta, wn), F32),
                   jax.ShapeDtypeStruct((ta, wn), BF16),
                   jax.ShapeDtypeStruct((ta, wn), BF16),
                   jax.ShapeDtypeStruct((wn, t), F32),
                   jax.ShapeDtypeStruct((ns, wn), F32),
                   jax.ShapeDtypeStruct((ta, wn), BF16),
                   jax.ShapeDtypeStruct((t * N_HEADS, HEAD_W), F32),
                   jax.ShapeDtypeStruct((ns, wn), F32),
                   jax.ShapeDtypeStruct((ta, 2 * d), BF16)],
        compiler_params=_cparams(("arbitrary",), INPROJ_VMEM_LIMIT),
        name="inproj",
    )(x_p, x_s, w_in_b)


def _lam_value(lamv_ref, lam0):
    lv = lamv_ref[...]
    a = jnp.sum(lv[0:1] * lv[1:2], axis=1, keepdims=True)
    b = jnp.sum(lv[2:3] * lv[3:4], axis=1, keepdims=True)
    return jnp.exp(a) - jnp.exp(b) + lam0


def _subln(o, g, lam0):
    ms = jnp.mean(o * o, axis=-1, keepdims=True)
    return o * lax.rsqrt(ms + SUBLN_EPS) * g * (1.0 - lam0)


def _prompt_attn_body(lamv_ref, gcol_ref, q_ref, k_ref, v_ref, o_ref, vt_ref, q2t_ref, s_even, s_odd,
                      m_ref, l_ref, acc_ref, *, tq, tk, hp, lam0):
    i = pl.program_id(1)
    nkv = v_ref.shape[0] // tk
    heads = [pl.ds(hh * HEAD_W, HEAD_W) for hh in range(hp)]

    @pl.when(i == 0)
    def _():
        def tr(jb, _):
            for hh in range(hp):
                blk = v_ref[pl.ds(pl.multiple_of(jb * tk, tk), tk), heads[hh]].astype(F32)
                vt_ref[hh, jb] = blk.T.astype(BF16)
            return 0
        lax.fori_loop(0, nkv, tr, 0)

    drow = lax.broadcasted_iota(I32, (HEAD_W, tq), 0)
    for hh in range(hp):
        qt = q_ref[:, heads[hh]].astype(F32).T
        q2t_ref[hh] = jnp.concatenate([jnp.where(drow < HEAD_DIM, qt, 0.0),
                                       jnp.where(drow >= HEAD_DIM, qt, 0.0)], axis=1).astype(BF16)
    m_ref[...] = jnp.full(m_ref.shape, -jnp.inf, F32)
    l_ref[...] = jnp.zeros(l_ref.shape, F32)
    acc_ref[...] = jnp.zeros(acc_ref.shape, F32)
    n_full = (i * tq) // tk

    def scores(j, s_ref):
        rows = pl.ds(pl.multiple_of(j * tk, tk), tk)
        for hh in range(hp):
            s_ref[hh] = jnp.dot(k_ref[rows, heads[hh]], q2t_ref[hh], preferred_element_type=F32)

    def absorb(j, s_ref, masked):
        for hh in range(hp):
            s = s_ref[hh]
            if masked:
                kpos = n_full * tk + lax.broadcasted_iota(I32, (tk, 2 * tq), 0)
                qcol = lax.broadcasted_iota(I32, (tk, 2 * tq), 1)
                qpos = i * tq + jnp.where(qcol >= tq, qcol - tq, qcol)
                s = jnp.where(kpos <= qpos, s, -jnp.inf)
            m = m_ref[hh]
            m_new = jnp.maximum(m, jnp.max(s, axis=0, keepdims=True))
            alpha = jnp.exp2(m - m_new)
            p = jnp.exp2(s - m_new)
            l_ref[hh] = alpha * l_ref[hh] + jnp.sum(p, axis=0, keepdims=True)
            acc_ref[hh] = alpha * acc_ref[hh] + jnp.dot(vt_ref[hh, j], p.astype(BF16), preferred_element_type=F32)
            m_ref[hh] = m_new

    def by_parity(j, fn):
        @pl.when(j % 2 == 0)
        def _():
            fn(s_even, s_odd)

        @pl.when(j % 2 == 1)
        def _():
            fn(s_odd, s_even)

    scores(0, s_even)

    def body(j, _):
        def stage(cur, nxt):
            scores(j + 1, nxt)
            absorb(j, cur, False)
        by_parity(j, stage)
        return 0

    lax.fori_loop(0, n_full, body, 0)
    by_parity(n_full, lambda cur, nxt: absorb(n_full, cur, True))

    lam = _lam_value(lamv_ref, lam0)
    for hh in range(hp):
        on = acc_ref[hh] / l_ref[hh]
        ot = on[:, :tq] - lam * on[:, tq:]
        ms = jnp.mean(ot * ot, axis=0, keepdims=True)
        ot = ot * lax.rsqrt(ms + SUBLN_EPS) * gcol_ref[...] * (1.0 - lam0)
        o_ref[:, heads[hh]] = ot.T.astype(o_ref.dtype)


def _prompt_attn(qb, kb, vb, lamv, subln_g, t, lam0):
    tq, tk, hp = TQ, TK, HEADS_PER_STEP
    assert t % tk == 0 and tk % tq == 0 and N_HEADS % hp == 0
    kern = functools.partial(_prompt_attn_body, tq=tq, tk=tk, hp=hp, lam0=lam0)
    gcol = subln_g.reshape(HEAD_W, 1)
    return pl.pallas_call(
        kern,
        grid=(N_HEADS // hp, t // tq),
        in_specs=[_const_spec(lamv.shape), _const_spec(gcol.shape),
                  pl.BlockSpec((tq, hp * HEAD_W), lambda h, i: (i, h)),
                  pl.BlockSpec((t, hp * HEAD_W), lambda h, i: (0, h)),
                  pl.BlockSpec((t, hp * HEAD_W), lambda h, i: (0, h))],
        out_specs=pl.BlockSpec((tq, hp * HEAD_W), lambda h, i: (i, h)),
        out_shape=jax.ShapeDtypeStruct((t, N_HEADS * HEAD_W), BF16),
        scratch_shapes=[pltpu.VMEM((hp, t // tk, HEAD_W, tk), BF16),
                        pltpu.VMEM((hp, HEAD_W, 2 * tq), BF16),
                        pltpu.VMEM((hp, tk, 2 * tq), F32), pltpu.VMEM((hp, tk, 2 * tq), F32),
                        pltpu.VMEM((hp, 1, 2 * tq), F32), pltpu.VMEM((hp, 1, 2 * tq), F32),
                        pltpu.VMEM((hp, HEAD_W, 2 * tq), F32)],
        compiler_params=_cparams(("arbitrary", "arbitrary")),
        name="prompt_attn",
    )(lamv, gcol, qb, kb, vb)


def _decode_attn_body(pt_ref, lamv_ref, g_ref, q_ref, qc_ref, kn_ref, vn_ref, kt_hbm, v_hbm, o_ref,
                      kbuf, vbuf, sem, *, n_pages, page, n_slots, group, lam0):
    b = pl.program_id(0)
    nb = pl.num_programs(0)
    nr = 2 * N_HEADS
    wn = N_HEADS * HEAD_W
    vrows = page * N_HEADS
    ahead = n_slots - group
    total = nb * n_pages
    assert n_pages % group == 0 and n_slots % group == 0 and ahead >= group

    def copies(g):
        slot = g % n_slots
        pg = pt_ref[g // n_pages, g % n_pages]
        ck = pltpu.make_async_copy(kt_hbm.at[pg], kbuf.at[slot], sem.at[0, slot])
        cv = pltpu.make_async_copy(v_hbm.at[pl.ds(pl.multiple_of(pg * vrows, vrows), vrows)],
                                   vbuf.at[slot], sem.at[1, slot])
        return ck, cv

    def start(g):
        for c in copies(g):
            c.start()

    @pl.when(b == 0)
    def _():
        for g in range(ahead):
            start(g)

    rr = lax.broadcasted_iota(I32, (nr, wn), 0)
    cc = lax.broadcasted_iota(I32, (nr, wn), 1)
    qrow = q_ref[0]
    qbd = jnp.where(cc // HEAD_DIM == rr, jnp.broadcast_to(qrow.astype(F32), (nr, wn)), 0.0).astype(BF16)
    rhead = lax.broadcasted_iota(I32, (nr, HEAD_W), 0) // 2
    qcol = jnp.broadcast_to(qc_ref[0], (wn, page))
    tok = lax.broadcasted_iota(I32, (page, vrows), 0)
    vrow = lax.broadcasted_iota(I32, (page, vrows), 1)
    spread = jnp.where(vrow // N_HEADS == tok, 1.0, 0.0).astype(BF16)
    own_head = lax.broadcasted_iota(I32, (nr, vrows), 1) % N_HEADS == lax.broadcasted_iota(I32, (nr, vrows), 0) // 2

    def body(jg, carry):
        m, l, acc = carry
        g0 = b * n_pages + jg * group
        slots = []
        ss = []
        for a in range(group):
            g = g0 + a

            @pl.when(g + ahead < total)
            def _():
                start(g + ahead)

            for c in copies(g):
                c.wait()
            slot = g % n_slots
            slots.append(slot)
            prod = kbuf[slot] * qcol
            ss.append(jnp.sum(prod.reshape(nr, HEAD_DIM, page), axis=1))
        m_new = m
        for s in ss:
            m_new = jnp.maximum(m_new, jnp.max(s, axis=1, keepdims=True))
        alpha = jnp.exp2(m - m_new)
        l = alpha * l
        acc = alpha * acc
        for s, slot in zip(ss, slots):
            p = jnp.exp2(s - m_new)
            l = l + jnp.sum(p, axis=1, keepdims=True)
            pe = jnp.dot(p.astype(BF16), spread, preferred_element_type=F32)
            pe = jnp.where(own_head, pe, 0.0).astype(BF16)
            acc = acc + jnp.dot(pe, vbuf[slot].astype(BF16), preferred_element_type=F32)
        return m_new, l, acc

    init = (jnp.full((nr, 1), -jnp.inf, F32), jnp.zeros((nr, 1), F32), jnp.zeros((nr, HEAD_W), F32))
    m, l, acc = lax.fori_loop(0, n_pages // group, body, init)

    kn = jnp.broadcast_to(kn_ref[0], (nr, wn))
    s_new = jnp.sum(qbd.astype(F32) * kn, axis=1, keepdims=True)
    vn = vn_ref[0]
    vnew = jnp.zeros((nr, HEAD_W), F32)
    for h in range(N_HEADS):
        vnew = vnew + jnp.where(rhead == h, jnp.broadcast_to(vn[:, h * HEAD_W:(h + 1) * HEAD_W], (nr, HEAD_W)), 0.0)
    m_new = jnp.maximum(m, s_new)
    alpha = jnp.exp2(m - m_new)
    p_new = jnp.exp2(s_new - m_new)
    l = alpha * l + p_new
    acc = alpha * acc + p_new * vnew
    on = acc / l
    lam = _lam_value(lamv_ref, lam0)
    o = on - lam * pltpu.roll(on, nr - 1, 0)
    o_ref[0] = _subln(o, g_ref[...], lam0)


def _decode_attn(page_table, lamv, subln_g, q_s, k_s, v_s, kt_pool, v_pool, lam0):
    nb, n_pages = page_table.shape
    wn = N_HEADS * HEAD_W
    page = kt_pool.shape[2]
    nr = 2 * N_HEADS
    n_slots = DECODE_SLOTS
    kern = functools.partial(_decode_attn_body, n_pages=n_pages, page=page, n_slots=n_slots,
                             group=DECODE_GROUP, lam0=lam0)
    grid_spec = pltpu.PrefetchScalarGridSpec(
        num_scalar_prefetch=1,
        grid=(nb,),
        in_specs=[pl.BlockSpec(lamv.shape, lambda b, pt: (0, 0)),
                  pl.BlockSpec(subln_g.shape, lambda b, pt: (0, 0)),
                  pl.BlockSpec((1, 1, wn), lambda b, pt: (b, 0, 0)),
                  pl.BlockSpec((1, wn, 1), lambda b, pt: (b, 0, 0)),
                  pl.BlockSpec((1, 1, wn), lambda b, pt: (b, 0, 0)),
                  pl.BlockSpec((1, 1, wn), lambda b, pt: (b, 0, 0)),
                  pl.BlockSpec(memory_space=pl.ANY),
                  pl.BlockSpec(memory_space=pl.ANY)],
        out_specs=pl.BlockSpec((1, nr, HEAD_W), lambda b, pt: (b, 0, 0)),
        scratch_shapes=[pltpu.VMEM((n_slots, wn, page), F32),
                        pltpu.VMEM((n_slots, page * N_HEADS, HEAD_W), F32),
                        pltpu.SemaphoreType.DMA((2, n_slots))],
    )
    out = pl.pallas_call(
        kern,
        grid_spec=grid_spec,
        out_shape=jax.ShapeDtypeStruct((nb, nr, HEAD_W), F32),
        compiler_params=_cparams(("arbitrary",)),
        name="decode_attn",
    )(page_table, lamv, subln_g, q_s.reshape(nb, 1, wn), q_s.astype(F32).reshape(nb, wn, 1),
      k_s.reshape(nb, 1, wn), v_s.reshape(nb, 1, wn), kt_pool, v_pool)
    return out[:, 0::2, :].reshape(nb, wn)


def _pool_prompt_body(u_ref, halo_ref, d_ref, ext_ref, *, tm, gw):
    i = pl.program_id(0)
    ext_ref[pl.ds(HALO, tm), :] = u_ref[...]

    @pl.when(i == 0)
    def _():
        ext_ref[pl.ds(0, HALO), :] = jnp.zeros((HALO, u_ref.shape[1]), F32)

    @pl.when(i > 0)
    def _():
        ext_ref[pl.ds(0, HALO), :] = halo_ref[...]

    pos = i * tm + lax.broadcasted_iota(I32, (tm, 1), 0)
    for g, w in enumerate(POOL_WINDOWS):
        cols = pl.ds(g * gw, gw)
        tok = ext_ref[pl.ds(HALO, tm), cols]
        s = tok
        for back in range(1, w):
            s = s + ext_ref[pl.ds(HALO - back, tm), cols]
        inv = 1.0 / jnp.minimum(pos + 1, w).astype(F32)
        d_ref[:, cols] = (s * inv - tok).astype(d_ref.dtype)


def _pool_prompt(u_all, t):
    wp = u_all.shape[1]
    gw = wp // len(POOL_WINDOWS)
    tm = TM_ROW
    assert t % tm == 0 and tm % HALO == 0 and gw % LANE == 0
    kern = functools.partial(_pool_prompt_body, tm=tm, gw=gw)
    return pl.pallas_call(
        kern,
        grid=(t // tm,),
        in_specs=[pl.BlockSpec((tm, wp), lambda i: (i, 0)),
                  pl.BlockSpec((HALO, wp), lambda i: (jnp.maximum(i * (tm // HALO) - 1, 0), 0))],
        out_specs=pl.BlockSpec((tm, wp), lambda i: (i, 0)),
        out_shape=jax.ShapeDtypeStruct((t, wp), BF16),
        scratch_shapes=[pltpu.VMEM((tm + HALO, wp), F32)],
        compiler_params=_cparams(("arbitrary",)),
        name="pool_prompt",
    )(u_all, u_all)


def _pool_sample_body(u_ref, st_ref, d_ref, *, gw, start_pos):
    st = st_ref[...]
    u = u_ref[...]
    ridx = lax.broadcasted_iota(I32, st.shape, 1)
    for g, w in enumerate(POOL_WINDOWS):
        hist = jnp.sum(jnp.where(ridx >= POOL_STATE - (w - 1), st, 0.0), axis=1)
        cnt = float(min(start_pos + 1, w))
        d = (hist + u) * (1.0 / cnt) - u
        d_ref[:, g * gw:(g + 1) * gw] = d[:, g * gw:(g + 1) * gw].astype(d_ref.dtype)


def _pool_sample(u_s, state, start_pos):
    nb, wp = u_s.shape
    gw = wp // len(POOL_WINDOWS)
    kern = functools.partial(_pool_sample_body, gw=gw, start_pos=start_pos)
    return pl.pallas_call(
        kern,
        grid=(1,),
        in_specs=[_const_spec(u_s.shape), _const_spec(state.shape)],
        out_specs=_const_spec((nb, wp)),
        out_shape=jax.ShapeDtypeStruct((nb, wp), BF16),
        compiler_params=_cparams(("arbitrary",)),
        name="pool_sample",
    )(u_s, state)


def _store_flat(flat_ref, x, lead=()):
    rows = x.shape[0]
    for c in range(CHUNK):
        flat_ref[(*lead, pl.ds(c, rows, stride=CHUNK), slice(None))] = x[:, c * LANE:(c + 1) * LANE]


def _load_flat(flat_ref, rows, lead=(), first=0):
    return jnp.concatenate(
        [flat_ref[(*lead, pl.ds(first * CHUNK + c, rows, stride=CHUNK), slice(None))] for c in range(CHUNK)],
        axis=1)


def _layer_norm(x, g, b):
    mu = jnp.mean(x, axis=-1, keepdims=True)
    xc = x - mu
    var = jnp.mean(xc * xc, axis=-1, keepdims=True)
    return xc * lax.rsqrt(var + LN_EPS) * g + b


def _mix_body(dp_ref, ds_ref, op_ref, os_ref, g_ref, xp_ref, xs_ref, pw_ref, ps_ref, wbp_ref, wba_ref, wo_ref,
              lg_ref, lb_ref, wr_ref, br_ref, x1f_ref, lo_ref, d_ref, o_ref, *, n_prompt_tiles, alpha):
    i = pl.program_id(0)
    tm, dm = xp_ref.shape
    ns = xs_ref.shape[0]
    ng = len(POOL_WINDOWS)
    gw = d_ref.shape[1] // ng

    @pl.when(i < n_prompt_tiles)
    def _():
        d_ref[...] = dp_ref[...]
        o_ref[...] = op_ref[...]

    @pl.when(i >= n_prompt_tiles)
    def _():
        d_ref[...] = jnp.zeros(d_ref.shape, d_ref.dtype)
        o_ref[...] = jnp.zeros(o_ref.shape, o_ref.dtype)
        d_ref[0:ns, :] = ds_ref[...]
        o_ref[0:ns, :] = os_ref[...].astype(o_ref.dtype)

    ys = []
    for g in range(ng):
        ys.append(jnp.dot(d_ref[:, g * gw:(g + 1) * gw], pw_ref[g], preferred_element_type=F32))
    pool_y = (jnp.concatenate(ys, axis=1) * ps_ref[...]).astype(BF16)
    a = jnp.dot(pool_y, wbp_ref[...], preferred_element_type=F32)
    bb = jnp.dot(o_ref[...], wba_ref[...], preferred_element_type=F32)
    gates = g_ref[...].astype(F32)
    merged = (gates[:, :dm] * a + gates[:, dm:] * bb).astype(BF16)
    mix = jnp.dot(merged, wo_ref[...], preferred_element_type=F32)

    def finish(x):
        x1 = _layer_norm(alpha * x + mix, lg_ref[...], lb_ref[...])
        _store_flat(x1f_ref, x1)
        hi = x1.astype(BF16)
        lo = (x1 - hi.astype(F32)).astype(BF16)
        dot = functools.partial(jnp.dot, preferred_element_type=F32)
        lo_ref[...] = dot(hi, wr_ref[0]) + (dot(lo, wr_ref[0]) + dot(hi, wr_ref[1])) + br_ref[...]

    @pl.when(i < n_prompt_tiles)
    def _():
        finish(xp_ref[...])

    @pl.when(i >= n_prompt_tiles)
    def _():
        ns = xs_ref.shape[0]
        finish(jnp.concatenate([xs_ref[...], jnp.zeros((tm - ns, dm), F32)], axis=0))


def _mix(d_p, d_s, o_p, o_s, gates, x_p, x_s, pool_w_b, pool_scale, wbp, wba, wo, ln_g, ln_b, wr, br, alpha):
    ta = gates.shape[0]
    t, dm = x_p.shape
    ns = x_s.shape[0]
    tm = TM_ROW
    assert ta % tm == 0 and t % tm == 0 and ns % (2 * SUBLANE) == 0 and ns <= tm and ta == t + tm
    npt = t // tm
    wp = d_p.shape[1]
    wa = o_p.shape[1]
    row = lambda w: pl.BlockSpec((tm, w), lambda i: (i, 0))
    prow = lambda w: pl.BlockSpec((tm, w), lambda i: (jnp.minimum(i, npt - 1), 0))
    kern = functools.partial(_mix_body, n_prompt_tiles=npt, alpha=alpha)
    return pl.pallas_call(
        kern,
        grid=(ta // tm,),
        in_specs=[prow(wp), _const_spec(d_s.shape), prow(wa), _const_spec(o_s.shape), row(2 * dm),
                  prow(dm),
                  _const_spec(x_s.shape),
                  _const_spec(pool_w_b.shape), _const_spec(pool_scale.shape),
                  _const_spec(wbp.shape), _const_spec(wba.shape), _const_spec(wo.shape),
                  _const_spec(ln_g.shape), _const_spec(ln_b.shape),
                  _const_spec(wr.shape), _const_spec(br.shape)],
        out_specs=[pl.BlockSpec((tm * CHUNK, LANE), lambda i: (i, 0)), row(LANE)],
        out_shape=[jax.ShapeDtypeStruct((ta * CHUNK, LANE), F32),
                   jax.ShapeDtypeStruct((ta, LANE), F32)],
        scratch_shapes=[pltpu.VMEM((tm, wp), BF16), pltpu.VMEM((tm, wa), BF16)],
        compiler_params=_cparams(("arbitrary",)),
        name="branch_mix",
    )(d_p, d_s, o_p, o_s, gates, x_p, x_s, pool_w_b, pool_scale, wbp, wba, wo, ln_g, ln_b, wr, br)


def _route_body(lo_ref, eid_ref, wt_ref):
    lo = lo_ref[...]
    lane_i = lax.broadcasted_iota(I32, lo.shape, 1)
    lane = lane_i.astype(F32)
    ninf = -jnp.inf
    big = float(LANE)
    is_g = lane_i < N_GROUPS
    gl = jnp.where(is_g, lo, ninf)
    gmax = jnp.max(gl, axis=1, keepdims=True)
    gidx = jnp.min(jnp.where(gl == gmax, lane, big), axis=1, keepdims=True)
    gsum = jnp.sum(jnp.where(is_g, jnp.exp(gl - gmax), 0.0), axis=1, keepdims=True)
    gw = 1.0 / gsum
    lo_e = N_GROUPS + EXPERTS_PER_GROUP * gidx
    el = jnp.where(jnp.logical_and(lane >= lo_e, lane < lo_e + EXPERTS_PER_GROUP), lo, ninf)
    v1 = jnp.max(el, axis=1, keepdims=True)
    i1 = jnp.min(jnp.where(el == v1, lane, big), axis=1, keepdims=True)
    el2 = jnp.where(lane == i1, ninf, el)
    v2 = jnp.max(el2, axis=1, keepdims=True)
    i2 = jnp.min(jnp.where(el2 == v2, lane, big), axis=1, keepdims=True)
    e21 = jnp.exp(v2 - v1)
    w1 = gw / (1.0 + e21)
    w2 = gw * e21 / (1.0 + e21)
    eid = jnp.where(lane_i == 0, i1 - N_GROUPS, jnp.where(lane_i == 1, i2 - N_GROUPS, 0.0))
    eid_ref[...] = eid.astype(I32)
    wt_ref[...] = jnp.where(lane_i == 0, w1, jnp.where(lane_i == 1, w2, 0.0))


def _route(logits):
    ta = logits.shape[0]
    tm = TM_SORT
    assert ta % tm == 0
    row = pl.BlockSpec((tm, LANE), lambda i: (i, 0))
    return pl.pallas_call(
        _route_body,
        grid=(ta // tm,),
        in_specs=[row],
        out_specs=[row, row],
        out_shape=[jax.ShapeDtypeStruct((ta, LANE), I32), jax.ShapeDtypeStruct((ta, LANE), F32)],
        compiler_params=_cparams(("arbitrary",)),
        name="route",
    )(logits)


def _slots_body(eid_ref, pos_ref, meta_ref, emeta_ref, cnt_ref, run_ref, off_ref, *, tm, tr):
    ph = pl.program_id(0)
    i = pl.program_id(1)
    eid = eid_ref[...]
    lane = lax.broadcasted_iota(I32, (tm, LANE), 1)
    e0 = eid[:, 0:1]
    e1 = eid[:, 1:2]
    oh0 = (lane == e0).astype(F32)
    oh1 = (lane == e1).astype(F32)
    both = oh0 + oh1

    @pl.when(jnp.logical_and(ph == 0, i == 0))
    def _():
        cnt_ref[...] = jnp.zeros_like(cnt_ref)

    @pl.when(ph == 0)
    def _():
        cnt_ref[...] = cnt_ref[...] + jnp.sum(both, axis=0, keepdims=True)

    @pl.when(jnp.logical_and(ph == 1, i == 0))
    def _():
        cnt = cnt_ref[...]
        tiles = jnp.floor((cnt + (tr - 1)) * (1.0 / tr))
        a = lax.broadcasted_iota(I32, (LANE, LANE), 0)
        b = lax.broadcasted_iota(I32, (LANE, LANE), 1)
        upper = (a < b).astype(BF16)
        prefix = lambda v: jnp.dot(jnp.broadcast_to(v, (SUBLANE, LANE)).astype(BF16), upper,
                                   preferred_element_type=F32)[0:1]
        off_tiles = prefix(tiles)
        cnt_hi = jnp.floor(cnt * (1.0 / 256.0))
        off_rows = 256.0 * prefix(cnt_hi) + prefix(cnt - 256.0 * cnt_hi)
        off_ref[...] = off_rows
        run_ref[...] = jnp.zeros_like(run_ref)
        end_tiles = off_tiles + tiles
        lane1 = lax.broadcasted_iota(I32, (LANE, LANE), 1)
        jrow = lax.broadcasted_iota(I32, (LANE, LANE), 0).astype(F32)
        done = jnp.logical_and(lane1 < N_EXPERTS, jnp.broadcast_to(end_tiles, (LANE, LANE)) <= jrow)
        te = jnp.sum(done.astype(F32), axis=1, keepdims=True)
        owns = jnp.logical_and(jnp.broadcast_to(off_tiles, (LANE, LANE)) <= jrow,
                               jrow < jnp.broadcast_to(end_tiles, (LANE, LANE)))
        left = jnp.broadcast_to(cnt, (LANE, LANE)) - (jrow - jnp.broadcast_to(off_tiles, (LANE, LANE))) * tr
        nv = jnp.sum(jnp.where(owns, jnp.minimum(left, float(tr)), 0.0), axis=1, keepdims=True)
        first = jnp.broadcast_to(off_rows, (LANE, LANE)) + (jrow - jnp.broadcast_to(off_tiles, (LANE, LANE))) * tr
        start = jnp.sum(jnp.where(owns, first, 0.0), axis=1, keepdims=True)
        meta_ref[...] = jnp.where(lane1 == 0, jnp.broadcast_to(te, (LANE, LANE)),
                                  jnp.where(lane1 == 1, jnp.broadcast_to(nv, (LANE, LANE)),
                                            jnp.broadcast_to(start, (LANE, LANE)))).astype(I32)
        erow = lax.broadcasted_iota(I32, (SUBLANE, LANE), 0)
        emeta_ref[...] = jnp.where(erow == 0, jnp.broadcast_to(off_tiles, (SUBLANE, LANE)),
                                   jnp.broadcast_to(tiles, (SUBLANE, LANE))).astype(I32)

    @pl.when(ph == 1)
    def _():
        r = lax.broadcasted_iota(I32, (tm, tm), 0)
        c = lax.broadcasted_iota(I32, (tm, tm), 1)
        lower = (c < r).astype(BF16)
        before = jnp.dot(lower, both.astype(BF16), preferred_element_type=F32) + run_ref[...] + off_ref[...]
        p0 = jnp.sum(oh0 * before, axis=1, keepdims=True)
        p1 = jnp.sum(oh1 * before, axis=1, keepdims=True)
        pos_ref[...] = jnp.where(lane == 0, p0, jnp.where(lane == 1, p1, 0.0)).astype(I32)
        run_ref[...] = run_ref[...] + jnp.sum(both, axis=0, keepdims=True)


def _slots(eid):
    ta = eid.shape[0]
    tm = TM_SORT
    assert ta % tm == 0
    kern = functools.partial(_slots_body, tm=tm, tr=TR)
    return pl.pallas_call(
        kern,
        grid=(2, ta // tm),
        in_specs=[pl.BlockSpec((tm, LANE), lambda ph, i: (i, 0))],
        out_specs=[pl.BlockSpec((tm, LANE), lambda ph, i: (ph * i, 0)),
                   pl.BlockSpec((LANE, LANE), lambda ph, i: (0, 0)),
                   pl.BlockSpec((SUBLANE, LANE), lambda ph, i: (0, 0))],
        out_shape=[jax.ShapeDtypeStruct((ta, LANE), I32),
                   jax.ShapeDtypeStruct((LANE, LANE), I32),
                   jax.ShapeDtypeStruct((SUBLANE, LANE), I32)],
        scratch_shapes=[pltpu.VMEM((1, LANE), F32), pltpu.VMEM((1, LANE), F32), pltpu.VMEM((1, LANE), F32)],
        compiler_params=_cparams(("arbitrary", "arbitrary")),
        name="slots",
    )(eid)


def _ple_body(p0_ref, p1_ref, x1f_ref, pe_ref, gw_ref, gb_ref, pp_ref, r_ref, xs_hbm, back_ref, zbuf, sem, zsem,
              *, alpha, n_pairs):
    i = pl.program_id(0)
    tm = r_ref.shape[0]
    ta = n_pairs // TOP_K

    for r in range(tm):
        tok = i * tm + r
        for choice, pos_ref in enumerate((p0_ref, p1_ref)):
            pos = pos_ref[tok]
            back_ref[pos] = choice * ta + tok
            dst = pl.ds(pl.multiple_of(pos * CHUNK, CHUNK), CHUNK)
            pltpu.make_async_copy(x1f_ref.at[pl.ds(r * CHUNK, CHUNK)], xs_hbm.at[dst], sem).start()
    x1 = _load_flat(x1f_ref, tm)
    gate = jax.nn.sigmoid(jnp.dot(x1.astype(BF16), gw_ref[...], preferred_element_type=F32) + gb_ref[...])
    emb = jnp.dot(pe_ref[...], pp_ref[...], preferred_element_type=F32)
    r_ref[...] = alpha * x1 + gate * emb

    @pl.when(i == 0)
    def _():
        zbuf[...] = jnp.zeros(zbuf.shape, F32)
        tail = pltpu.make_async_copy(zbuf, xs_hbm.at[pl.ds(n_pairs * CHUNK, zbuf.shape[0])], zsem)
        tail.start()
        tail.wait()

    pltpu.make_async_copy(x1f_ref, xs_hbm.at[pl.ds(0, tm * CHUNK)], sem).wait()
    pltpu.make_async_copy(x1f_ref, xs_hbm.at[pl.ds(0, tm * CHUNK)], sem).wait()


def _ple(pos0, pos1, x1_flat, pe_b, gw_b, gb, pp_b, alpha):
    ta = pos0.shape[0]
    dm = gw_b.shape[0]
    tm = TM_ROW
    tr = TR
    n_pairs = TOP_K * ta
    row = lambda w: pl.BlockSpec((tm, w), lambda i, *_: (i, 0))
    const = lambda a: pl.BlockSpec(a.shape, lambda i, *_: (0,) * a.ndim, pipeline_mode=pl.Buffered(1))
    grid_spec = pltpu.PrefetchScalarGridSpec(
        num_scalar_prefetch=2,
        grid=(ta // tm,),
        in_specs=[pl.BlockSpec((tm * CHUNK, LANE), lambda i, *_: (i, 0)), row(pe_b.shape[1]),
                  const(gw_b), const(gb), const(pp_b)],
        out_specs=[row(dm), pl.BlockSpec(memory_space=pl.ANY), pl.BlockSpec(memory_space=pltpu.SMEM)],
        scratch_shapes=[pltpu.VMEM((tr * CHUNK, LANE), F32), pltpu.SemaphoreType.DMA(()),
                        pltpu.SemaphoreType.DMA(())],
    )
    return pl.pallas_call(
        functools.partial(_ple_body, alpha=alpha, n_pairs=n_pairs),
        grid_spec=grid_spec,
        out_shape=[jax.ShapeDtypeStruct((ta, dm), F32),
                   jax.ShapeDtypeStruct(((n_pairs + tr) * CHUNK, LANE), F32),
                   jax.ShapeDtypeStruct((n_pairs,), I32)],
        compiler_params=_cparams(("arbitrary",)),
        name="ple_dispatch",
    )(pos0, pos1, x1_flat, pe_b, gw_b, gb, pp_b)


def _expert_body(first_ref, ntile_ref, start_ref, nv_ref, dst_ref, x_hbm, wg_ref, wu_ref, wd_ref, y_hbm,
                 xbuf, ybuf, wgb, wub, wdb, gsem, ssem, *, tr):
    e = pl.program_id(0)
    ne = pl.num_programs(0)
    nt = first_ref[ne - 1] + ntile_ref[ne - 1]
    UNROLL = 8

    def get_tile(jj, sl):
        rows = pl.ds(pl.multiple_of(start_ref[jj] * CHUNK, CHUNK), tr * CHUNK)
        return pltpu.make_async_copy(x_hbm.at[rows], xbuf.at[sl], gsem.at[sl])

    def put_row(jj, sl, r):
        stage = pl.ds(pl.multiple_of(r * CHUNK, CHUNK), CHUNK)
        dst = pl.ds(pl.multiple_of(dst_ref[start_ref[jj] + r] * CHUNK, CHUNK), CHUNK)
        return pltpu.make_async_copy(ybuf.at[sl, stage], y_hbm.at[dst], ssem.at[sl])

    def put_rows(jj, sl):
        def block(rb, _):
            for q in range(UNROLL):
                put_row(jj, sl, rb * UNROLL + q).start()
            return 0

        nv = nv_ref[jj]
        full = nv // UNROLL
        lax.fori_loop(0, full, block, 0)
        for q in range(UNROLL - 1):
            @pl.when(full * UNROLL + q < nv)
            def _():
                put_row(jj, sl, full * UNROLL + q).start()

    def put_wait(jj, sl):
        rows = pl.ds(0, nv_ref[jj] * CHUNK)
        pltpu.make_async_copy(ybuf.at[sl, rows], y_hbm.at[rows], ssem.at[sl]).wait()

    @pl.when(e == 0)
    def _():
        get_tile(0, 0).start()

    @pl.when(ntile_ref[e] > 0)
    def _():
        wgb[...] = wg_ref[0].astype(BF16)
        wub[...] = wu_ref[0].astype(BF16)
        wdb[...] = wd_ref[0].astype(BF16)

        def tile(jj, _):
            j = first_ref[e] + jj
            slot = j % 2
            get_tile(j, slot).wait()

            @pl.when(j + 1 < nt)
            def _():
                get_tile(j + 1, 1 - slot).start()

            x = _load_flat(xbuf, tr, (slot,)).astype(BF16)
            hg = jnp.dot(x, wgb[...], preferred_element_type=F32)
            hu = jnp.dot(x, wub[...], preferred_element_type=F32)
            h = (hg * jax.nn.sigmoid(hg) * hu).astype(BF16)
            y = jnp.dot(h, wdb[...], preferred_element_type=F32)

            @pl.when(j >= 2)
            def _():
                put_wait(j - 2, slot)

            _store_flat(ybuf, y, (slot,))
            put_rows(j, slot)
            return 0

        lax.fori_loop(0, ntile_ref[e], tile, 0)

    @pl.when(e == ne - 1)
    def _():
        last = nt - 1
        put_wait(last, last % 2)

        @pl.when(nt >= 2)
        def _():
            put_wait(last - 1, 1 - last % 2)


def _experts(first_tile, n_tile, tile_start, tile_valid, back, xs_flat, w_gate, w_up, w_down):
    n_pairs = back.shape[0]
    ne, dm, ff = w_gate.shape
    tr = TR
    assert dm == CHUNK * LANE and first_tile.shape == (ne,)
    sp = lambda e, *_: (e, 0, 0)
    any_spec = pl.BlockSpec(memory_space=pl.ANY)
    grid_spec = pltpu.PrefetchScalarGridSpec(
        num_scalar_prefetch=5,
        grid=(ne,),
        in_specs=[any_spec,
                  pl.BlockSpec((1, dm, ff), sp), pl.BlockSpec((1, dm, ff), sp), pl.BlockSpec((1, ff, dm), sp)],
        out_specs=any_spec,
        scratch_shapes=[pltpu.VMEM((2, tr * CHUNK, LANE), F32), pltpu.VMEM((2, tr * CHUNK, LANE), F32),
                        pltpu.VMEM((dm, ff), BF16), pltpu.VMEM((dm, ff), BF16), pltpu.VMEM((ff, dm), BF16),
                        pltpu.SemaphoreType.DMA((2,)), pltpu.SemaphoreType.DMA((2,))],
    )
    return pl.pallas_call(
        functools.partial(_expert_body, tr=tr),
        grid_spec=grid_spec,
        out_shape=jax.ShapeDtypeStruct((n_pairs * CHUNK, LANE), F32),
        compiler_params=_cparams(("arbitrary",)),
        name="experts",
    )(first_tile, n_tile, tile_start, tile_valid, back, xs_flat, w_gate, w_up, w_down)


def _final_body(r_ref, y0_ref, y1_ref, wt_ref, g_ref, b_ref, yp_ref, ys_ref, *, n_prompt_tiles):
    i = pl.program_id(0)
    wt = wt_ref[...]
    tm = r_ref.shape[0]
    x = r_ref[...] + wt[:, 0:1] * _load_flat(y0_ref, tm) + wt[:, 1:2] * _load_flat(y1_ref, tm)
    y = _layer_norm(x, g_ref[...], b_ref[...])

    @pl.when(i < n_prompt_tiles)
    def _():
        yp_ref[...] = y

    @pl.when(i >= n_prompt_tiles)
    def _():
        ys_ref[...] = y


def _final(r, y01_flat, wts, ln_g, ln_b, t):
    ta, dm = r.shape
    tm = TM_ROW
    npt = t // tm
    nta = ta // tm
    row = lambda w: pl.BlockSpec((tm, w), lambda i: (i, 0))
    return pl.pallas_call(
        functools.partial(_final_body, n_prompt_tiles=npt),
        grid=(nta,),
        in_specs=[row(dm),
                  pl.BlockSpec((tm * CHUNK, LANE), lambda i: (i, 0)),
                  pl.BlockSpec((tm * CHUNK, LANE), lambda i: (nta + i, 0)),
                  row(LANE), _const_spec(ln_g.shape), _const_spec(ln_b.shape)],
        out_specs=[pl.BlockSpec((tm, dm), lambda i: (jnp.minimum(i, npt - 1), 0)),
                   pl.BlockSpec((tm, dm), lambda i: (0, 0))],
        out_shape=[jax.ShapeDtypeStruct((t, dm), F32), jax.ShapeDtypeStruct((tm, dm), F32)],
        compiler_params=_cparams(("arbitrary",)),
        name="final",
    )(r, y01_flat, y01_flat, wts, ln_g, ln_b)


def _layer(layer, x_p, x_s, pe_p, pe_s, cache_k, cache_v, state, page_table, w):
    t, dm = x_p.shape
    nb = x_s.shape[0]
    wn = N_HEADS * HEAD_W
    ta = t + ROW_PAD
    assert nb <= ROW_PAD and t % ROW_PAD == 0
    lam0 = _lambda_init(layer)
    depth_total = w["depth"]
    alpha = (2.0 * depth_total) ** 0.25
    page = cache_k.shape[1]
    start_pos = page_table.shape[1] * page

    pad = ta - t - nb
    u, qb, kb, kt_p, k_s, vb, vf_p, v_s, gates = _inproj(x_p, x_s, w["w_in"].astype(BF16), ta)

    lamv = jnp.concatenate([w["lam_q1"], w["lam_k1"], w["lam_q2"], w["lam_k2"]], axis=0)
    subln_g = w["subln_g"]

    o_p = _prompt_attn(qb, kb, vb, lamv, subln_g, t, lam0)
    kt_pool = jnp.transpose(cache_k, (0, 2, 3, 4, 1)).reshape(cache_k.shape[0], wn, page)
    v_pool = cache_v.reshape(cache_v.shape[0] * page * N_HEADS, HEAD_W)
    o_s = _decode_attn(page_table, lamv, subln_g, qb[t:t + nb], k_s, v_s, kt_pool, v_pool, lam0)

    d_p = _pool_prompt(u, t)
    d_s = _pool_sample(u[t:t + nb], state, start_pos)

    x1_flat, logits = _mix(d_p, d_s, o_p, o_s, gates, x_p, x_s, w["pool_w"].astype(BF16), w["pool_scale"],
                           w["w_branch_pool"].astype(BF16), w["w_branch_attn"].astype(BF16),
                           w["w_out"].astype(BF16), w["ln1_g"], w["ln1_b"], w["router_w"], w["router_b"], alpha)

    eid, wts = _route(logits)
    pos, meta, emeta = _slots(eid)
    n_row_tiles = (TOP_K * ta) // TR + N_EXPERTS
    assert n_row_tiles <= LANE
    pos0, pos1 = pos[:, 0], pos[:, 1]

    pe_b = jnp.concatenate([pe_p.astype(BF16), pe_s.astype(BF16), jnp.zeros((pad, pe_p.shape[1]), BF16)], axis=0)
    r, xs_flat, back = _ple(pos0, pos1, x1_flat, pe_b, w["ple_gate_w"].astype(BF16), w["ple_gate_b"],
                            w["ple_proj"].astype(BF16), alpha)
    y01_flat = _experts(emeta[0, :N_EXPERTS], emeta[1, :N_EXPERTS], meta[:n_row_tiles, 2], meta[:n_row_tiles, 1],
                        back, xs_flat, w["expert_gate"], w["expert_up"], w["expert_down"])
    y_p, y_s = _final(r, y01_flat, wts, w["ln2_g"], w["ln2_b"], t)
    k_prompt = jnp.transpose(kt_p.reshape(N_HEADS, 2, HEAD_DIM, t), (3, 0, 1, 2))
    v_prompt = vf_p.reshape(t, N_HEADS, HEAD_W)
    return y_p, y_s[:nb], k_prompt, v_prompt, k_s, v_s, u


def kernel(x_prompt, x_sample, p_prompt, p_sample, cache_k, cache_v, state_pool, page_table, w_in, pool_w, pool_scale, lam_q1, lam_k1, lam_q2, lam_k2, subln_g, w_branch_pool, w_branch_attn, w_out, ln1_g, ln1_b, router_group_w, router_group_b, router_expert_w, router_expert_b, expert_gate, expert_up, expert_down, ple_proj, ple_gate_w, ple_gate_b, ln2_g, ln2_b):
    depth = w_in.shape[0]
    bp, t, dm = x_prompt.shape
    nb, ts, _ = x_sample.shape
    assert bp == 1 and ts == 1 and dm // 256 == N_HEADS
    assert dm % CHUNK == 0 and dm // CHUNK == LANE
    hp = x_prompt.reshape(t, dm)
    hs = x_sample.reshape(nb, dm)
    outs = {n: [] for n in ("kp", "vp", "sp", "ks", "vs", "ss")}
    for l in range(depth):
        wr = jnp.concatenate([router_group_w[l],
                              jnp.transpose(router_expert_w[l], (1, 0, 2)).reshape(dm, N_EXPERTS)], axis=1)
        wr = jnp.pad(wr, ((0, 0), (0, LANE - wr.shape[1])))
        wr_hi = wr.astype(BF16)
        wr = jnp.stack([wr_hi, (wr - wr_hi.astype(F32)).astype(BF16)])
        br = jnp.pad(jnp.concatenate([router_group_b[l], router_expert_b[l].reshape(-1)]),
                     (0, LANE - N_GROUPS - N_EXPERTS)).reshape(1, LANE)
        row = lambda a: a[l].reshape(1, -1)
        w = dict(depth=depth, w_in=w_in[l], pool_w=pool_w[l], pool_scale=row(pool_scale),
                 lam_q1=row(lam_q1), lam_k1=row(lam_k1), lam_q2=row(lam_q2), lam_k2=row(lam_k2),
                 subln_g=row(subln_g), w_branch_pool=w_branch_pool[l], w_branch_attn=w_branch_attn[l],
                 w_out=w_out[l], ln1_g=row(ln1_g), ln1_b=row(ln1_b), router_w=wr, router_b=br,
                 expert_gate=expert_gate[l], expert_up=expert_up[l], expert_down=expert_down[l],
                 ple_proj=ple_proj[l], ple_gate_w=ple_gate_w[l], ple_gate_b=row(ple_gate_b),
                 ln2_g=row(ln2_g), ln2_b=row(ln2_b))
        state = state_pool[l]
        hp, hs, k_p, v_p, k_s, v_s, u = _layer(l, hp, hs, p_prompt[l, 0], p_sample[l, :, 0], cache_k[l],
                                               cache_v[l], state, page_table, w)
        outs["kp"].append(k_p[None])
        outs["vp"].append(v_p[None])
        outs["sp"].append(u[t - POOL_STATE:t].reshape(1, POOL_STATE, -1))
        outs["ks"].append(k_s.reshape(nb, 1, N_HEADS, 2, HEAD_DIM))
        outs["vs"].append(v_s.reshape(nb, 1, N_HEADS, HEAD_W))
        outs["ss"].append(jnp.concatenate([state[:, 1:], u[t:t + nb][:, None, :]], axis=1))
    st = lambda n: jnp.stack(outs[n])
    return (hp.reshape(1, t, dm), hs.reshape(nb, 1, dm), st("kp"), st("vp"), st("sp"),
            st("ks"), st("vs"), st("ss"))
```

```python
import functools
import math

import jax
import jax.numpy as jnp
from jax import lax
from jax.experimental import pallas as pl
from jax.experimental.pallas import tpu as pltpu

F32 = jnp.float32
BF16 = jnp.bfloat16
I32 = jnp.int32

N_HEADS = 8
HEAD_DIM = 64
HEAD_W = 2 * HEAD_DIM
ATTN_SCALE = HEAD_DIM ** -0.5
LOG2E = math.log2(math.e)
SUBLN_EPS = 1e-5
LN_EPS = 1e-5
POOL_WINDOWS = (2, 4, 8, 16)
POOL_STATE = max(POOL_WINDOWS) - 1
N_GROUPS = 4
EXPERTS_PER_GROUP = 8
N_EXPERTS = N_GROUPS * EXPERTS_PER_GROUP
TOP_K = 2

LANE = 128
SUBLANE = 8
VMEM_LIMIT = 56 * 1024 * 1024
INPROJ_VMEM_LIMIT = 60 * 1024 * 1024

ROW_PAD = 256
TM_ROW = 256
TM_SORT = 768
TQ = 512
TK = 512
HEADS_PER_STEP = 1
TR = 256
HALO = 16
CHUNK = 16
DECODE_GROUP = 4
DECODE_SLOTS = 12


def _lambda_init(layer):
    return 0.8 - 0.6 * math.exp(-0.3 * layer)


def _cparams(sem, vmem=VMEM_LIMIT):
    return pltpu.CompilerParams(dimension_semantics=sem, vmem_limit_bytes=vmem)


def _const_spec(shape):
    nd = len(shape)
    return pl.BlockSpec(shape, lambda *_: (0,) * nd, pipeline_mode=pl.Buffered(1))


def _inproj_body(xp_ref, xs_ref, w_ref, u_ref, dp_ref, q_ref, kb_ref, kt_ref, ks_ref, vb_ref, vf_ref, vs_ref,
                 g_ref, ext_ref, *, n_prompt_tiles):
    i = pl.program_id(0)
    tm, d = xp_ref.shape
    ns = xs_ref.shape[0]
    wn = u_ref.shape[1]

    @pl.when(i == 0)
    def _():
        ext_ref[pl.ds(0, HALO), :] = jnp.zeros((HALO, wn), F32)

    def project(x, prompt):
        xb = x.astype(BF16)
        col = lambda c: jnp.dot(xb, w_ref[:, c * wn:(c + 1) * wn], preferred_element_type=F32)
        u = col(0)
        u_ref[...] = u
        if prompt:
            ext_ref[pl.ds(HALO, tm), :] = u
            gw = wn // len(POOL_WINDOWS)
            pos = i * tm + lax.broadcasted_iota(I32, (tm, 1), 0)
            for g, w in enumerate(POOL_WINDOWS):
                cols = pl.ds(g * gw, gw)
                tok = ext_ref[pl.ds(HALO, tm), cols]
                s = tok
                for back in range(1, w):
                    s = s + ext_ref[pl.ds(HALO - back, tm), cols]
                inv = 1.0 / jnp.minimum(pos + 1, w).astype(F32)
                dp_ref[:, cols] = (s * inv - tok).astype(dp_ref.dtype)
            ext_ref[pl.ds(0, HALO), :] = ext_ref[pl.ds(tm, HALO), :]
        q_ref[...] = (col(1) * (ATTN_SCALE * LOG2E)).astype(BF16)
        kz = col(2)
        kb_ref[...] = kz.astype(BF16)
        vz = col(3)
        vb_ref[...] = vz.astype(BF16)
        if prompt:
            kt_ref[...] = kz.T
            for h in range(N_HEADS):
                vf_ref[pl.ds(h, tm, stride=N_HEADS), :] = vz[:, h * HEAD_W:(h + 1) * HEAD_W]
        else:
            ks_ref[...] = kz[:ns]
            vs_ref[...] = vz[:ns]
        for c in range(4, w_ref.shape[1] // wn):
            g_ref[:, (c - 4) * wn:(c - 3) * wn] = jax.nn.sigmoid(col(c)).astype(BF16)

    @pl.when(i < n_prompt_tiles)
    def _():
        project(xp_ref[...], True)

    @pl.when(i >= n_prompt_tiles)
    def _():
        project(jnp.concatenate([xs_ref[...], jnp.zeros((tm - ns, d), F32)], axis=0), False)


def _inproj(x_p, x_s, w_in_b, ta):
    t, d = x_p.shape
    ns = x_s.shape[0]
    n_in = w_in_b.shape[1]
    wn = N_HEADS * HEAD_W
    assert n_in == 4 * wn + 2 * d and (2 * d) % wn == 0
    tm = TM_ROW
    assert t % tm == 0 and ta == t + tm and ns <= tm and ns % SUBLANE == 0
    npt = t // tm
    row = lambda w: pl.BlockSpec((tm, w), lambda i: (i, 0))
    prow = lambda i: jnp.minimum(i, npt - 1)
    last = lambda: pl.BlockSpec((ns, wn), lambda i: (0, 0))
    return pl.pallas_call(
        functools.partial(_inproj_body, n_prompt_tiles=npt),
        grid=(ta // tm,),
        in_specs=[pl.BlockSpec((tm, d), lambda i: (prow(i), 0)), _const_spec(x_s.shape), _const_spec(w_in_b.shape)],
        out_specs=[row(wn), pl.BlockSpec((tm, wn), lambda i: (prow(i), 0)), row(wn),
                   row(wn), pl.BlockSpec((wn, tm), lambda i: (0, prow(i))), last(),
                   row(wn), pl.BlockSpec((tm * N_HEADS, HEAD_W), lambda i: (prow(i), 0)), last(),
                   row(2 * d)],
        scratch_shapes=[pltpu.VMEM((tm + HALO, wn), F32)],
        out_shape=[jax.ShapeDtypeStruct((ta, wn), F32),
                   jax.ShapeDtypeStruct((t, wn), BF16),
                   jax.ShapeDtypeStruct((ta, wn), BF16),
                   jax.ShapeDtypeStruct((ta, wn), BF16),
                   jax.ShapeDtypeStruct((wn, t), F32),
                   jax.ShapeDtypeStruct((ns, wn), F32),
                   jax.ShapeDtypeStruct((ta, wn), BF16),
                   jax.ShapeDtypeStruct((t * N_HEADS, HEAD_W), F32),
                   jax.ShapeDtypeStruct((ns, wn), F32),
                   jax.ShapeDtypeStruct((ta, 2 * d), BF16)],
        compiler_params=_cparams(("arbitrary",), INPROJ_VMEM_LIMIT),
        name="inproj",
    )(x_p, x_s, w_in_b)


def _lam_value(lamv_ref, lam0):
    lv = lamv_ref[...]
    a = jnp.sum(lv[0:1] * lv[1:2], axis=1, keepdims=True)
    b = jnp.sum(lv[2:3] * lv[3:4], axis=1, keepdims=True)
    return jnp.exp(a) - jnp.exp(b) + lam0


def _subln(o, g, lam0):
    ms = jnp.mean(o * o, axis=-1, keepdims=True)
    return o * lax.rsqrt(ms + SUBLN_EPS) * g * (1.0 - lam0)


def _prompt_attn_body(lamv_ref, gcol_ref, q_ref, k_ref, v_ref, o_ref, vt_ref, q2t_ref, s_even, s_odd,
                      m_ref, l_ref, acc_ref, *, tq, tk, hp, lam0):
    i = pl.program_id(1)
    nkv = v_ref.shape[0] // tk
    heads = [pl.ds(hh * HEAD_W, HEAD_W) for hh in range(hp)]

    @pl.when(i == 0)
    def _():
        def tr(jb, _):
            for hh in range(hp):
                blk = v_ref[pl.ds(pl.multiple_of(jb * tk, tk), tk), heads[hh]].astype(F32)
                vt_ref[hh, jb] = blk.T.astype(BF16)
            return 0
        lax.fori_loop(0, nkv, tr, 0)

    drow = lax.broadcasted_iota(I32, (HEAD_W, tq), 0)
    for hh in range(hp):
        qt = q_ref[:, heads[hh]].astype(F32).T
        q2t_ref[hh] = jnp.concatenate([jnp.where(drow < HEAD_DIM, qt, 0.0),
                                       jnp.where(drow >= HEAD_DIM, qt, 0.0)], axis=1).astype(BF16)
    m_ref[...] = jnp.full(m_ref.shape, -jnp.inf, F32)
    l_ref[...] = jnp.zeros(l_ref.shape, F32)
    acc_ref[...] = jnp.zeros(acc_ref.shape, F32)
    n_full = (i * tq) // tk

    def scores(j, s_ref):
        rows = pl.ds(pl.multiple_of(j * tk, tk), tk)
        for hh in range(hp):
            s_ref[hh] = jnp.dot(k_ref[rows, heads[hh]], q2t_ref[hh], preferred_element_type=F32)

    def absorb(j, s_ref, masked):
        for hh in range(hp):
            s = s_ref[hh]
            if masked:
                kpos = n_full * tk + lax.broadcasted_iota(I32, (tk, 2 * tq), 0)
                qcol = lax.broadcasted_iota(I32, (tk, 2 * tq), 1)
                qpos = i * tq + jnp.where(qcol >= tq, qcol - tq, qcol)
                s = jnp.where(kpos <= qpos, s, -jnp.inf)
            m = m_ref[hh]
            m_new = jnp.maximum(m, jnp.max(s, axis=0, keepdims=True))
            alpha = jnp.exp2(m - m_new)
            p = jnp.exp2(s - m_new)
            l_ref[hh] = alpha * l_ref[hh] + jnp.sum(p, axis=0, keepdims=True)
            acc_ref[hh] = alpha * acc_ref[hh] + jnp.dot(vt_ref[hh, j], p.astype(BF16), preferred_element_type=F32)
            m_ref[hh] = m_new

    def by_parity(j, fn):
        @pl.when(j % 2 == 0)
        def _():
            fn(s_even, s_odd)

        @pl.when(j % 2 == 1)
        def _():
            fn(s_odd, s_even)

    scores(0, s_even)

    def body(j, _):
        def stage(cur, nxt):
            scores(j + 1, nxt)
            absorb(j, cur, False)
        by_parity(j, stage)
        return 0

    lax.fori_loop(0, n_full, body, 0)
    by_parity(n_full, lambda cur, nxt: absorb(n_full, cur, True))

    lam = _lam_value(lamv_ref, lam0)
    for hh in range(hp):
        on = acc_ref[hh] / l_ref[hh]
        ot = on[:, :tq] - lam * on[:, tq:]
        ms = jnp.mean(ot * ot, axis=0, keepdims=True)
        ot = ot * lax.rsqrt(ms + SUBLN_EPS) * gcol_ref[...] * (1.0 - lam0)
        o_ref[:, heads[hh]] = ot.T.astype(o_ref.dtype)


def _prompt_attn(qb, kb, vb, lamv, subln_g, t, lam0):
    tq, tk, hp = TQ, TK, HEADS_PER_STEP
    assert t % tk == 0 and tk % tq == 0 and N_HEADS % hp == 0
    kern = functools.partial(_prompt_attn_body, tq=tq, tk=tk, hp=hp, lam0=lam0)
    gcol = subln_g.reshape(HEAD_W, 1)
    return pl.pallas_call(
        kern,
        grid=(N_HEADS // hp, t // tq),
        in_specs=[_const_spec(lamv.shape), _const_spec(gcol.shape),
                  pl.BlockSpec((tq, hp * HEAD_W), lambda h, i: (i, h)),
                  pl.BlockSpec((t, hp * HEAD_W), lambda h, i: (0, h)),
                  pl.BlockSpec((t, hp * HEAD_W), lambda h, i: (0, h))],
        out_specs=pl.BlockSpec((tq, hp * HEAD_W), lambda h, i: (i, h)),
        out_shape=jax.ShapeDtypeStruct((t, N_HEADS * HEAD_W), BF16),
        scratch_shapes=[pltpu.VMEM((hp, t // tk, HEAD_W, tk), BF16),
                        pltpu.VMEM((hp, HEAD_W, 2 * tq), BF16),
                        pltpu.VMEM((hp, tk, 2 * tq), F32), pltpu.VMEM((hp, tk, 2 * tq), F32),
                        pltpu.VMEM((hp, 1, 2 * tq), F32), pltpu.VMEM((hp, 1, 2 * tq), F32),
                        pltpu.VMEM((hp, HEAD_W, 2 * tq), F32)],
        compiler_params=_cparams(("arbitrary", "arbitrary")),
        name="prompt_attn",
    )(lamv, gcol, qb, kb, vb)


def _decode_attn_body(pt_ref, lamv_ref, g_ref, q_ref, qc_ref, kn_ref, vn_ref, kt_hbm, v_hbm, o_ref,
                      kbuf, vbuf, sem, *, n_pages, page, n_slots, group, lam0):
    b = pl.program_id(0)
    nb = pl.num_programs(0)
    nr = 2 * N_HEADS
    wn = N_HEADS * HEAD_W
    vrows = page * N_HEADS
    ahead = n_slots - group
    total = nb * n_pages
    assert n_pages % group == 0 and n_slots % group == 0 and ahead >= group

    def copies(g):
        slot = g % n_slots
        pg = pt_ref[g // n_pages, g % n_pages]
        ck = pltpu.make_async_copy(kt_hbm.at[pg], kbuf.at[slot], sem.at[0, slot])
        cv = pltpu.make_async_copy(v_hbm.at[pl.ds(pl.multiple_of(pg * vrows, vrows), vrows)],
                                   vbuf.at[slot], sem.at[1, slot])
        return ck, cv

    def start(g):
        for c in copies(g):
            c.start()

    @pl.when(b == 0)
    def _():
        for g in range(ahead):
            start(g)

    rr = lax.broadcasted_iota(I32, (nr, wn), 0)
    cc = lax.broadcasted_iota(I32, (nr, wn), 1)
    qrow = q_ref[0]
    qbd = jnp.where(cc // HEAD_DIM == rr, jnp.broadcast_to(qrow.astype(F32), (nr, wn)), 0.0).astype(BF16)
    rhead = lax.broadcasted_iota(I32, (nr, HEAD_W), 0) // 2
    qcol = jnp.broadcast_to(qc_ref[0], (wn, page))
    tok = lax.broadcasted_iota(I32, (page, vrows), 0)
    vrow = lax.broadcasted_iota(I32, (page, vrows), 1)
    spread = jnp.where(vrow // N_HEADS == tok, 1.0, 0.0).astype(BF16)
    own_head = lax.broadcasted_iota(I32, (nr, vrows), 1) % N_HEADS == lax.broadcasted_iota(I32, (nr, vrows), 0) // 2

    def body(jg, carry):
        m, l, acc = carry
        g0 = b * n_pages + jg * group
        slots = []
        ss = []
        for a in range(group):
            g = g0 + a

            @pl.when(g + ahead < total)
            def _():
                start(g + ahead)

            for c in copies(g):
                c.wait()
            slot = g % n_slots
            slots.append(slot)
            prod = kbuf[slot] * qcol
            ss.append(jnp.sum(prod.reshape(nr, HEAD_DIM, page), axis=1))
        m_new = m
        for s in ss:
            m_new = jnp.maximum(m_new, jnp.max(s, axis=1, keepdims=True))
        alpha = jnp.exp2(m - m_new)
        l = alpha * l
        acc = alpha * acc
        for s, slot in zip(ss, slots):
            p = jnp.exp2(s - m_new)
            l = l + jnp.sum(p, axis=1, keepdims=True)
            pe = jnp.dot(p.astype(BF16), spread, preferred_element_type=F32)
            pe = jnp.where(own_head, pe, 0.0).astype(BF16)
            acc = acc + jnp.dot(pe, vbuf[slot].astype(BF16), preferred_element_type=F32)
        return m_new, l, acc

    init = (jnp.full((nr, 1), -jnp.inf, F32), jnp.zeros((nr, 1), F32), jnp.zeros((nr, HEAD_W), F32))
    m, l, acc = lax.fori_loop(0, n_pages // group, body, init)

    kn = jnp.broadcast_to(kn_ref[0], (nr, wn))
    s_new = jnp.sum(qbd.astype(F32) * kn, axis=1, keepdims=True)
    vn = vn_ref[0]
    vnew = jnp.zeros((nr, HEAD_W), F32)
    for h in range(N_HEADS):
        vnew = vnew + jnp.where(rhead == h, jnp.broadcast_to(vn[:, h * HEAD_W:(h + 1) * HEAD_W], (nr, HEAD_W)), 0.0)
    m_new = jnp.maximum(m, s_new)
    alpha = jnp.exp2(m - m_new)
    p_new = jnp.exp2(s_new - m_new)
    l = alpha * l + p_new
    acc = alpha * acc + p_new * vnew
    on = acc / l
    lam = _lam_value(lamv_ref, lam0)
    o = on - lam * pltpu.roll(on, nr - 1, 0)
    o_ref[0] = _subln(o, g_ref[...], lam0)


def _decode_attn(page_table, lamv, subln_g, q_s, k_s, v_s, kt_pool, v_pool, lam0):
    nb, n_pages = page_table.shape
    wn = N_HEADS * HEAD_W
    page = kt_pool.shape[2]
    nr = 2 * N_HEADS
    n_slots = DECODE_SLOTS
    kern = functools.partial(_decode_attn_body, n_pages=n_pages, page=page, n_slots=n_slots,
                             group=DECODE_GROUP, lam0=lam0)
    grid_spec = pltpu.PrefetchScalarGridSpec(
        num_scalar_prefetch=1,
        grid=(nb,),
        in_specs=[pl.BlockSpec(lamv.shape, lambda b, pt: (0, 0)),
                  pl.BlockSpec(subln_g.shape, lambda b, pt: (0, 0)),
                  pl.BlockSpec((1, 1, wn), lambda b, pt: (b, 0, 0)),
                  pl.BlockSpec((1, wn, 1), lambda b, pt: (b, 0, 0)),
                  pl.BlockSpec((1, 1, wn), lambda b, pt: (b, 0, 0)),
                  pl.BlockSpec((1, 1, wn), lambda b, pt: (b, 0, 0)),
                  pl.BlockSpec(memory_space=pl.ANY),
                  pl.BlockSpec(memory_space=pl.ANY)],
        out_specs=pl.BlockSpec((1, nr, HEAD_W), lambda b, pt: (b, 0, 0)),
        scratch_shapes=[pltpu.VMEM((n_slots, wn, page), F32),
                        pltpu.VMEM((n_slots, page * N_HEADS, HEAD_W), F32),
                        pltpu.SemaphoreType.DMA((2, n_slots))],
    )
    out = pl.pallas_call(
        kern,
        grid_spec=grid_spec,
        out_shape=jax.ShapeDtypeStruct((nb, nr, HEAD_W), F32),
        compiler_params=_cparams(("arbitrary",)),
        name="decode_attn",
    )(page_table, lamv, subln_g, q_s.reshape(nb, 1, wn), q_s.astype(F32).reshape(nb, wn, 1),
      k_s.reshape(nb, 1, wn), v_s.reshape(nb, 1, wn), kt_pool, v_pool)
    return out[:, 0::2, :].reshape(nb, wn)


def _pool_sample_body(u_ref, st_ref, d_ref, *, gw, start_pos):
    st = st_ref[...]
    u = u_ref[...]
    ridx = lax.broadcasted_iota(I32, st.shape, 1)
    for g, w in enumerate(POOL_WINDOWS):
        hist = jnp.sum(jnp.where(ridx >= POOL_STATE - (w - 1), st, 0.0), axis=1)
        cnt = float(min(start_pos + 1, w))
        d = (hist + u) * (1.0 / cnt) - u
        d_ref[:, g * gw:(g + 1) * gw] = d[:, g * gw:(g + 1) * gw].astype(d_ref.dtype)


def _pool_sample(u_s, state, start_pos):
    nb, wp = u_s.shape
    gw = wp // len(POOL_WINDOWS)
    kern = functools.partial(_pool_sample_body, gw=gw, start_pos=start_pos)
    return pl.pallas_call(
        kern,
        grid=(1,),
        in_specs=[_const_spec(u_s.shape), _const_spec(state.shape)],
        out_specs=_const_spec((nb, wp)),
        out_shape=jax.ShapeDtypeStruct((nb, wp), BF16),
        compiler_params=_cparams(("arbitrary",)),
        name="pool_sample",
    )(u_s, state)


def _store_flat(flat_ref, x, lead=()):
    rows = x.shape[0]
    for c in range(CHUNK):
        flat_ref[(*lead, pl.ds(c, rows, stride=CHUNK), slice(None))] = x[:, c * LANE:(c + 1) * LANE]


def _load_flat(flat_ref, rows, lead=(), first=0):
    return jnp.concatenate(
        [flat_ref[(*lead, pl.ds(first * CHUNK + c, rows, stride=CHUNK), slice(None))] for c in range(CHUNK)],
        axis=1)


def _layer_norm(x, g, b):
    mu = jnp.mean(x, axis=-1, keepdims=True)
    xc = x - mu
    var = jnp.mean(xc * xc, axis=-1, keepdims=True)
    return xc * lax.rsqrt(var + LN_EPS) * g + b


def _mix_body(dp_ref, ds_ref, op_ref, os_ref, g_ref, xp_ref, xs_ref, pw_ref, ps_ref, wbp_ref, wba_ref, wo_ref,
              lg_ref, lb_ref, wr_ref, br_ref, x1f_ref, lo_ref, d_ref, o_ref, *, n_prompt_tiles, alpha):
    i = pl.program_id(0)
    tm, dm = xp_ref.shape
    ns = xs_ref.shape[0]
    ng = len(POOL_WINDOWS)
    gw = d_ref.shape[1] // ng

    @pl.when(i < n_prompt_tiles)
    def _():
        d_ref[...] = dp_ref[...]
        o_ref[...] = op_ref[...]

    @pl.when(i >= n_prompt_tiles)
    def _():
        d_ref[...] = jnp.zeros(d_ref.shape, d_ref.dtype)
        o_ref[...] = jnp.zeros(o_ref.shape, o_ref.dtype)
        d_ref[0:ns, :] = ds_ref[...]
        o_ref[0:ns, :] = os_ref[...].astype(o_ref.dtype)

    ys = []
    for g in range(ng):
        ys.append(jnp.dot(d_ref[:, g * gw:(g + 1) * gw], pw_ref[g], preferred_element_type=F32))
    pool_y = (jnp.concatenate(ys, axis=1) * ps_ref[...]).astype(BF16)
    a = jnp.dot(pool_y, wbp_ref[...], preferred_element_type=F32)
    bb = jnp.dot(o_ref[...], wba_ref[...], preferred_element_type=F32)
    gates = g_ref[...].astype(F32)
    merged = (gates[:, :dm] * a + gates[:, dm:] * bb).astype(BF16)
    mix = jnp.dot(merged, wo_ref[...], preferred_element_type=F32)

    def finish(x):
        x1 = _layer_norm(alpha * x + mix, lg_ref[...], lb_ref[...])
        _store_flat(x1f_ref, x1)
        hi = x1.astype(BF16)
        lo = (x1 - hi.astype(F32)).astype(BF16)
        dot = functools.partial(jnp.dot, preferred_element_type=F32)
        lo_ref[...] = dot(hi, wr_ref[0]) + (dot(lo, wr_ref[0]) + dot(hi, wr_ref[1])) + br_ref[...]

    @pl.when(i < n_prompt_tiles)
    def _():
        finish(xp_ref[...])

    @pl.when(i >= n_prompt_tiles)
    def _():
        ns = xs_ref.shape[0]
        finish(jnp.concatenate([xs_ref[...], jnp.zeros((tm - ns, dm), F32)], axis=0))


def _mix(d_p, d_s, o_p, o_s, gates, x_p, x_s, pool_w_b, pool_scale, wbp, wba, wo, ln_g, ln_b, wr, br, alpha):
    ta = gates.shape[0]
    t, dm = x_p.shape
    ns = x_s.shape[0]
    tm = TM_ROW
    assert ta % tm == 0 and t % tm == 0 and ns % (2 * SUBLANE) == 0 and ns <= tm and ta == t + tm
    npt = t // tm
    wp = d_p.shape[1]
    wa = o_p.shape[1]
    row = lambda w: pl.BlockSpec((tm, w), lambda i: (i, 0))
    prow = lambda w: pl.BlockSpec((tm, w), lambda i: (jnp.minimum(i, npt - 1), 0))
    kern = functools.partial(_mix_body, n_prompt_tiles=npt, alpha=alpha)
    return pl.pallas_call(
        kern,
        grid=(ta // tm,),
        in_specs=[prow(wp), _const_spec(d_s.shape), prow(wa), _const_spec(o_s.shape), row(2 * dm),
                  prow(dm),
                  _const_spec(x_s.shape),
                  _const_spec(pool_w_b.shape), _const_spec(pool_scale.shape),
                  _const_spec(wbp.shape), _const_spec(wba.shape), _const_spec(wo.shape),
                  _const_spec(ln_g.shape), _const_spec(ln_b.shape),
                  _const_spec(wr.shape), _const_spec(br.shape)],
        out_specs=[pl.BlockSpec((tm * CHUNK, LANE), lambda i: (i, 0)), row(LANE)],
        out_shape=[jax.ShapeDtypeStruct((ta * CHUNK, LANE), F32),
                   jax.ShapeDtypeStruct((ta, LANE), F32)],
        scratch_shapes=[pltpu.VMEM((tm, wp), BF16), pltpu.VMEM((tm, wa), BF16)],
        compiler_params=_cparams(("arbitrary",)),
        name="branch_mix",
    )(d_p, d_s, o_p, o_s, gates, x_p, x_s, pool_w_b, pool_scale, wbp, wba, wo, ln_g, ln_b, wr, br)


def _route_body(lo_ref, eid_ref, wt_ref):
    lo = lo_ref[...]
    lane_i = lax.broadcasted_iota(I32, lo.shape, 1)
    lane = lane_i.astype(F32)
    ninf = -jnp.inf
    big = float(LANE)
    is_g = lane_i < N_GROUPS
    gl = jnp.where(is_g, lo, ninf)
    gmax = jnp.max(gl, axis=1, keepdims=True)
    gidx = jnp.min(jnp.where(gl == gmax, lane, big), axis=1, keepdims=True)
    gsum = jnp.sum(jnp.where(is_g, jnp.exp(gl - gmax), 0.0), axis=1, keepdims=True)
    gw = 1.0 / gsum
    lo_e = N_GROUPS + EXPERTS_PER_GROUP * gidx
    el = jnp.where(jnp.logical_and(lane >= lo_e, lane < lo_e + EXPERTS_PER_GROUP), lo, ninf)
    v1 = jnp.max(el, axis=1, keepdims=True)
    i1 = jnp.min(jnp.where(el == v1, lane, big), axis=1, keepdims=True)
    el2 = jnp.where(lane == i1, ninf, el)
    v2 = jnp.max(el2, axis=1, keepdims=True)
    i2 = jnp.min(jnp.where(el2 == v2, lane, big), axis=1, keepdims=True)
    e21 = jnp.exp(v2 - v1)
    w1 = gw / (1.0 + e21)
    w2 = gw * e21 / (1.0 + e21)
    eid = jnp.where(lane_i == 0, i1 - N_GROUPS, jnp.where(lane_i == 1, i2 - N_GROUPS, 0.0))
    eid_ref[...] = eid.astype(I32)
    wt_ref[...] = jnp.where(lane_i == 0, w1, jnp.where(lane_i == 1, w2, 0.0))


def _route(logits):
    ta = logits.shape[0]
    tm = TM_SORT
    assert ta % tm == 0
    row = pl.BlockSpec((tm, LANE), lambda i: (i, 0))
    return pl.pallas_call(
        _route_body,
        grid=(ta // tm,),
        in_specs=[row],
        out_specs=[row, row],
        out_shape=[jax.ShapeDtypeStruct((ta, LANE), I32), jax.ShapeDtypeStruct((ta, LANE), F32)],
        compiler_params=_cparams(("arbitrary",)),
        name="route",
    )(logits)


def _slots_body(eid_ref, pos_ref, meta_ref, emeta_ref, cnt_ref, run_ref, off_ref, *, tm, tr):
    ph = pl.program_id(0)
    i = pl.program_id(1)
    eid = eid_ref[...]
    lane = lax.broadcasted_iota(I32, (tm, LANE), 1)
    e0 = eid[:, 0:1]
    e1 = eid[:, 1:2]
    oh0 = (lane == e0).astype(F32)
    oh1 = (lane == e1).astype(F32)
    both = oh0 + oh1

    @pl.when(jnp.logical_and(ph == 0, i == 0))
    def _():
        cnt_ref[...] = jnp.zeros_like(cnt_ref)

    @pl.when(ph == 0)
    def _():
        cnt_ref[...] = cnt_ref[...] + jnp.sum(both, axis=0, keepdims=True)

    @pl.when(jnp.logical_and(ph == 1, i == 0))
    def _():
        cnt = cnt_ref[...]
        tiles = jnp.floor((cnt + (tr - 1)) * (1.0 / tr))
        a = lax.broadcasted_iota(I32, (LANE, LANE), 0)
        b = lax.broadcasted_iota(I32, (LANE, LANE), 1)
        upper = (a < b).astype(BF16)
        prefix = lambda v: jnp.dot(jnp.broadcast_to(v, (SUBLANE, LANE)).astype(BF16), upper,
                                   preferred_element_type=F32)[0:1]
        off_tiles = prefix(tiles)
        cnt_hi = jnp.floor(cnt * (1.0 / 256.0))
        off_rows = 256.0 * prefix(cnt_hi) + prefix(cnt - 256.0 * cnt_hi)
        off_ref[...] = off_rows
        run_ref[...] = jnp.zeros_like(run_ref)
        end_tiles = off_tiles + tiles
        lane1 = lax.broadcasted_iota(I32, (LANE, LANE), 1)
        jrow = lax.broadcasted_iota(I32, (LANE, LANE), 0).astype(F32)
        done = jnp.logical_and(lane1 < N_EXPERTS, jnp.broadcast_to(end_tiles, (LANE, LANE)) <= jrow)
        te = jnp.sum(done.astype(F32), axis=1, keepdims=True)
        owns = jnp.logical_and(jnp.broadcast_to(off_tiles, (LANE, LANE)) <= jrow,
                               jrow < jnp.broadcast_to(end_tiles, (LANE, LANE)))
        left = jnp.broadcast_to(cnt, (LANE, LANE)) - (jrow - jnp.broadcast_to(off_tiles, (LANE, LANE))) * tr
        nv = jnp.sum(jnp.where(owns, jnp.minimum(left, float(tr)), 0.0), axis=1, keepdims=True)
        first = jnp.broadcast_to(off_rows, (LANE, LANE)) + (jrow - jnp.broadcast_to(off_tiles, (LANE, LANE))) * tr
        start = jnp.sum(jnp.where(owns, first, 0.0), axis=1, keepdims=True)
        meta_ref[...] = jnp.where(lane1 == 0, jnp.broadcast_to(te, (LANE, LANE)),
                                  jnp.where(lane1 == 1, jnp.broadcast_to(nv, (LANE, LANE)),
                                            jnp.broadcast_to(start, (LANE, LANE)))).astype(I32)
        erow = lax.broadcasted_iota(I32, (SUBLANE, LANE), 0)
        emeta_ref[...] = jnp.where(erow == 0, jnp.broadcast_to(off_tiles, (SUBLANE, LANE)),
                                   jnp.broadcast_to(tiles, (SUBLANE, LANE))).astype(I32)

    @pl.when(ph == 1)
    def _():
        r = lax.broadcasted_iota(I32, (tm, tm), 0)
        c = lax.broadcasted_iota(I32, (tm, tm), 1)
        lower = (c < r).astype(BF16)
        before = jnp.dot(lower, both.astype(BF16), preferred_element_type=F32) + run_ref[...] + off_ref[...]
        p0 = jnp.sum(oh0 * before, axis=1, keepdims=True)
        p1 = jnp.sum(oh1 * before, axis=1, keepdims=True)
        pos_ref[...] = jnp.where(lane == 0, p0, jnp.where(lane == 1, p1, 0.0)).astype(I32)
        run_ref[...] = run_ref[...] + jnp.sum(both, axis=0, keepdims=True)


def _slots(eid):
    ta = eid.shape[0]
    tm = TM_SORT
    assert ta % tm == 0
    kern = functools.partial(_slots_body, tm=tm, tr=TR)
    return pl.pallas_call(
        kern,
        grid=(2, ta // tm),
        in_specs=[pl.BlockSpec((tm, LANE), lambda ph, i: (i, 0))],
        out_specs=[pl.BlockSpec((tm, LANE), lambda ph, i: (ph * i, 0)),
                   pl.BlockSpec((LANE, LANE), lambda ph, i: (0, 0)),
                   pl.BlockSpec((SUBLANE, LANE), lambda ph, i: (0, 0))],
        out_shape=[jax.ShapeDtypeStruct((ta, LANE), I32),
                   jax.ShapeDtypeStruct((LANE, LANE), I32),
                   jax.ShapeDtypeStruct((SUBLANE, LANE), I32)],
        scratch_shapes=[pltpu.VMEM((1, LANE), F32), pltpu.VMEM((1, LANE), F32), pltpu.VMEM((1, LANE), F32)],
        compiler_params=_cparams(("arbitrary", "arbitrary")),
        name="slots",
    )(eid)


def _ple_body(p0_ref, p1_ref, x1f_ref, pe_ref, gw_ref, gb_ref, pp_ref, r_ref, xs_hbm, back_ref, zbuf, sem, zsem,
              *, alpha, n_pairs):
    i = pl.program_id(0)
    tm = r_ref.shape[0]
    ta = n_pairs // TOP_K

    for r in range(tm):
        tok = i * tm + r
        for choice, pos_ref in enumerate((p0_ref, p1_ref)):
            pos = pos_ref[tok]
            back_ref[pos] = choice * ta + tok
            dst = pl.ds(pl.multiple_of(pos * CHUNK, CHUNK), CHUNK)
            pltpu.make_async_copy(x1f_ref.at[pl.ds(r * CHUNK, CHUNK)], xs_hbm.at[dst], sem).start()
    x1 = _load_flat(x1f_ref, tm)
    gate = jax.nn.sigmoid(jnp.dot(x1.astype(BF16), gw_ref[...], preferred_element_type=F32) + gb_ref[...])
    emb = jnp.dot(pe_ref[...], pp_ref[...], preferred_element_type=F32)
    r_ref[...] = alpha * x1 + gate * emb

    @pl.when(i == 0)
    def _():
        zbuf[...] = jnp.zeros(zbuf.shape, F32)
        tail = pltpu.make_async_copy(zbuf, xs_hbm.at[pl.ds(n_pairs * CHUNK, zbuf.shape[0])], zsem)
        tail.start()
        tail.wait()

    pltpu.make_async_copy(x1f_ref, xs_hbm.at[pl.ds(0, tm * CHUNK)], sem).wait()
    pltpu.make_async_copy(x1f_ref, xs_hbm.at[pl.ds(0, tm * CHUNK)], sem).wait()


def _ple(pos0, pos1, x1_flat, pe_b, gw_b, gb, pp_b, alpha):
    ta = pos0.shape[0]
    dm = gw_b.shape[0]
    tm = TM_ROW
    tr = TR
    n_pairs = TOP_K * ta
    row = lambda w: pl.BlockSpec((tm, w), lambda i, *_: (i, 0))
    const = lambda a: pl.BlockSpec(a.shape, lambda i, *_: (0,) * a.ndim, pipeline_mode=pl.Buffered(1))
    grid_spec = pltpu.PrefetchScalarGridSpec(
        num_scalar_prefetch=2,
        grid=(ta // tm,),
        in_specs=[pl.BlockSpec((tm * CHUNK, LANE), lambda i, *_: (i, 0)), row(pe_b.shape[1]),
                  const(gw_b), const(gb), const(pp_b)],
        out_specs=[row(dm), pl.BlockSpec(memory_space=pl.ANY), pl.BlockSpec(memory_space=pltpu.SMEM)],
        scratch_shapes=[pltpu.VMEM((tr * CHUNK, LANE), F32), pltpu.SemaphoreType.DMA(()),
                        pltpu.SemaphoreType.DMA(())],
    )
    return pl.pallas_call(
        functools.partial(_ple_body, alpha=alpha, n_pairs=n_pairs),
        grid_spec=grid_spec,
        out_shape=[jax.ShapeDtypeStruct((ta, dm), F32),
                   jax.ShapeDtypeStruct(((n_pairs + tr) * CHUNK, LANE), F32),
                   jax.ShapeDtypeStruct((n_pairs,), I32)],
        compiler_params=_cparams(("arbitrary",)),
        name="ple_dispatch",
    )(pos0, pos1, x1_flat, pe_b, gw_b, gb, pp_b)


def _expert_body(first_ref, ntile_ref, start_ref, nv_ref, dst_ref, x_hbm, wg_ref, wu_ref, wd_ref, y_hbm,
                 xbuf, ybuf, wgb, wub, wdb, gsem, ssem, *, tr):
    e = pl.program_id(0)
    ne = pl.num_programs(0)
    nt = first_ref[ne - 1] + ntile_ref[ne - 1]
    UNROLL = 8

    def get_tile(jj, sl):
        rows = pl.ds(pl.multiple_of(start_ref[jj] * CHUNK, CHUNK), tr * CHUNK)
        return pltpu.make_async_copy(x_hbm.at[rows], xbuf.at[sl], gsem.at[sl])

    def put_row(jj, sl, r):
        stage = pl.ds(pl.multiple_of(r * CHUNK, CHUNK), CHUNK)
        dst = pl.ds(pl.multiple_of(dst_ref[start_ref[jj] + r] * CHUNK, CHUNK), CHUNK)
        return pltpu.make_async_copy(ybuf.at[sl, stage], y_hbm.at[dst], ssem.at[sl])

    def put_rows(jj, sl):
        def block(rb, _):
            for q in range(UNROLL):
                put_row(jj, sl, rb * UNROLL + q).start()
            return 0

        nv = nv_ref[jj]
        full = nv // UNROLL
        lax.fori_loop(0, full, block, 0)
        for q in range(UNROLL - 1):
            @pl.when(full * UNROLL + q < nv)
            def _():
                put_row(jj, sl, full * UNROLL + q).start()

    def put_wait(jj, sl):
        rows = pl.ds(0, nv_ref[jj] * CHUNK)
        pltpu.make_async_copy(ybuf.at[sl, rows], y_hbm.at[rows], ssem.at[sl]).wait()

    @pl.when(e == 0)
    def _():
        get_tile(0, 0).start()

    @pl.when(ntile_ref[e] > 0)
    def _():
        wgb[...] = wg_ref[0].astype(BF16)
        wub[...] = wu_ref[0].astype(BF16)
        wdb[...] = wd_ref[0].astype(BF16)

        def tile(jj, _):
            j = first_ref[e] + jj
            slot = j % 2
            get_tile(j, slot).wait()

            @pl.when(j + 1 < nt)
            def _():
                get_tile(j + 1, 1 - slot).start()

            x = _load_flat(xbuf, tr, (slot,)).astype(BF16)
            hg = jnp.dot(x, wgb[...], preferred_element_type=F32)
            hu = jnp.dot(x, wub[...], preferred_element_type=F32)
            h = (hg * jax.nn.sigmoid(hg) * hu).astype(BF16)
            y = jnp.dot(h, wdb[...], preferred_element_type=F32)

            @pl.when(j >= 2)
            def _():
                put_wait(j - 2, slot)

            _store_flat(ybuf, y, (slot,))
            put_rows(j, slot)
            return 0

        lax.fori_loop(0, ntile_ref[e], tile, 0)

    @pl.when(e == ne - 1)
    def _():
        last = nt - 1
        put_wait(last, last % 2)

        @pl.when(nt >= 2)
        def _():
            put_wait(last - 1, 1 - last % 2)


def _experts(first_tile, n_tile, tile_start, tile_valid, back, xs_flat, w_gate, w_up, w_down):
    n_pairs = back.shape[0]
    ne, dm, ff = w_gate.shape
    tr = TR
    assert dm == CHUNK * LANE and first_tile.shape == (ne,)
    sp = lambda e, *_: (e, 0, 0)
    any_spec = pl.BlockSpec(memory_space=pl.ANY)
    grid_spec = pltpu.PrefetchScalarGridSpec(
        num_scalar_prefetch=5,
        grid=(ne,),
        in_specs=[any_spec,
                  pl.BlockSpec((1, dm, ff), sp), pl.BlockSpec((1, dm, ff), sp), pl.BlockSpec((1, ff, dm), sp)],
        out_specs=any_spec,
        scratch_shapes=[pltpu.VMEM((2, tr * CHUNK, LANE), F32), pltpu.VMEM((2, tr * CHUNK, LANE), F32),
                        pltpu.VMEM((dm, ff), BF16), pltpu.VMEM((dm, ff), BF16), pltpu.VMEM((ff, dm), BF16),
                        pltpu.SemaphoreType.DMA((2,)), pltpu.SemaphoreType.DMA((2,))],
    )
    return pl.pallas_call(
        functools.partial(_expert_body, tr=tr),
        grid_spec=grid_spec,
        out_shape=jax.ShapeDtypeStruct((n_pairs * CHUNK, LANE), F32),
        compiler_params=_cparams(("arbitrary",)),
        name="experts",
    )(first_tile, n_tile, tile_start, tile_valid, back, xs_flat, w_gate, w_up, w_down)


def _final_body(r_ref, y0_ref, y1_ref, wt_ref, g_ref, b_ref, yp_ref, ys_ref, *, n_prompt_tiles):
    i = pl.program_id(0)
    wt = wt_ref[...]
    tm = r_ref.shape[0]
    x = r_ref[...] + wt[:, 0:1] * _load_flat(y0_ref, tm) + wt[:, 1:2] * _load_flat(y1_ref, tm)
    y = _layer_norm(x, g_ref[...], b_ref[...])

    @pl.when(i < n_prompt_tiles)
    def _():
        yp_ref[...] = y

    @pl.when(i >= n_prompt_tiles)
    def _():
        ys_ref[...] = y


def _final(r, y01_flat, wts, ln_g, ln_b, t):
    ta, dm = r.shape
    tm = TM_ROW
    npt = t // tm
    nta = ta // tm
    row = lambda w: pl.BlockSpec((tm, w), lambda i: (i, 0))
    return pl.pallas_call(
        functools.partial(_final_body, n_prompt_tiles=npt),
        grid=(nta,),
        in_specs=[row(dm),
                  pl.BlockSpec((tm * CHUNK, LANE), lambda i: (i, 0)),
                  pl.BlockSpec((tm * CHUNK, LANE), lambda i: (nta + i, 0)),
                  row(LANE), _const_spec(ln_g.shape), _const_spec(ln_b.shape)],
        out_specs=[pl.BlockSpec((tm, dm), lambda i: (jnp.minimum(i, npt - 1), 0)),
                   pl.BlockSpec((tm, dm), lambda i: (0, 0))],
        out_shape=[jax.ShapeDtypeStruct((t, dm), F32), jax.ShapeDtypeStruct((tm, dm), F32)],
        compiler_params=_cparams(("arbitrary",)),
        name="final",
    )(r, y01_flat, y01_flat, wts, ln_g, ln_b)


def _layer(layer, x_p, x_s, pe_p, pe_s, cache_k, cache_v, state, page_table, w):
    t, dm = x_p.shape
    nb = x_s.shape[0]
    wn = N_HEADS * HEAD_W
    ta = t + ROW_PAD
    assert nb <= ROW_PAD and t % ROW_PAD == 0
    lam0 = _lambda_init(layer)
    depth_total = w["depth"]
    alpha = (2.0 * depth_total) ** 0.25
    page = cache_k.shape[1]
    start_pos = page_table.shape[1] * page

    pad = ta - t - nb
    u, d_p, qb, kb, kt_p, k_s, vb, vf_p, v_s, gates = _inproj(x_p, x_s, w["w_in"].astype(BF16), ta)

    lamv = jnp.concatenate([w["lam_q1"], w["lam_k1"], w["lam_q2"], w["lam_k2"]], axis=0)
    subln_g = w["subln_g"]

    o_p = _prompt_attn(qb, kb, vb, lamv, subln_g, t, lam0)
    kt_pool = jnp.transpose(cache_k, (0, 2, 3, 4, 1)).reshape(cache_k.shape[0], wn, page)
    v_pool = cache_v.reshape(cache_v.shape[0] * page * N_HEADS, HEAD_W)
    o_s = _decode_attn(page_table, lamv, subln_g, qb[t:t + nb], k_s, v_s, kt_pool, v_pool, lam0)

    d_s = _pool_sample(u[t:t + nb], state, start_pos)

    x1_flat, logits = _mix(d_p, d_s, o_p, o_s, gates, x_p, x_s, w["pool_w"].astype(BF16), w["pool_scale"],
                           w["w_branch_pool"].astype(BF16), w["w_branch_attn"].astype(BF16),
                           w["w_out"].astype(BF16), w["ln1_g"], w["ln1_b"], w["router_w"], w["router_b"], alpha)

    eid, wts = _route(logits)
    pos, meta, emeta = _slots(eid)
    n_row_tiles = (TOP_K * ta) // TR + N_EXPERTS
    assert n_row_tiles <= LANE
    pos0, pos1 = pos[:, 0], pos[:, 1]

    pe_b = jnp.concatenate([pe_p.astype(BF16), pe_s.astype(BF16), jnp.zeros((pad, pe_p.shape[1]), BF16)], axis=0)
    r, xs_flat, back = _ple(pos0, pos1, x1_flat, pe_b, w["ple_gate_w"].astype(BF16), w["ple_gate_b"],
                            w["ple_proj"].astype(BF16), alpha)
    y01_flat = _experts(emeta[0, :N_EXPERTS], emeta[1, :N_EXPERTS], meta[:n_row_tiles, 2], meta[:n_row_tiles, 1],
                        back, xs_flat, w["expert_gate"], w["expert_up"], w["expert_down"])
    y_p, y_s = _final(r, y01_flat, wts, w["ln2_g"], w["ln2_b"], t)
    k_prompt = jnp.transpose(kt_p.reshape(N_HEADS, 2, HEAD_DIM, t), (3, 0, 1, 2))
    v_prompt = vf_p.reshape(t, N_HEADS, HEAD_W)
    return y_p, y_s[:nb], k_prompt, v_prompt, k_s, v_s, u


def kernel(x_prompt, x_sample, p_prompt, p_sample, cache_k, cache_v, state_pool, page_table, w_in, pool_w, pool_scale, lam_q1, lam_k1, lam_q2, lam_k2, subln_g, w_branch_pool, w_branch_attn, w_out, ln1_g, ln1_b, router_group_w, router_group_b, router_expert_w, router_expert_b, expert_gate, expert_up, expert_down, ple_proj, ple_gate_w, ple_gate_b, ln2_g, ln2_b):
    depth = w_in.shape[0]
    bp, t, dm = x_prompt.shape
    nb, ts, _ = x_sample.shape
    assert bp == 1 and ts == 1 and dm // 256 == N_HEADS
    assert dm % CHUNK == 0 and dm // CHUNK == LANE
    hp = x_prompt.reshape(t, dm)
    hs = x_sample.reshape(nb, dm)
    outs = {n: [] for n in ("kp", "vp", "sp", "ks", "vs", "ss")}
    for l in range(depth):
        wr = jnp.concatenate([router_group_w[l],
                              jnp.transpose(router_expert_w[l], (1, 0, 2)).reshape(dm, N_EXPERTS)], axis=1)
        wr = jnp.pad(wr, ((0, 0), (0, LANE - wr.shape[1])))
        wr_hi = wr.astype(BF16)
        wr = jnp.stack([wr_hi, (wr - wr_hi.astype(F32)).astype(BF16)])
        br = jnp.pad(jnp.concatenate([router_group_b[l], router_expert_b[l].reshape(-1)]),
                     (0, LANE - N_GROUPS - N_EXPERTS)).reshape(1, LANE)
        row = lambda a: a[l].reshape(1, -1)
        w = dict(depth=depth, w_in=w_in[l], pool_w=pool_w[l], pool_scale=row(pool_scale),
                 lam_q1=row(lam_q1), lam_k1=row(lam_k1), lam_q2=row(lam_q2), lam_k2=row(lam_k2),
                 subln_g=row(subln_g), w_branch_pool=w_branch_pool[l], w_branch_attn=w_branch_attn[l],
                 w_out=w_out[l], ln1_g=row(ln1_g), ln1_b=row(ln1_b), router_w=wr, router_b=br,
                 expert_gate=expert_gate[l], expert_up=expert_up[l], expert_down=expert_down[l],
                 ple_proj=ple_proj[l], ple_gate_w=ple_gate_w[l], ple_gate_b=row(ple_gate_b),
                 ln2_g=row(ln2_g), ln2_b=row(ln2_b))
        state = state_pool[l]
        hp, hs, k_p, v_p, k_s, v_s, u = _layer(l, hp, hs, p_prompt[l, 0], p_sample[l, :, 0], cache_k[l],
                                               cache_v[l], state, page_table, w)
        outs["kp"].append(k_p[None])
        outs["vp"].append(v_p[None])
        outs["sp"].append(u[t - POOL_STATE:t].reshape(1, POOL_STATE, -1))
        outs["ks"].append(k_s.reshape(nb, 1, N_HEADS, 2, HEAD_DIM))
        outs["vs"].append(v_s.reshape(nb, 1, N_HEADS, HEAD_W))
        outs["ss"].append(jnp.concatenate([state[:, 1:], u[t:t + nb][:, None, :]], axis=1))
    st = lambda n: jnp.stack(outs[n])
    return (hp.reshape(1, t, dm), hs.reshape(nb, 1, dm), st("kp"), st("vp"), st("sp"),
            st("ks"), st("vs"), st("ss"))
```

```python
import functools
import math

import jax
import jax.numpy as jnp
from jax import lax
from jax.experimental import pallas as pl
from jax.experimental.pallas import tpu as pltpu

F32 = jnp.float32
BF16 = jnp.bfloat16
I32 = jnp.int32

N_HEADS = 8
HEAD_DIM = 64
HEAD_W = 2 * HEAD_DIM
ATTN_SCALE = HEAD_DIM ** -0.5
LOG2E = math.log2(math.e)
SUBLN_EPS = 1e-5
LN_EPS = 1e-5
POOL_WINDOWS = (2, 4, 8, 16)
POOL_STATE = max(POOL_WINDOWS) - 1
N_GROUPS = 4
EXPERTS_PER_GROUP = 8
N_EXPERTS = N_GROUPS * EXPERTS_PER_GROUP
TOP_K = 2

LANE = 128
SUBLANE = 8
VMEM_LIMIT = 56 * 1024 * 1024
INPROJ_VMEM_LIMIT = 60 * 1024 * 1024

ROW_PAD = 256
TM_ROW = 256
TM_SORT = 768
TQ = 512
TK = 512
HEADS_PER_STEP = 1
TR = 256
HALO = 16
CHUNK = 16
DECODE_GROUP = 4
DECODE_SLOTS = 12


def _lambda_init(layer):
    return 0.8 - 0.6 * math.exp(-0.3 * layer)


def _cparams(sem, vmem=VMEM_LIMIT):
    return pltpu.CompilerParams(dimension_semantics=sem, vmem_limit_bytes=vmem)


def _const_spec(shape):
    nd = len(shape)
    return pl.BlockSpec(shape, lambda *_: (0,) * nd, pipeline_mode=pl.Buffered(1))


def _inproj_body(xp_ref, xs_ref, w_ref, u_ref, q_ref, kb_ref, kt_ref, ks_ref, vb_ref, vf_ref, vs_ref, g_ref,
                 *, n_prompt_tiles):
    i = pl.program_id(0)
    tm, d = xp_ref.shape
    ns = xs_ref.shape[0]
    wn = u_ref.shape[1]

    def project(x, prompt):
        xb = x.astype(BF16)
        col = lambda c: jnp.dot(xb, w_ref[:, c * wn:(c + 1) * wn], preferred_element_type=F32)
        u_ref[...] = col(0)
        q_ref[...] = (col(1) * (ATTN_SCALE * LOG2E)).astype(BF16)
        kz = col(2)
        kb_ref[...] = kz.astype(BF16)
        vz = col(3)
        vb_ref[...] = vz.astype(BF16)
        if prompt:
            kt_ref[...] = kz.T
            for h in range(N_HEADS):
                vf_ref[pl.ds(h, tm, stride=N_HEADS), :] = vz[:, h * HEAD_W:(h + 1) * HEAD_W]
        else:
            ks_ref[...] = kz[:ns]
            vs_ref[...] = vz[:ns]
        for c in range(4, w_ref.shape[1] // wn):
            g_ref[:, (c - 4) * wn:(c - 3) * wn] = jax.nn.sigmoid(col(c)).astype(BF16)

    @pl.when(i < n_prompt_tiles)
    def _():
        project(xp_ref[...], True)

    @pl.when(i >= n_prompt_tiles)
    def _():
        project(jnp.concatenate([xs_ref[...], jnp.zeros((tm - ns, d), F32)], axis=0), False)


def _inproj(x_p, x_s, w_in_b, ta):
    t, d = x_p.shape
    ns = x_s.shape[0]
    n_in = w_in_b.shape[1]
    wn = N_HEADS * HEAD_W
    assert n_in == 4 * wn + 2 * d and (2 * d) % wn == 0
    tm = TM_ROW
    assert t % tm == 0 and ta == t + tm and ns <= tm and ns % SUBLANE == 0
    npt = t // tm
    row = lambda w: pl.BlockSpec((tm, w), lambda i: (i, 0))
    prow = lambda i: jnp.minimum(i, npt - 1)
    last = lambda: pl.BlockSpec((ns, wn), lambda i: (0, 0))
    return pl.pallas_call(
        functools.partial(_inproj_body, n_prompt_tiles=npt),
        grid=(ta // tm,),
        in_specs=[pl.BlockSpec((tm, d), lambda i: (prow(i), 0)), _const_spec(x_s.shape), _const_spec(w_in_b.shape)],
        out_specs=[row(wn), row(wn),
                   row(wn), pl.BlockSpec((wn, tm), lambda i: (0, prow(i))), last(),
                   row(wn), pl.BlockSpec((tm * N_HEADS, HEAD_W), lambda i: (prow(i), 0)), last(),
                   row(2 * d)],
        out_shape=[jax.ShapeDtypeStruct((ta, wn), F32),
                   jax.ShapeDtypeStruct((ta, wn), BF16),
                   jax.ShapeDtypeStruct((ta, wn), BF16),
                   jax.ShapeDtypeStruct((wn, t), F32),
                   jax.ShapeDtypeStruct((ns, wn), F32),
                   jax.ShapeDtypeStruct((ta, wn), BF16),
                   jax.ShapeDtypeStruct((t * N_HEADS, HEAD_W), F32),
                   jax.ShapeDtypeStruct((ns, wn), F32),
                   jax.ShapeDtypeStruct((ta, 2 * d), BF16)],
        compiler_params=_cparams(("arbitrary",), INPROJ_VMEM_LIMIT),
        name="inproj",
    )(x_p, x_s, w_in_b)


def _lam_value(lamv_ref, lam0):
    lv = lamv_ref[...]
    a = jnp.sum(lv[0:1] * lv[1:2], axis=1, keepdims=True)
    b = jnp.sum(lv[2:3] * lv[3:4], axis=1, keepdims=True)
    return jnp.exp(a) - jnp.exp(b) + lam0


def _subln(o, g, lam0):
    ms = jnp.mean(o * o, axis=-1, keepdims=True)
    return o * lax.rsqrt(ms + SUBLN_EPS) * g * (1.0 - lam0)


def _prompt_attn_body(lamv_ref, gcol_ref, q_ref, k_ref, v_ref, o_ref, vt_ref, q2t_ref, s_even, s_odd,
                      m_ref, l_ref, acc_ref, *, tq, tk, hp, lam0):
    i = pl.program_id(1)
    nkv = v_ref.shape[0] // tk
    heads = [pl.ds(hh * HEAD_W, HEAD_W) for hh in range(hp)]

    @pl.when(i == 0)
    def _():
        def tr(jb, _):
            for hh in range(hp):
                blk = v_ref[pl.ds(pl.multiple_of(jb * tk, tk), tk), heads[hh]].astype(F32)
                vt_ref[hh, jb] = blk.T.astype(BF16)
            return 0
        lax.fori_loop(0, nkv, tr, 0)

    drow = lax.broadcasted_iota(I32, (HEAD_W, tq), 0)
    for hh in range(hp):
        qt = q_ref[:, heads[hh]].astype(F32).T
        q2t_ref[hh] = jnp.concatenate([jnp.where(drow < HEAD_DIM, qt, 0.0),
                                       jnp.where(drow >= HEAD_DIM, qt, 0.0)], axis=1).astype(BF16)
    m_ref[...] = jnp.full(m_ref.shape, -jnp.inf, F32)
    l_ref[...] = jnp.zeros(l_ref.shape, F32)
    acc_ref[...] = jnp.zeros(acc_ref.shape, F32)
    n_full = (i * tq) // tk

    def scores(j, s_ref):
        rows = pl.ds(pl.multiple_of(j * tk, tk), tk)
        for hh in range(hp):
            s_ref[hh] = jnp.dot(k_ref[rows, heads[hh]], q2t_ref[hh], preferred_element_type=F32)

    def absorb(j, s_ref, masked):
        for hh in range(hp):
            s = s_ref[hh]
            if masked:
                kpos = n_full * tk + lax.broadcasted_iota(I32, (tk, 2 * tq), 0)
                qcol = lax.broadcasted_iota(I32, (tk, 2 * tq), 1)
                qpos = i * tq + jnp.where(qcol >= tq, qcol - tq, qcol)
                s = jnp.where(kpos <= qpos, s, -jnp.inf)
            m = m_ref[hh]
            m_new = jnp.maximum(m, jnp.max(s, axis=0, keepdims=True))
            alpha = jnp.exp2(m - m_new)
            p = jnp.exp2(s - m_new)
            l_ref[hh] = alpha * l_ref[hh] + jnp.sum(p, axis=0, keepdims=True)
            acc_ref[hh] = alpha * acc_ref[hh] + jnp.dot(vt_ref[hh, j], p.astype(BF16), preferred_element_type=F32)
            m_ref[hh] = m_new

    def by_parity(j, fn):
        @pl.when(j % 2 == 0)
        def _():
            fn(s_even, s_odd)

        @pl.when(j % 2 == 1)
        def _():
            fn(s_odd, s_even)

    scores(0, s_even)

    def body(j, _):
        def stage(cur, nxt):
            scores(j + 1, nxt)
            absorb(j, cur, False)
        by_parity(j, stage)
        return 0

    lax.fori_loop(0, n_full, body, 0)
    by_parity(n_full, lambda cur, nxt: absorb(n_full, cur, True))

    lam = _lam_value(lamv_ref, lam0)
    for hh in range(hp):
        on = acc_ref[hh] / l_ref[hh]
        ot = on[:, :tq] - lam * on[:, tq:]
        ms = jnp.mean(ot * ot, axis=0, keepdims=True)
        ot = ot * lax.rsqrt(ms + SUBLN_EPS) * gcol_ref[...] * (1.0 - lam0)
        o_ref[:, heads[hh]] = ot.T.astype(o_ref.dtype)


def _prompt_attn(qb, kb, vb, lamv, subln_g, t, lam0):
    tq, tk, hp = TQ, TK, HEADS_PER_STEP
    assert t % tk == 0 and tk % tq == 0 and N_HEADS % hp == 0
    kern = functools.partial(_prompt_attn_body, tq=tq, tk=tk, hp=hp, lam0=lam0)
    gcol = subln_g.reshape(HEAD_W, 1)
    return pl.pallas_call(
        kern,
        grid=(N_HEADS // hp, t // tq),
        in_specs=[_const_spec(lamv.shape), _const_spec(gcol.shape),
                  pl.BlockSpec((tq, hp * HEAD_W), lambda h, i: (i, h)),
                  pl.BlockSpec((t, hp * HEAD_W), lambda h, i: (0, h)),
                  pl.BlockSpec((t, hp * HEAD_W), lambda h, i: (0, h))],
        out_specs=pl.BlockSpec((tq, hp * HEAD_W), lambda h, i: (i, h)),
        out_shape=jax.ShapeDtypeStruct((t, N_HEADS * HEAD_W), BF16),
        scratch_shapes=[pltpu.VMEM((hp, t // tk, HEAD_W, tk), BF16),
                        pltpu.VMEM((hp, HEAD_W, 2 * tq), BF16),
                        pltpu.VMEM((hp, tk, 2 * tq), F32), pltpu.VMEM((hp, tk, 2 * tq), F32),
                        pltpu.VMEM((hp, 1, 2 * tq), F32), pltpu.VMEM((hp, 1, 2 * tq), F32),
                        pltpu.VMEM((hp, HEAD_W, 2 * tq), F32)],
        compiler_params=_cparams(("arbitrary", "arbitrary")),
        name="prompt_attn",
    )(lamv, gcol, qb, kb, vb)


def _decode_attn_body(pt_ref, lamv_ref, g_ref, q_ref, qc_ref, kn_ref, vn_ref, kt_hbm, v_hbm, o_ref,
                      kbuf, vbuf, sem, *, n_pages, page, n_slots, group, lam0):
    b = pl.program_id(0)
    nb = pl.num_programs(0)
    nr = 2 * N_HEADS
    wn = N_HEADS * HEAD_W
    vrows = page * N_HEADS
    ahead = n_slots - group
    total = nb * n_pages
    assert n_pages % group == 0 and n_slots % group == 0 and ahead >= group

    def copies(g):
        slot = g % n_slots
        pg = pt_ref[g // n_pages, g % n_pages]
        ck = pltpu.make_async_copy(kt_hbm.at[pg], kbuf.at[slot], sem.at[0, slot])
        cv = pltpu.make_async_copy(v_hbm.at[pl.ds(pl.multiple_of(pg * vrows, vrows), vrows)],
                                   vbuf.at[slot], sem.at[1, slot])
        return ck, cv

    def start(g):
        for c in copies(g):
            c.start()

    @pl.when(b == 0)
    def _():
        for g in range(ahead):
            start(g)

    rr = lax.broadcasted_iota(I32, (nr, wn), 0)
    cc = lax.broadcasted_iota(I32, (nr, wn), 1)
    qrow = q_ref[0]
    qbd = jnp.where(cc // HEAD_DIM == rr, jnp.broadcast_to(qrow.astype(F32), (nr, wn)), 0.0).astype(BF16)
    rhead = lax.broadcasted_iota(I32, (nr, HEAD_W), 0) // 2
    qcol = jnp.broadcast_to(qc_ref[0], (wn, page))
    tok = lax.broadcasted_iota(I32, (page, vrows), 0)
    vrow = lax.broadcasted_iota(I32, (page, vrows), 1)
    spread = jnp.where(vrow // N_HEADS == tok, 1.0, 0.0).astype(BF16)
    own_head = lax.broadcasted_iota(I32, (nr, vrows), 1) % N_HEADS == lax.broadcasted_iota(I32, (nr, vrows), 0) // 2

    def body(jg, carry):
        m, l, acc = carry
        g0 = b * n_pages + jg * group
        slots = []
        ss = []
        for a in range(group):
            g = g0 + a

            @pl.when(g + ahead < total)
            def _():
                start(g + ahead)

            for c in copies(g):
                c.wait()
            slot = g % n_slots
            slots.append(slot)
            prod = kbuf[slot] * qcol
            ss.append(jnp.sum(prod.reshape(nr, HEAD_DIM, page), axis=1))
        m_new = m
        for s in ss:
            m_new = jnp.maximum(m_new, jnp.max(s, axis=1, keepdims=True))
        alpha = jnp.exp2(m - m_new)
        l = alpha * l
        acc = alpha * acc
        for s, slot in zip(ss, slots):
            p = jnp.exp2(s - m_new)
            l = l + jnp.sum(p, axis=1, keepdims=True)
            pe = jnp.dot(p.astype(BF16), spread, preferred_element_type=F32)
            pe = jnp.where(own_head, pe, 0.0).astype(BF16)
            acc = acc + jnp.dot(pe, vbuf[slot].astype(BF16), preferred_element_type=F32)
        return m_new, l, acc

    init = (jnp.full((nr, 1), -jnp.inf, F32), jnp.zeros((nr, 1), F32), jnp.zeros((nr, HEAD_W), F32))
    m, l, acc = lax.fori_loop(0, n_pages // group, body, init)

    kn = jnp.broadcast_to(kn_ref[0], (nr, wn))
    s_new = jnp.sum(qbd.astype(F32) * kn, axis=1, keepdims=True)
    vn = vn_ref[0]
    vnew = jnp.zeros((nr, HEAD_W), F32)
    for h in range(N_HEADS):
        vnew = vnew + jnp.where(rhead == h, jnp.broadcast_to(vn[:, h * HEAD_W:(h + 1) * HEAD_W], (nr, HEAD_W)), 0.0)
    m_new = jnp.maximum(m, s_new)
    alpha = jnp.exp2(m - m_new)
    p_new = jnp.exp2(s_new - m_new)
    l = alpha * l + p_new
    acc = alpha * acc + p_new * vnew
    on = acc / l
    lam = _lam_value(lamv_ref, lam0)
    o = on - lam * pltpu.roll(on, nr - 1, 0)
    o_ref[0] = _subln(o, g_ref[...], lam0)


def _decode_attn(page_table, lamv, subln_g, q_s, k_s, v_s, kt_pool, v_pool, lam0):
    nb, n_pages = page_table.shape
    wn = N_HEADS * HEAD_W
    page = kt_pool.shape[2]
    nr = 2 * N_HEADS
    n_slots = DECODE_SLOTS
    kern = functools.partial(_decode_attn_body, n_pages=n_pages, page=page, n_slots=n_slots,
                             group=DECODE_GROUP, lam0=lam0)
    grid_spec = pltpu.PrefetchScalarGridSpec(
        num_scalar_prefetch=1,
        grid=(nb,),
        in_specs=[pl.BlockSpec(lamv.shape, lambda b, pt: (0, 0)),
                  pl.BlockSpec(subln_g.shape, lambda b, pt: (0, 0)),
                  pl.BlockSpec((1, 1, wn), lambda b, pt: (b, 0, 0)),
                  pl.BlockSpec((1, wn, 1), lambda b, pt: (b, 0, 0)),
                  pl.BlockSpec((1, 1, wn), lambda b, pt: (b, 0, 0)),
                  pl.BlockSpec((1, 1, wn), lambda b, pt: (b, 0, 0)),
                  pl.BlockSpec(memory_space=pl.ANY),
                  pl.BlockSpec(memory_space=pl.ANY)],
        out_specs=pl.BlockSpec((1, nr, HEAD_W), lambda b, pt: (b, 0, 0)),
        scratch_shapes=[pltpu.VMEM((n_slots, wn, page), F32),
                        pltpu.VMEM((n_slots, page * N_HEADS, HEAD_W), F32),
                        pltpu.SemaphoreType.DMA((2, n_slots))],
    )
    out = pl.pallas_call(
        kern,
        grid_spec=grid_spec,
        out_shape=jax.ShapeDtypeStruct((nb, nr, HEAD_W), F32),
        compiler_params=_cparams(("arbitrary",)),
        name="decode_attn",
    )(page_table, lamv, subln_g, q_s.reshape(nb, 1, wn), q_s.astype(F32).reshape(nb, wn, 1),
      k_s.reshape(nb, 1, wn), v_s.reshape(nb, 1, wn), kt_pool, v_pool)
    return out[:, 0::2, :].reshape(nb, wn)


def _pool_prompt_body(u_ref, halo_ref, d_ref, ext_ref, *, tm, gw):
    i = pl.program_id(0)
    ext_ref[pl.ds(HALO, tm), :] = u_ref[...]

    @pl.when(i == 0)
    def _():
        ext_ref[pl.ds(0, HALO), :] = jnp.zeros((HALO, u_ref.shape[1]), F32)

    @pl.when(i > 0)
    def _():
        ext_ref[pl.ds(0, HALO), :] = halo_ref[...]

    pos = i * tm + lax.broadcasted_iota(I32, (tm, 1), 0)
    for g, w in enumerate(POOL_WINDOWS):
        cols = pl.ds(g * gw, gw)
        tok = ext_ref[pl.ds(HALO, tm), cols]
        s = tok
        for back in range(1, w):
            s = s + ext_ref[pl.ds(HALO - back, tm), cols]
        inv = 1.0 / jnp.minimum(pos + 1, w).astype(F32)
        d_ref[:, cols] = (s * inv - tok).astype(d_ref.dtype)


def _pool_prompt(u_all, t):
    wp = u_all.shape[1]
    gw = wp // len(POOL_WINDOWS)
    tm = TM_ROW
    assert t % tm == 0 and tm % HALO == 0 and gw % LANE == 0
    kern = functools.partial(_pool_prompt_body, tm=tm, gw=gw)
    return pl.pallas_call(
        kern,
        grid=(t // tm,),
        in_specs=[pl.BlockSpec((tm, wp), lambda i: (i, 0)),
                  pl.BlockSpec((HALO, wp), lambda i: (jnp.maximum(i * (tm // HALO) - 1, 0), 0))],
        out_specs=pl.BlockSpec((tm, wp), lambda i: (i, 0)),
        out_shape=jax.ShapeDtypeStruct((t, wp), BF16),
        scratch_shapes=[pltpu.VMEM((tm + HALO, wp), F32)],
        compiler_params=_cparams(("arbitrary",)),
        name="pool_prompt",
    )(u_all, u_all)


def _pool_sample_body(u_ref, st_ref, d_ref, *, gw, start_pos):
    st = st_ref[...]
    u = u_ref[...]
    ridx = lax.broadcasted_iota(I32, st.shape, 1)
    for g, w in enumerate(POOL_WINDOWS):
        hist = jnp.sum(jnp.where(ridx >= POOL_STATE - (w - 1), st, 0.0), axis=1)
        cnt = float(min(start_pos + 1, w))
        d = (hist + u) * (1.0 / cnt) - u
        d_ref[:, g * gw:(g + 1) * gw] = d[:, g * gw:(g + 1) * gw].astype(d_ref.dtype)


def _pool_sample(u_s, state, start_pos):
    nb, wp = u_s.shape
    gw = wp // len(POOL_WINDOWS)
    kern = functools.partial(_pool_sample_body, gw=gw, start_pos=start_pos)
    return pl.pallas_call(
        kern,
        grid=(1,),
        in_specs=[_const_spec(u_s.shape), _const_spec(state.shape)],
        out_specs=_const_spec((nb, wp)),
        out_shape=jax.ShapeDtypeStruct((nb, wp), BF16),
        compiler_params=_cparams(("arbitrary",)),
        name="pool_sample",
    )(u_s, state)


def _store_flat(flat_ref, x, lead=()):
    rows = x.shape[0]
    for c in range(CHUNK):
        flat_ref[(*lead, pl.ds(c, rows, stride=CHUNK), slice(None))] = x[:, c * LANE:(c + 1) * LANE]


def _load_flat(flat_ref, rows, lead=(), first=0):
    return jnp.concatenate(
        [flat_ref[(*lead, pl.ds(first * CHUNK + c, rows, stride=CHUNK), slice(None))] for c in range(CHUNK)],
        axis=1)


def _layer_norm(x, g, b):
    mu = jnp.mean(x, axis=-1, keepdims=True)
    xc = x - mu
    var = jnp.mean(xc * xc, axis=-1, keepdims=True)
    return xc * lax.rsqrt(var + LN_EPS) * g + b


def _mix_body(dp_ref, ds_ref, op_ref, os_ref, g_ref, xp_ref, xs_ref, pw_ref, ps_ref, wbp_ref, wba_ref, wo_ref,
              lg_ref, lb_ref, wr_ref, br_ref, x1f_ref, lo_ref, d_ref, o_ref, *, n_prompt_tiles, alpha):
    i = pl.program_id(0)
    tm, dm = xp_ref.shape
    ns = xs_ref.shape[0]
    ng = len(POOL_WINDOWS)
    gw = d_ref.shape[1] // ng

    @pl.when(i < n_prompt_tiles)
    def _():
        d_ref[...] = dp_ref[...]
        o_ref[...] = op_ref[...]

    @pl.when(i >= n_prompt_tiles)
    def _():
        d_ref[...] = jnp.zeros(d_ref.shape, d_ref.dtype)
        o_ref[...] = jnp.zeros(o_ref.shape, o_ref.dtype)
        d_ref[0:ns, :] = ds_ref[...]
        o_ref[0:ns, :] = os_ref[...].astype(o_ref.dtype)

    ys = []
    for g in range(ng):
        ys.append(jnp.dot(d_ref[:, g * gw:(g + 1) * gw], pw_ref[g], preferred_element_type=F32))
    pool_y = (jnp.concatenate(ys, axis=1) * ps_ref[...]).astype(BF16)
    a = jnp.dot(pool_y, wbp_ref[...], preferred_element_type=F32)
    bb = jnp.dot(o_ref[...], wba_ref[...], preferred_element_type=F32)
    gates = g_ref[...].astype(F32)
    merged = (gates[:, :dm] * a + gates[:, dm:] * bb).astype(BF16)
    mix = jnp.dot(merged, wo_ref[...], preferred_element_type=F32)

    def finish(x):
        x1 = _layer_norm(alpha * x + mix, lg_ref[...], lb_ref[...])
        _store_flat(x1f_ref, x1)
        hi = x1.astype(BF16)
        lo = (x1 - hi.astype(F32)).astype(BF16)
        dot = functools.partial(jnp.dot, preferred_element_type=F32)
        lo_ref[...] = dot(hi, wr_ref[0]) + (dot(lo, wr_ref[0]) + dot(hi, wr_ref[1])) + br_ref[...]

    @pl.when(i < n_prompt_tiles)
    def _():
        finish(xp_ref[...])

    @pl.when(i >= n_prompt_tiles)
    def _():
        ns = xs_ref.shape[0]
        finish(jnp.concatenate([xs_ref[...], jnp.zeros((tm - ns, dm), F32)], axis=0))


def _mix(d_p, d_s, o_p, o_s, gates, x_p, x_s, pool_w_b, pool_scale, wbp, wba, wo, ln_g, ln_b, wr, br, alpha):
    ta = gates.shape[0]
    t, dm = x_p.shape
    ns = x_s.shape[0]
    tm = TM_ROW
    assert ta % tm == 0 and t % tm == 0 and ns % (2 * SUBLANE) == 0 and ns <= tm and ta == t + tm
    npt = t // tm
    wp = d_p.shape[1]
    wa = o_p.shape[1]
    row = lambda w: pl.BlockSpec((tm, w), lambda i: (i, 0))
    prow = lambda w: pl.BlockSpec((tm, w), lambda i: (jnp.minimum(i, npt - 1), 0))
    kern = functools.partial(_mix_body, n_prompt_tiles=npt, alpha=alpha)
    return pl.pallas_call(
        kern,
        grid=(ta // tm,),
        in_specs=[prow(wp), _const_spec(d_s.shape), prow(wa), _const_spec(o_s.shape), row(2 * dm),
                  prow(dm),
                  _const_spec(x_s.shape),
                  _const_spec(pool_w_b.shape), _const_spec(pool_scale.shape),
                  _const_spec(wbp.shape), _const_spec(wba.shape), _const_spec(wo.shape),
                  _const_spec(ln_g.shape), _const_spec(ln_b.shape),
                  _const_spec(wr.shape), _const_spec(br.shape)],
        out_specs=[pl.BlockSpec((tm * CHUNK, LANE), lambda i: (i, 0)), row(LANE)],
        out_shape=[jax.ShapeDtypeStruct((ta * CHUNK, LANE), F32),
                   jax.ShapeDtypeStruct((ta, LANE), F32)],
        scratch_shapes=[pltpu.VMEM((tm, wp), BF16), pltpu.VMEM((tm, wa), BF16)],
        compiler_params=_cparams(("arbitrary",)),
        name="branch_mix",
    )(d_p, d_s, o_p, o_s, gates, x_p, x_s, pool_w_b, pool_scale, wbp, wba, wo, ln_g, ln_b, wr, br)


def _route_body(lo_ref, eid_ref, wt_ref):
    lo = lo_ref[...]
    lane_i = lax.broadcasted_iota(I32, lo.shape, 1)
    lane = lane_i.astype(F32)
    ninf = -jnp.inf
    big = float(LANE)
    is_g = lane_i < N_GROUPS
    gl = jnp.where(is_g, lo, ninf)
    gmax = jnp.max(gl, axis=1, keepdims=True)
    gidx = jnp.min(jnp.where(gl == gmax, lane, big), axis=1, keepdims=True)
    gsum = jnp.sum(jnp.where(is_g, jnp.exp(gl - gmax), 0.0), axis=1, keepdims=True)
    gw = 1.0 / gsum
    lo_e = N_GROUPS + EXPERTS_PER_GROUP * gidx
    el = jnp.where(jnp.logical_and(lane >= lo_e, lane < lo_e + EXPERTS_PER_GROUP), lo, ninf)
    v1 = jnp.max(el, axis=1, keepdims=True)
    i1 = jnp.min(jnp.where(el == v1, lane, big), axis=1, keepdims=True)
    el2 = jnp.where(lane == i1, ninf, el)
    v2 = jnp.max(el2, axis=1, keepdims=True)
    i2 = jnp.min(jnp.where(el2 == v2, lane, big), axis=1, keepdims=True)
    e21 = jnp.exp(v2 - v1)
    w1 = gw / (1.0 + e21)
    w2 = gw * e21 / (1.0 + e21)
    eid = jnp.where(lane_i == 0, i1 - N_GROUPS, jnp.where(lane_i == 1, i2 - N_GROUPS, 0.0))
    eid_ref[...] = eid.astype(I32)
    wt_ref[...] = jnp.where(lane_i == 0, w1, jnp.where(lane_i == 1, w2, 0.0))


def _route(logits):
    ta = logits.shape[0]
    tm = TM_SORT
    assert ta % tm == 0
    row = pl.BlockSpec((tm, LANE), lambda i: (i, 0))
    return pl.pallas_call(
        _route_body,
        grid=(ta // tm,),
        in_specs=[row],
        out_specs=[row, row],
        out_shape=[jax.ShapeDtypeStruct((ta, LANE), I32), jax.ShapeDtypeStruct((ta, LANE), F32)],
        compiler_params=_cparams(("arbitrary",)),
        name="route",
    )(logits)


def _slots_body(eid_ref, pos_ref, meta_ref, emeta_ref, cnt_ref, run_ref, off_ref, *, tm, tr):
    ph = pl.program_id(0)
    i = pl.program_id(1)
    eid = eid_ref[...]
    lane = lax.broadcasted_iota(I32, (tm, LANE), 1)
    e0 = eid[:, 0:1]
    e1 = eid[:, 1:2]
    oh0 = (lane == e0).astype(F32)
    oh1 = (lane == e1).astype(F32)
    both = oh0 + oh1

    @pl.when(jnp.logical_and(ph == 0, i == 0))
    def _():
        cnt_ref[...] = jnp.zeros_like(cnt_ref)

    @pl.when(ph == 0)
    def _():
        cnt_ref[...] = cnt_ref[...] + jnp.sum(both, axis=0, keepdims=True)

    @pl.when(jnp.logical_and(ph == 1, i == 0))
    def _():
        cnt = cnt_ref[...]
        tiles = jnp.floor((cnt + (tr - 1)) * (1.0 / tr))
        a = lax.broadcasted_iota(I32, (LANE, LANE), 0)
        b = lax.broadcasted_iota(I32, (LANE, LANE), 1)
        upper = (a < b).astype(BF16)
        prefix = lambda v: jnp.dot(jnp.broadcast_to(v, (SUBLANE, LANE)).astype(BF16), upper,
                                   preferred_element_type=F32)[0:1]
        off_tiles = prefix(tiles)
        cnt_hi = jnp.floor(cnt * (1.0 / 256.0))
        off_rows = 256.0 * prefix(cnt_hi) + prefix(cnt - 256.0 * cnt_hi)
        off_ref[...] = off_rows
        run_ref[...] = jnp.zeros_like(run_ref)
        end_tiles = off_tiles + tiles
        lane1 = lax.broadcasted_iota(I32, (LANE, LANE), 1)
        jrow = lax.broadcasted_iota(I32, (LANE, LANE), 0).astype(F32)
        done = jnp.logical_and(lane1 < N_EXPERTS, jnp.broadcast_to(end_tiles, (LANE, LANE)) <= jrow)
        te = jnp.sum(done.astype(F32), axis=1, keepdims=True)
        owns = jnp.logical_and(jnp.broadcast_to(off_tiles, (LANE, LANE)) <= jrow,
                               jrow < jnp.broadcast_to(end_tiles, (LANE, LANE)))
        left = jnp.broadcast_to(cnt, (LANE, LANE)) - (jrow - jnp.broadcast_to(off_tiles, (LANE, LANE))) * tr
        nv = jnp.sum(jnp.where(owns, jnp.minimum(left, float(tr)), 0.0), axis=1, keepdims=True)
        first = jnp.broadcast_to(off_rows, (LANE, LANE)) + (jrow - jnp.broadcast_to(off_tiles, (LANE, LANE))) * tr
        start = jnp.sum(jnp.where(owns, first, 0.0), axis=1, keepdims=True)
        meta_ref[...] = jnp.where(lane1 == 0, jnp.broadcast_to(te, (LANE, LANE)),
                                  jnp.where(lane1 == 1, jnp.broadcast_to(nv, (LANE, LANE)),
                                            jnp.broadcast_to(start, (LANE, LANE)))).astype(I32)
        erow = lax.broadcasted_iota(I32, (SUBLANE, LANE), 0)
        emeta_ref[...] = jnp.where(erow == 0, jnp.broadcast_to(off_tiles, (SUBLANE, LANE)),
                                   jnp.broadcast_to(tiles, (SUBLANE, LANE))).astype(I32)

    @pl.when(ph == 1)
    def _():
        r = lax.broadcasted_iota(I32, (tm, tm), 0)
        c = lax.broadcasted_iota(I32, (tm, tm), 1)
        lower = (c < r).astype(BF16)
        before = jnp.dot(lower, both.astype(BF16), preferred_element_type=F32) + run_ref[...] + off_ref[...]
        p0 = jnp.sum(oh0 * before, axis=1, keepdims=True)
        p1 = jnp.sum(oh1 * before, axis=1, keepdims=True)
        pos_ref[...] = jnp.where(lane == 0, p0, jnp.where(lane == 1, p1, 0.0)).astype(I32)
        run_ref[...] = run_ref[...] + jnp.sum(both, axis=0, keepdims=True)


def _slots(eid):
    ta = eid.shape[0]
    tm = TM_SORT
    assert ta % tm == 0
    kern = functools.partial(_slots_body, tm=tm, tr=TR)
    return pl.pallas_call(
        kern,
        grid=(2, ta // tm),
        in_specs=[pl.BlockSpec((tm, LANE), lambda ph, i: (i, 0))],
        out_specs=[pl.BlockSpec((tm, LANE), lambda ph, i: (ph * i, 0)),
                   pl.BlockSpec((LANE, LANE), lambda ph, i: (0, 0)),
                   pl.BlockSpec((SUBLANE, LANE), lambda ph, i: (0, 0))],
        out_shape=[jax.ShapeDtypeStruct((ta, LANE), I32),
                   jax.ShapeDtypeStruct((LANE, LANE), I32),
                   jax.ShapeDtypeStruct((SUBLANE, LANE), I32)],
        scratch_shapes=[pltpu.VMEM((1, LANE), F32), pltpu.VMEM((1, LANE), F32), pltpu.VMEM((1, LANE), F32)],
        compiler_params=_cparams(("arbitrary", "arbitrary")),
        name="slots",
    )(eid)


def _ple_body(p0_ref, p1_ref, x1f_ref, pe_ref, gw_ref, gb_ref, pp_ref, r_ref, xs_hbm, back_ref, zbuf, sem, zsem,
              *, alpha, n_pairs):
    i = pl.program_id(0)
    tm = r_ref.shape[0]
    ta = n_pairs // TOP_K

    for r in range(tm):
        tok = i * tm + r
        for choice, pos_ref in enumerate((p0_ref, p1_ref)):
            pos = pos_ref[tok]
            back_ref[pos] = choice * ta + tok
            dst = pl.ds(pl.multiple_of(pos * CHUNK, CHUNK), CHUNK)
            pltpu.make_async_copy(x1f_ref.at[pl.ds(r * CHUNK, CHUNK)], xs_hbm.at[dst], sem).start()
    x1 = _load_flat(x1f_ref, tm)
    gate = jax.nn.sigmoid(jnp.dot(x1.astype(BF16), gw_ref[...], preferred_element_type=F32) + gb_ref[...])
    emb = jnp.dot(pe_ref[...], pp_ref[...], preferred_element_type=F32)
    r_ref[...] = alpha * x1 + gate * emb

    @pl.when(i == 0)
    def _():
        zbuf[...] = jnp.zeros(zbuf.shape, F32)
        tail = pltpu.make_async_copy(zbuf, xs_hbm.at[pl.ds(n_pairs * CHUNK, zbuf.shape[0])], zsem)
        tail.start()
        tail.wait()

    pltpu.make_async_copy(x1f_ref, xs_hbm.at[pl.ds(0, tm * CHUNK)], sem).wait()
    pltpu.make_async_copy(x1f_ref, xs_hbm.at[pl.ds(0, tm * CHUNK)], sem).wait()


def _ple(pos0, pos1, x1_flat, pe_b, gw_b, gb, pp_b, alpha):
    ta = pos0.shape[0]
    dm = gw_b.shape[0]
    tm = TM_ROW
    tr = TR
    n_pairs = TOP_K * ta
    row = lambda w: pl.BlockSpec((tm, w), lambda i, *_: (i, 0))
    const = lambda a: pl.BlockSpec(a.shape, lambda i, *_: (0,) * a.ndim, pipeline_mode=pl.Buffered(1))
    grid_spec = pltpu.PrefetchScalarGridSpec(
        num_scalar_prefetch=2,
        grid=(ta // tm,),
        in_specs=[pl.BlockSpec((tm * CHUNK, LANE), lambda i, *_: (i, 0)), row(pe_b.shape[1]),
                  const(gw_b), const(gb), const(pp_b)],
        out_specs=[row(dm), pl.BlockSpec(memory_space=pl.ANY), pl.BlockSpec(memory_space=pltpu.SMEM)],
        scratch_shapes=[pltpu.VMEM((tr * CHUNK, LANE), F32), pltpu.SemaphoreType.DMA(()),
                        pltpu.SemaphoreType.DMA(())],
    )
    return pl.pallas_call(
        functools.partial(_ple_body, alpha=alpha, n_pairs=n_pairs),
        grid_spec=grid_spec,
        out_shape=[jax.ShapeDtypeStruct((ta, dm), F32),
                   jax.ShapeDtypeStruct(((n_pairs + tr) * CHUNK, LANE), F32),
                   jax.ShapeDtypeStruct((n_pairs,), I32)],
        compiler_params=_cparams(("arbitrary",)),
        name="ple_dispatch",
    )(pos0, pos1, x1_flat, pe_b, gw_b, gb, pp_b)


def _expert_body(first_ref, ntile_ref, start_ref, nv_ref, dst_ref, x_hbm, wg_ref, wu_ref, wd_ref, y_hbm,
                 xbuf, ybuf, wgb, wub, wdb, gsem, ssem, *, tr):
    e = pl.program_id(0)
    ne = pl.num_programs(0)
    nt = first_ref[ne - 1] + ntile_ref[ne - 1]
    UNROLL = 8

    def get_tile(jj, sl):
        rows = pl.ds(pl.multiple_of(start_ref[jj] * CHUNK, CHUNK), tr * CHUNK)
        return pltpu.make_async_copy(x_hbm.at[rows], xbuf.at[sl], gsem.at[sl])

    def put_row(jj, sl, r):
        stage = pl.ds(pl.multiple_of(r * CHUNK, CHUNK), CHUNK)
        dst = pl.ds(pl.multiple_of(dst_ref[start_ref[jj] + r] * CHUNK, CHUNK), CHUNK)
        return pltpu.make_async_copy(ybuf.at[sl, stage], y_hbm.at[dst], ssem.at[sl])

    def put_rows(jj, sl):
        def block(rb, _):
            for q in range(UNROLL):
                put_row(jj, sl, rb * UNROLL + q).start()
            return 0

        nv = nv_ref[jj]
        full = nv // UNROLL
        lax.fori_loop(0, full, block, 0)
        for q in range(UNROLL - 1):
            @pl.when(full * UNROLL + q < nv)
            def _():
                put_row(jj, sl, full * UNROLL + q).start()

    def put_wait(jj, sl):
        rows = pl.ds(0, nv_ref[jj] * CHUNK)
        pltpu.make_async_copy(ybuf.at[sl, rows], y_hbm.at[rows], ssem.at[sl]).wait()

    @pl.when(e == 0)
    def _():
        get_tile(0, 0).start()

    @pl.when(ntile_ref[e] > 0)
    def _():
        wgb[...] = wg_ref[0].astype(BF16)
        wub[...] = wu_ref[0].astype(BF16)
        wdb[...] = wd_ref[0].astype(BF16)

        def tile(jj, _):
            j = first_ref[e] + jj
            slot = j % 2
            get_tile(j, slot).wait()

            @pl.when(j + 1 < nt)
            def _():
                get_tile(j + 1, 1 - slot).start()

            x = _load_flat(xbuf, tr, (slot,)).astype(BF16)
            hg = jnp.dot(x, wgb[...], preferred_element_type=F32)
            hu = jnp.dot(x, wub[...], preferred_element_type=F32)
            h = (hg * jax.nn.sigmoid(hg) * hu).astype(BF16)
            y = jnp.dot(h, wdb[...], preferred_element_type=F32)

            @pl.when(j >= 2)
            def _():
                put_wait(j - 2, slot)

            _store_flat(ybuf, y, (slot,))
            put_rows(j, slot)
            return 0

        lax.fori_loop(0, ntile_ref[e], tile, 0)

    @pl.when(e == ne - 1)
    def _():
        last = nt - 1
        put_wait(last, last % 2)

        @pl.when(nt >= 2)
        def _():
            put_wait(last - 1, 1 - last % 2)


def _experts(first_tile, n_tile, tile_start, tile_valid, back, xs_flat, w_gate, w_up, w_down):
    n_pairs = back.shape[0]
    ne, dm, ff = w_gate.shape
    tr = TR
    assert dm == CHUNK * LANE and first_tile.shape == (ne,)
    sp = lambda e, *_: (e, 0, 0)
    any_spec = pl.BlockSpec(memory_space=pl.ANY)
    grid_spec = pltpu.PrefetchScalarGridSpec(
        num_scalar_prefetch=5,
        grid=(ne,),
        in_specs=[any_spec,
                  pl.BlockSpec((1, dm, ff), sp), pl.BlockSpec((1, dm, ff), sp), pl.BlockSpec((1, ff, dm), sp)],
        out_specs=any_spec,
        scratch_shapes=[pltpu.VMEM((2, tr * CHUNK, LANE), F32), pltpu.VMEM((2, tr * CHUNK, LANE), F32),
                        pltpu.VMEM((dm, ff), BF16), pltpu.VMEM((dm, ff), BF16), pltpu.VMEM((ff, dm), BF16),
                        pltpu.SemaphoreType.DMA((2,)), pltpu.SemaphoreType.DMA((2,))],
    )
    return pl.pallas_call(
        functools.partial(_expert_body, tr=tr),
        grid_spec=grid_spec,
        out_shape=jax.ShapeDtypeStruct((n_pairs * CHUNK, LANE), F32),
        compiler_params=_cparams(("arbitrary",)),
        name="experts",
    )(first_tile, n_tile, tile_start, tile_valid, back, xs_flat, w_gate, w_up, w_down)


def _final_body(r_ref, y0_ref, y1_ref, wt_ref, g_ref, b_ref, yp_ref, ys_ref, *, n_prompt_tiles):
    i = pl.program_id(0)
    wt = wt_ref[...]
    tm = r_ref.shape[0]
    x = r_ref[...] + wt[:, 0:1] * _load_flat(y0_ref, tm) + wt[:, 1:2] * _load_flat(y1_ref, tm)
    y = _layer_norm(x, g_ref[...], b_ref[...])

    @pl.when(i < n_prompt_tiles)
    def _():
        yp_ref[...] = y

    @pl.when(i >= n_prompt_tiles)
    def _():
        ys_ref[...] = y


def _final(r, y01_flat, wts, ln_g, ln_b, t):
    ta, dm = r.shape
    tm = TM_ROW
    npt = t // tm
    nta = ta // tm
    row = lambda w: pl.BlockSpec((tm, w), lambda i: (i, 0))
    return pl.pallas_call(
        functools.partial(_final_body, n_prompt_tiles=npt),
        grid=(nta,),
        in_specs=[row(dm),
                  pl.BlockSpec((tm * CHUNK, LANE), lambda i: (i, 0)),
                  pl.BlockSpec((tm * CHUNK, LANE), lambda i: (nta + i, 0)),
                  row(LANE), _const_spec(ln_g.shape), _const_spec(ln_b.shape)],
        out_specs=[pl.BlockSpec((tm, dm), lambda i: (jnp.minimum(i, npt - 1), 0)),
                   pl.BlockSpec((tm, dm), lambda i: (0, 0))],
        out_shape=[jax.ShapeDtypeStruct((t, dm), F32), jax.ShapeDtypeStruct((tm, dm), F32)],
        compiler_params=_cparams(("arbitrary",)),
        name="final",
    )(r, y01_flat, y01_flat, wts, ln_g, ln_b)


def _layer(layer, x_p, x_s, pe_p, pe_s, cache_k, cache_v, state, page_table, w):
    t, dm = x_p.shape
    nb = x_s.shape[0]
    wn = N_HEADS * HEAD_W
    ta = t + ROW_PAD
    assert nb <= ROW_PAD and t % ROW_PAD == 0
    lam0 = _lambda_init(layer)
    depth_total = w["depth"]
    alpha = (2.0 * depth_total) ** 0.25
    page = cache_k.shape[1]
    start_pos = page_table.shape[1] * page

    pad = ta - t - nb
    u, qb, kb, kt_p, k_s, vb, vf_p, v_s, gates = _inproj(x_p, x_s, w["w_in"].astype(BF16), ta)

    lamv = jnp.concatenate([w["lam_q1"], w["lam_k1"], w["lam_q2"], w["lam_k2"]], axis=0)
    subln_g = w["subln_g"]

    o_p = _prompt_attn(qb, kb, vb, lamv, subln_g, t, lam0)
    kt_pool = jnp.transpose(cache_k, (0, 2, 3, 4, 1)).reshape(cache_k.shape[0], wn, page)
    v_pool = cache_v.reshape(cache_v.shape[0] * page * N_HEADS, HEAD_W)
    o_s = _decode_attn(page_table, lamv, subln_g, qb[t:t + nb], k_s, v_s, kt_pool, v_pool, lam0)

    d_p = _pool_prompt(u, t)
    d_s = _pool_sample(u[t:t + nb], state, start_pos)

    x1_flat, logits = _mix(d_p, d_s, o_p, o_s, gates, x_p, x_s, w["pool_w"].astype(BF16), w["pool_scale"],
                           w["w_branch_pool"].astype(BF16), w["w_branch_attn"].astype(BF16),
                           w["w_out"].astype(BF16), w["ln1_g"], w["ln1_b"], w["router_w"], w["router_b"], alpha)

    eid, wts = _route(logits)
    pos, meta, emeta = _slots(eid)
    n_row_tiles = (TOP_K * ta) // TR + N_EXPERTS
    assert n_row_tiles <= LANE
    pos0, pos1 = pos[:, 0], pos[:, 1]

    pe_b = jnp.concatenate([pe_p.astype(BF16), pe_s.astype(BF16), jnp.zeros((pad, pe_p.shape[1]), BF16)], axis=0)
    r, xs_flat, back = _ple(pos0, pos1, x1_flat, pe_b, w["ple_gate_w"].astype(BF16), w["ple_gate_b"],
                            w["ple_proj"].astype(BF16), alpha)
    y01_flat = _experts(emeta[0, :N_EXPERTS], emeta[1, :N_EXPERTS], meta[:n_row_tiles, 2], meta[:n_row_tiles, 1],
                        back, xs_flat, w["expert_gate"], w["expert_up"], w["expert_down"])
    y_p, y_s = _final(r, y01_flat, wts, w["ln2_g"], w["ln2_b"], t)
    k_prompt = jnp.transpose(kt_p.reshape(N_HEADS, 2, HEAD_DIM, t), (3, 0, 1, 2))
    v_prompt = vf_p.reshape(t, N_HEADS, HEAD_W)
    return y_p, y_s[:nb], k_prompt, v_prompt, k_s, v_s, u


def kernel(x_prompt, x_sample, p_prompt, p_sample, cache_k, cache_v, state_pool, page_table, w_in, pool_w, pool_scale, lam_q1, lam_k1, lam_q2, lam_k2, subln_g, w_branch_pool, w_branch_attn, w_out, ln1_g, ln1_b, router_group_w, router_group_b, router_expert_w, router_expert_b, expert_gate, expert_up, expert_down, ple_proj, ple_gate_w, ple_gate_b, ln2_g, ln2_b):
    depth = w_in.shape[0]
    bp, t, dm = x_prompt.shape
    nb, ts, _ = x_sample.shape
    assert bp == 1 and ts == 1 and dm // 256 == N_HEADS
    assert dm % CHUNK == 0 and dm // CHUNK == LANE
    hp = x_prompt.reshape(t, dm)
    hs = x_sample.reshape(nb, dm)
    outs = {n: [] for n in ("kp", "vp", "sp", "ks", "vs", "ss")}
    for l in range(depth):
        wr = jnp.concatenate([router_group_w[l],
                              jnp.transpose(router_expert_w[l], (1, 0, 2)).reshape(dm, N_EXPERTS)], axis=1)
        wr = jnp.pad(wr, ((0, 0), (0, LANE - wr.shape[1])))
        wr_hi = wr.astype(BF16)
        wr = jnp.stack([wr_hi, (wr - wr_hi.astype(F32)).astype(BF16)])
        br = jnp.pad(jnp.concatenate([router_group_b[l], router_expert_b[l].reshape(-1)]),
                     (0, LANE - N_GROUPS - N_EXPERTS)).reshape(1, LANE)
        row = lambda a: a[l].reshape(1, -1)
        w = dict(depth=depth, w_in=w_in[l], pool_w=pool_w[l], pool_scale=row(pool_scale),
                 lam_q1=row(lam_q1), lam_k1=row(lam_k1), lam_q2=row(lam_q2), lam_k2=row(lam_k2),
                 subln_g=row(subln_g), w_branch_pool=w_branch_pool[l], w_branch_attn=w_branch_attn[l],
                 w_out=w_out[l], ln1_g=row(ln1_g), ln1_b=row(ln1_b), router_w=wr, router_b=br,
                 expert_gate=expert_gate[l], expert_up=expert_up[l], expert_down=expert_down[l],
                 ple_proj=ple_proj[l], ple_gate_w=ple_gate_w[l], ple_gate_b=row(ple_gate_b),
                 ln2_g=row(ln2_g), ln2_b=row(ln2_b))
        state = state_pool[l]
        hp, hs, k_p, v_p, k_s, v_s, u = _layer(l, hp, hs, p_prompt[l, 0], p_sample[l, :, 0], cache_k[l],
                                               cache_v[l], state, page_table, w)
        outs["kp"].append(k_p[None])
        outs["vp"].append(v_p[None])
        outs["sp"].append(u[t - POOL_STATE:t].reshape(1, POOL_STATE, -1))
        outs["ks"].append(k_s.reshape(nb, 1, N_HEADS, 2, HEAD_DIM))
        outs["vs"].append(v_s.reshape(nb, 1, N_HEADS, HEAD_W))
        outs["ss"].append(jnp.concatenate([state[:, 1:], u[t:t + nb][:, None, :]], axis=1))
    st = lambda n: jnp.stack(outs[n])
    return (hp.reshape(1, t, dm), hs.reshape(nb, 1, dm), st("kp"), st("vp"), st("sp"),
            st("ks"), st("vs"), st("ss"))
```
